```python
import math
import jax, jax.numpy as jnp
from jax import lax
import numpy as np

D_MODEL = 1024
BATCH = 8
SEQ = 8192
DEPTH = 1

GRID_W = 64
CTX_LEN = 256
HEAD_DIM = 64
ATTN_SCALE = HEAD_DIM ** -0.5
ROPE_FREQS = HEAD_DIM // 4
ROPE_BASE = 10000.0
EPS = 1e-6
BLOCK = 128
A_Q_HEADS = 16
A_KV_HEADS = 2
A_GROUP = A_Q_HEADS // A_KV_HEADS
WINDOW = 128
A_WIDTH = A_Q_HEADS * HEAD_DIM
B_HEADS = 8
B_V_DIM = 2 * HEAD_DIM
B_WIDTH = B_HEADS * B_V_DIM
KA_W = A_KV_HEADS * HEAD_DIM
KB_W = B_HEADS * 2 * HEAD_DIM
KV_W = 2 * KA_W + KB_W + B_WIDTH
IN_W = KV_W + A_WIDTH + KB_W + 2 * D_MODEL
N_EXPERTS = 256
TOP_K = 8
N_GROUPS = 8
TOPK_GROUPS = 4
D_EXPERT = D_MODEL // 4
ROUTED_SCALE = 2.5
MOE_BLOCK = 128

kernel_name = "hybrid_dit_gqa_sink_diffattn_moe"


def rms_norm(x, g):
    xf = x.astype(jnp.float32)
    y = xf * lax.rsqrt(jnp.mean(xf * xf, axis=-1, keepdims=True) + EPS)
    return (y * g.astype(jnp.float32)).astype(x.dtype)


def rope_tables(n):
    rows = n // GRID_W
    row = jnp.broadcast_to(jnp.arange(rows)[:, None], (rows, GRID_W)).reshape(-1)
    col = jnp.broadcast_to(jnp.arange(GRID_W)[None, :], (rows, GRID_W)).reshape(-1)
    freqs = ROPE_BASE ** (-jnp.arange(ROPE_FREQS, dtype=jnp.float32) / ROPE_FREQS)
    pos = jnp.stack([row, col], axis=-1).astype(jnp.float32)
    ang = pos[:, :, None] * freqs
    return jnp.cos(ang), jnp.sin(ang)


def apply_rope(x, cos, sin):
    bshape = (x.shape[1],) + (1,) * (x.ndim - 3) + (2, ROPE_FREQS)
    c = cos.reshape(bshape)
    s = sin.reshape(bshape)
    xr = x.astype(jnp.float32).reshape(x.shape[:-1] + (2, 2, ROPE_FREQS))
    x1, x2 = xr[..., 0, :], xr[..., 1, :]
    out = jnp.stack([x1 * c - x2 * s, x2 * c + x1 * s], axis=-2)
    return out.reshape(x.shape).astype(x.dtype)


def sink_attend(q, k, v, sink, mask=None):
    s = jnp.einsum('bqkgd,bskd->bkgqs', q, k).astype(jnp.float32) * ATTN_SCALE
    if mask is not None:
        s = jnp.where(mask, s, -jnp.inf)
    sink_col = jnp.broadcast_to(sink.astype(jnp.float32)[None, :, :, None, None], s.shape[:-1] + (1,))
    p = jax.nn.softmax(jnp.concatenate([s, sink_col], axis=-1), axis=-1)[..., :-1]
    return jnp.einsum('bkgqs,bskd->bqkgd', p.astype(v.dtype), v)


def window_attention(q, k, v, kc, vc, sink):
    B, n = q.shape[:2]
    nb = n // BLOCK
    span = BLOCK + 2 * WINDOW
    pad = ((0, 0), (WINDOW, WINDOW), (0, 0), (0, 0))
    k_pad = jnp.pad(k, pad)
    v_pad = jnp.pad(v, pad)
    qb = jnp.moveaxis(q.reshape((B, nb, BLOCK) + q.shape[2:]), 1, 0)
    offs = jnp.arange(span) - WINDOW
    band = jnp.abs(jnp.arange(BLOCK)[:, None] - offs[None, :]) <= WINDOW
    ctx_ok = jnp.ones((BLOCK, kc.shape[1]), dtype=bool)

    def one_block(args):
        i, q_i = args
        start = i * BLOCK
        k_i = jnp.concatenate([lax.dynamic_slice_in_dim(k_pad, start, span, axis=1), kc], axis=1)
        v_i = jnp.concatenate([lax.dynamic_slice_in_dim(v_pad, start, span, axis=1), vc], axis=1)
        kpos = start + offs
        valid = band & ((kpos >= 0) & (kpos < n))[None, :]
        mask = jnp.concatenate([valid, ctx_ok], axis=1)
        return sink_attend(q_i, k_i, v_i, sink, mask)

    out = lax.map(one_block, (jnp.arange(nb), qb))
    return jnp.moveaxis(out, 0, 1).reshape(B, n, A_WIDTH)


def diff_attend(q, k, v, lam):
    s = jnp.einsum('bqhjd,bshjd->bhjqs', q, k).astype(jnp.float32) * ATTN_SCALE
    p = jax.nn.softmax(s, axis=-1)
    a = p[:, :, 0] - lam * p[:, :, 1]
    return jnp.einsum('bhqs,bshe->bqhe', a.astype(v.dtype), v)


def blocked_diff_attention(q, k, v, lam):
    B, n = q.shape[:2]
    nb = n // BLOCK
    qb = jnp.moveaxis(q.reshape((B, nb, BLOCK) + q.shape[2:]), 1, 0)
    out = lax.map(lambda q_i: diff_attend(q_i, k, v, lam), qb)
    return jnp.moveaxis(out, 0, 1).reshape((B, n) + out.shape[3:])


def mixer_kv(kv, p):
    B, S = kv.shape[:2]
    ka, va, kb, vb = jnp.split(kv, [KA_W, 2 * KA_W, 2 * KA_W + KB_W], axis=-1)
    ka = rms_norm(ka.reshape(B, S, A_KV_HEADS, HEAD_DIM), p['knorm_a'])
    va = va.reshape(B, S, A_KV_HEADS, HEAD_DIM)
    kb = rms_norm(kb.reshape(B, S, B_HEADS, 2, HEAD_DIM), p['knorm_b'])
    vb = vb.reshape(B, S, B_HEADS, B_V_DIM)
    return ka, va, kb, vb


def mixer_q(qg, p):
    B, S = qg.shape[:2]
    qa, qb, ga, gb = jnp.split(qg, [A_WIDTH, A_WIDTH + KB_W, A_WIDTH + KB_W + D_MODEL], axis=-1)
    qa = rms_norm(qa.reshape(B, S, A_KV_HEADS, A_GROUP, HEAD_DIM), p['qnorm_a'])
    qb = rms_norm(qb.reshape(B, S, B_HEADS, 2, HEAD_DIM), p['qnorm_b'])
    return qa, qb, jax.nn.sigmoid(ga), jax.nn.sigmoid(gb)


def merge_branches(oa, ob, ga, gb, p, lam_init):
    B, S = oa.shape[:2]
    ob = rms_norm(ob, p['subln_g']) * (1.0 - lam_init)
    ya = oa.reshape(B, S, A_WIDTH) @ p['w_pa']
    yb = ob.reshape(B, S, B_WIDTH) @ p['w_pb']
    return (ga * ya + gb * yb) @ p['w_o']


def token_mixers(h, hc, cos, sin, p, lam_init, update_ctx):
    proj = h @ p['w_in']
    ka, va, kb, vb = mixer_kv(proj[..., :KV_W], p)
    qa, qb, ga, gb = mixer_q(proj[..., KV_W:], p)
    ka, kb, qa, qb = (apply_rope(t, cos, sin) for t in (ka, kb, qa, qb))
    proj_c = hc @ (p['w_in'] if update_ctx else p['w_in'][:, :KV_W])
    kac, vac, kbc, vbc = mixer_kv(proj_c[..., :KV_W], p)
    oa = window_attention(qa, ka, va, kac, vac, p['sink_a'])
    ob = blocked_diff_attention(qb, jnp.concatenate([kb, kbc], axis=1),
                                jnp.concatenate([vb, vbc], axis=1), p['lam'])
    out = merge_branches(oa, ob, ga, gb, p, lam_init)
    out_c = None
    if update_ctx:
        qac, qbc, gac, gbc = mixer_q(proj_c[..., KV_W:], p)
        oac = sink_attend(qac, kac, vac, p['sink_a']).reshape(hc.shape[0], hc.shape[1], A_WIDTH)
        obc = diff_attend(qbc, kbc, vbc, p['lam'])
        out_c = merge_branches(oac, obc, gac, gbc, p, lam_init)
    return out, out_c


def swiglu(x, wg, wu, wd):
    return (jax.nn.silu(x @ wg) * (x @ wu)) @ wd


def moe(tokens, p):
    T, D = tokens.shape
    scores = jax.nn.sigmoid((tokens @ p['w_router']).astype(jnp.float32))
    biased = scores + p['router_bias'].astype(jnp.float32)
    grp = biased.reshape(T, N_GROUPS, N_EXPERTS // N_GROUPS)
    grp_score = lax.top_k(grp, 2)[0].sum(-1)
    _, top_g = lax.top_k(grp_score, TOPK_GROUPS)
    gmask = jax.nn.one_hot(top_g, N_GROUPS, dtype=jnp.float32).sum(-2) > 0
    emask = jnp.repeat(gmask, N_EXPERTS // N_GROUPS, axis=1)
    _, idx = lax.top_k(jnp.where(emask, biased, -jnp.inf), TOP_K)
    w = jnp.take_along_axis(scores, idx, axis=-1)
    w = w / jnp.sum(w, axis=-1, keepdims=True) * ROUTED_SCALE

    M = T * TOP_K
    e_flat = idx.reshape(M)
    tok_flat = jnp.arange(M) // TOP_K
    w_flat = w.reshape(M)
    order = jnp.argsort(e_flat)
    e_s, tok_s, w_s = e_flat[order], tok_flat[order], w_flat[order]
    counts = jnp.bincount(e_flat, length=N_EXPERTS)
    start = jnp.cumsum(counts) - counts
    padded = (counts + MOE_BLOCK - 1) // MOE_BLOCK * MOE_BLOCK
    pend = jnp.cumsum(padded)
    pstart = pend - padded
    dest = pstart[e_s] + (jnp.arange(M) - start[e_s])
    nblk = -(-M // MOE_BLOCK) + N_EXPERTS
    P = nblk * MOE_BLOCK
    buf_tok = jnp.zeros((P,), jnp.int32).at[dest].set(tok_s.astype(jnp.int32))
    buf_w = jnp.zeros((P,), jnp.float32).at[dest].set(w_s)
    blk_expert = jnp.minimum(jnp.searchsorted(pend, jnp.arange(nblk) * MOE_BLOCK, side='right'),
                             N_EXPERTS - 1)

    def body(b, y):
        rows = lax.dynamic_slice_in_dim(buf_tok, b * MOE_BLOCK, MOE_BLOCK)
        wts = lax.dynamic_slice_in_dim(buf_w, b * MOE_BLOCK, MOE_BLOCK)
        e = blk_expert[b]
        out = swiglu(tokens[rows], p['w_gate_e'][e], p['w_up_e'][e], p['w_down_e'][e])
        return y.at[rows].add(out.astype(jnp.float32) * wts[:, None])

    y = lax.fori_loop(0, nblk, body, jnp.zeros((T, D), jnp.float32))
    shared = swiglu(tokens, p['w_gate_s'], p['w_up_s'], p['w_down_s']).astype(jnp.float32)
    return (y + shared).astype(tokens.dtype)


def setup_inputs(seed: int = 0) -> dict:
    key = jax.random.key(seed)
    ks = jax.random.split(key, 32)
    L, D = DEPTH, D_MODEL

    def nrm(k, shape, s):
        return jax.random.normal(k, shape, jnp.float32) * s

    return {
        "x": nrm(ks[0], (BATCH, SEQ, D), 1.0),
        "c": nrm(ks[1], (BATCH, D), 1.0),
        "ctx": nrm(ks[2], (BATCH, CTX_LEN, D), 1.0),
        "c_ctx": nrm(ks[3], (D,), 1.0),
        "w_ada": nrm(ks[4], (L, D, 6 * D), 0.5 * D ** -0.5),
        "b_ada": nrm(ks[5], (L, 6 * D), 0.02),
        "norm1_g": 1.0 + nrm(ks[6], (L, D), 0.1),
        "norm2_g": 1.0 + nrm(ks[7], (L, D), 0.1),
        "w_in": nrm(ks[8], (L, D, IN_W), D ** -0.5),
        "qnorm_a": 1.0 + nrm(ks[9], (L, HEAD_DIM), 0.1),
        "knorm_a": 1.0 + nrm(ks[10], (L, HEAD_DIM), 0.1),
        "sink_a": nrm(ks[11], (L, A_KV_HEADS, A_GROUP), 0.5),
        "qnorm_b": 1.0 + nrm(ks[12], (L, HEAD_DIM), 0.1),
        "knorm_b": 1.0 + nrm(ks[13], (L, HEAD_DIM), 0.1),
        "lam_q1": nrm(ks[14], (L, HEAD_DIM), 0.1),
        "lam_k1": nrm(ks[15], (L, HEAD_DIM), 0.1),
        "lam_q2": nrm(ks[16], (L, HEAD_DIM), 0.1),
        "lam_k2": nrm(ks[17], (L, HEAD_DIM), 0.1),
        "subln_g": 1.0 + nrm(ks[18], (L, B_V_DIM), 0.1),
        "w_pa": nrm(ks[19], (L, A_WIDTH, D), A_WIDTH ** -0.5),
        "w_pb": nrm(ks[20], (L, B_WIDTH, D), B_WIDTH ** -0.5),
        "w_o": nrm(ks[21], (L, D, D), D ** -0.5),
        "w_router": nrm(ks[22], (L, D, N_EXPERTS), D ** -0.5),
        "router_bias": nrm(ks[23], (L, N_EXPERTS), 0.01),
        "w_gate_e": nrm(ks[24], (L, N_EXPERTS, D, D_EXPERT), D ** -0.5),
        "w_up_e": nrm(ks[25], (L, N_EXPERTS, D, D_EXPERT), D ** -0.5),
        "w_down_e": nrm(ks[26], (L, N_EXPERTS, D_EXPERT, D), D_EXPERT ** -0.5),
        "w_gate_s": nrm(ks[27], (L, D, D_EXPERT), D ** -0.5),
        "w_up_s": nrm(ks[28], (L, D, D_EXPERT), D ** -0.5),
        "w_down_s": nrm(ks[29], (L, D_EXPERT, D), D_EXPERT ** -0.5),
    }


def reference(x, c, ctx, c_ctx, w_ada, b_ada, norm1_g, norm2_g, w_in, qnorm_a, knorm_a, sink_a,
              qnorm_b, knorm_b, lam_q1, lam_k1, lam_q2, lam_k2, subln_g, w_pa, w_pb, w_o,
              w_router, router_bias, w_gate_e, w_up_e, w_down_e, w_gate_s, w_up_s, w_down_s):
    B, n, D = x.shape
    cos, sin = rope_tables(n)
    for l in range(DEPTH):
        update_ctx = l < DEPTH - 1
        lam_init = 0.8 - 0.6 * math.exp(-0.3 * l)
        lam = (jnp.exp(jnp.sum(lam_q1[l].astype(jnp.float32) * lam_k1[l].astype(jnp.float32)))
               - jnp.exp(jnp.sum(lam_q2[l].astype(jnp.float32) * lam_k2[l].astype(jnp.float32)))
               + lam_init)
        p = {
            'w_in': w_in[l], 'qnorm_a': qnorm_a[l], 'knorm_a': knorm_a[l], 'sink_a': sink_a[l],
            'qnorm_b': qnorm_b[l], 'knorm_b': knorm_b[l], 'lam': lam, 'subln_g': subln_g[l],
            'w_pa': w_pa[l], 'w_pb': w_pb[l], 'w_o': w_o[l],
            'w_router': w_router[l], 'router_bias': router_bias[l],
            'w_gate_e': w_gate_e[l], 'w_up_e': w_up_e[l], 'w_down_e': w_down_e[l],
            'w_gate_s': w_gate_s[l], 'w_up_s': w_up_s[l], 'w_down_s': w_down_s[l],
        }
        mod = (jax.nn.silu(c) @ w_ada[l] + b_ada[l])[:, None, :]
        mod_c = jax.nn.silu(c_ctx) @ w_ada[l] + b_ada[l]
        sh1, sc1, g1, sh2, sc2, g2 = jnp.split(mod, 6, axis=-1)
        csh1, csc1, cg1, csh2, csc2, cg2 = jnp.split(mod_c, 6, axis=-1)

        h = rms_norm(x, norm1_g[l]) * (1.0 + sc1) + sh1
        hc = rms_norm(ctx, norm1_g[l]) * (1.0 + csc1) + csh1
        mix, mix_c = token_mixers(h, hc, cos, sin, p, lam_init, update_ctx)
        x = x + g1 * mix
        h2 = rms_norm(x, norm2_g[l]) * (1.0 + sc2) + sh2
        if update_ctx:
            ctx = ctx + cg1 * mix_c
            h2c = rms_norm(ctx, norm2_g[l]) * (1.0 + csc2) + csh2
            y = moe(jnp.concatenate([h2.reshape(-1, D), h2c.reshape(-1, D)], axis=0), p)
            x = x + g2 * y[:B * n].reshape(B, n, D)
            ctx = ctx + cg2 * y[B * n:].reshape(ctx.shape)
        else:
            x = x + g2 * moe(h2.reshape(-1, D), p).reshape(B, n, D)
    return x
```

```python
import functools
import math

import jax
import jax.numpy as jnp
from jax import lax
from jax.experimental import pallas as pl
from jax.experimental.pallas import tpu as pltpu

F32 = jnp.float32
BF16 = jnp.bfloat16
I32 = jnp.int32
U32 = jnp.uint32

HEAD_DIM = 64
GRID_W = 64
ROPE_FREQS = HEAD_DIM // 4
ROPE_BASE = 10000.0
EPS = 1e-6
ATTN_SCALE = HEAD_DIM ** -0.5
BLOCK = 128
A_Q_HEADS = 16
A_KV_HEADS = 2
A_GROUP = A_Q_HEADS // A_KV_HEADS
B_HEADS = 8
N_EXPERTS = 256
TOP_K = 8
N_GROUPS = 8
TOPK_GROUPS = 4
GROUP_SIZE = N_EXPERTS // N_GROUPS
ROUTED_SCALE = 2.5
MOE_BLOCK = 128
LAM_INIT = 0.8 - 0.6 * math.exp(-0.3 * 0)

LANES = 128
MXU_N = 256
VMEM_LIMIT = 56 * 1024 * 1024
NEG_BIG = -1e30

LAT_KB, LAT_QA, LAT_QB, LAT_KA, LAT_VA, LAT_VB = 0, 8, 16, 24, 25, 26
LAT_COLS = 34 * LANES
LAT_NQK = 13
CTX_KB, CTX_KA, CTX_VA, CTX_VB = 0, 8, 9, 10
CTX_COLS = 18 * LANES
CTX_NQK = 5


def _tile(n, pref):
    return pref if n % pref == 0 else n


def _cparams(sem):
    return pltpu.CompilerParams(dimension_semantics=sem, vmem_limit_bytes=VMEM_LIMIT)


def _nt_dot(a, b):
    return lax.dot_general(a, b, (((1,), (1,)), ((), ())), preferred_element_type=F32)


def _dot(a, b):
    return jnp.dot(a, b, preferred_element_type=F32)


def _silu(x):
    return x * (1.0 / (1.0 + jnp.exp(-x)))


def _sigmoid(x):
    return 1.0 / (1.0 + jnp.exp(-x))


def _modulated_norm(xf, g_row, shift_row, scale_row):
    ms = jnp.mean(xf * xf, axis=-1, keepdims=True)
    y = xf * lax.rsqrt(ms + EPS) * g_row
    return y * (1.0 + scale_row) + shift_row


def _ada_kernel(c_ref, w_ref, b_ref, o_ref):
    o_ref[...] = _dot(_silu(c_ref[...]).astype(BF16), w_ref[...].astype(BF16)) + b_ref[...]


def _ada(cc, w_ada, b_ada):
    rows, d = cc.shape
    n = w_ada.shape[1]
    tn = _tile(n, 512)
    return pl.pallas_call(
        _ada_kernel,
        out_shape=jax.ShapeDtypeStruct((rows, n), F32),
        grid=(n // tn,),
        in_specs=[pl.BlockSpec((rows, d), lambda j: (0, 0)),
                  pl.BlockSpec((d, tn), lambda j: (0, j)),
                  pl.BlockSpec((1, tn), lambda j: (0, j))],
        out_specs=pl.BlockSpec((rows, tn), lambda j: (0, j)),
        compiler_params=_cparams(("parallel",)),
        name="ada",
    )(cc, w_ada, b_ada.reshape(1, n))


def _proj_kernel(x_ref, mod_ref, n1_ref, w_ref, gain_ref, flag_ref, seg_ref, cos_ref, sin_ref,
                 o_ref, h_ref, *, nqk, use_rope):
    j = pl.program_id(2)

    @pl.when(j == 0)
    def _():
        h = _modulated_norm(x_ref[0], n1_ref[...], mod_ref[0, 0:1, :], mod_ref[0, 1:2, :])
        h_ref[...] = h.astype(BF16)

    acc = _dot(h_ref[...], w_ref[...])

    @pl.when(j < nqk)
    def _():
        sq = acc * acc
        hi = sq.astype(BF16)
        lo = (sq - hi.astype(F32)).astype(BF16)
        ssum = _dot(hi, seg_ref[...]) + _dot(lo, seg_ref[...])
        y = acc * lax.rsqrt(ssum * (1.0 / HEAD_DIM) + EPS) * gain_ref[0]
        if use_rope:
            lane = lax.broadcasted_iota(I32, y.shape, 1)
            first = (lane % (2 * ROPE_FREQS)) < ROPE_FREQS
            rot = jnp.where(first, pltpu.roll(y, MXU_N - ROPE_FREQS, 1), pltpu.roll(y, ROPE_FREQS, 1))
            y = y * cos_ref[...] + rot * sin_ref[...]
        o_ref[0] = jnp.where(flag_ref[0] > 0.0, y, acc).astype(BF16)

    @pl.when(j >= nqk)
    def _():
        o_ref[0] = acc.astype(BF16)


def _proj(x, mod, n1, w, gains, flags, seg, cos_t, sin_t, *, nqk, use_rope):
    b, s, d = x.shape
    ncols = w.shape[1]
    tm = _tile(s, 1024)
    nj = ncols // MXU_N
    return pl.pallas_call(
        functools.partial(_proj_kernel, nqk=nqk, use_rope=use_rope),
        out_shape=jax.ShapeDtypeStruct((b, s, ncols), BF16),
        grid=(b, s // tm, nj),
        in_specs=[
            pl.BlockSpec((1, tm, d), lambda bi, si, j: (bi, si, 0)),
            pl.BlockSpec((1, 6, d), lambda bi, si, j: (bi, 0, 0)),
            pl.BlockSpec((1, d), lambda bi, si, j: (0, 0)),
            pl.BlockSpec((d, MXU_N), lambda bi, si, j: (0, j)),
            pl.BlockSpec((1, 1, MXU_N), lambda bi, si, j: (jnp.minimum(j, nqk - 1), 0, 0)),
            pl.BlockSpec((1, 1, MXU_N), lambda bi, si, j: (jnp.minimum(j, nqk - 1), 0, 0)),
            pl.BlockSpec((MXU_N, MXU_N), lambda bi, si, j: (0, 0)),
            pl.BlockSpec((tm, MXU_N), lambda bi, si, j: (si, 0)),
            pl.BlockSpec((tm, MXU_N), lambda bi, si, j: (si, 0)),
        ],
        out_specs=pl.BlockSpec((1, tm, MXU_N), lambda bi, si, j: (bi, si, j)),
        scratch_shapes=[pltpu.VMEM((tm, d), BF16)],
        compiler_params=_cparams(("parallel", "parallel", "arbitrary")),
        name="proj_rope" if use_rope else "proj_ctx",
    )(x, mod, n1, w, gains, flags, seg, cos_t, sin_t)


def _attn_a_kernel(sink_ref, q_ref, kp_ref, kc_ref, kn_ref, vp_ref, vc_ref, vn_ref, kx_ref, vx_ref, o_ref):
    i = pl.program_id(1)
    nb = pl.num_programs(1)
    nctx = kx_ref.shape[1]
    span = 3 * BLOCK + nctx
    kall = jnp.concatenate([kp_ref[0], kc_ref[0], kn_ref[0], kx_ref[0]], axis=0)
    vall = jnp.concatenate([vp_ref[0], vc_ref[0], vn_ref[0], vx_ref[0]], axis=0)

    r = lax.broadcasted_iota(I32, (BLOCK, span), 0)
    c = lax.broadcasted_iota(I32, (BLOCK, span), 1)
    prev_ok = (c < BLOCK) & (c >= r) & (i > 0)
    cur_ok = (c >= BLOCK) & (c < 2 * BLOCK)
    next_ok = (c >= 2 * BLOCK) & (c < 3 * BLOCK) & (c - 2 * BLOCK <= r) & (i < nb - 1)
    valid = prev_ok | cur_ok | next_ok | (c >= 3 * BLOCK)
    bias = jnp.where(valid, 0.0, NEG_BIG)

    lane = lax.broadcasted_iota(I32, (BLOCK, LANES), 1)
    for kv in range(A_KV_HEADS):
        in_half = (lane >= kv * HEAD_DIM) & (lane < (kv + 1) * HEAD_DIM)
        qs = []
        for g in range(A_GROUP):
            hh = kv * A_GROUP + g
            blk = q_ref[0, :, (hh // 2) * LANES:(hh // 2 + 1) * LANES].astype(F32)
            if hh % 2 != kv:
                blk = pltpu.roll(blk, HEAD_DIM, 1)
            qs.append(jnp.where(in_half, blk, 0.0).astype(BF16))
        s_all = _nt_dot(jnp.concatenate(qs, axis=0), kall)
        ps, ls = [], []
        for g in range(A_GROUP):
            sink = sink_ref[kv, g]
            s = s_all[g * BLOCK:(g + 1) * BLOCK] + bias
            m = jnp.maximum(jnp.max(s, axis=-1, keepdims=True), sink)
            p = jnp.exp(s - m)
            ls.append(jnp.sum(p, axis=-1, keepdims=True) + jnp.exp(sink - m))
            ps.append(p.astype(BF16))
        o_all = _dot(jnp.concatenate(ps, axis=0), vall)
        os_ = [o_all[g * BLOCK:(g + 1) * BLOCK] / ls[g] for g in range(A_GROUP)]
        for pair in range(A_GROUP // 2):
            o0, o1 = os_[2 * pair], os_[2 * pair + 1]
            if kv == 0:
                o1 = pltpu.roll(o1, HEAD_DIM, 1)
            else:
                o0 = pltpu.roll(o0, HEAD_DIM, 1)
            col = (kv * (A_GROUP // 2) + pair) * LANES
            o_ref[0, :, col:col + LANES] = jnp.where(lane < HEAD_DIM, o0, o1).astype(BF16)


def _attn_a(sink, lat, ctxp):
    b, s, _ = lat.shape
    nctx = ctxp.shape[1]
    nb = s // BLOCK
    width = A_Q_HEADS * HEAD_DIM

    def kspec(col, shift):
        return pl.BlockSpec((1, BLOCK, LANES),
                            lambda bi, i: (bi, jnp.clip(i + shift, 0, nb - 1), col))

    return pl.pallas_call(
        _attn_a_kernel,
        out_shape=jax.ShapeDtypeStruct((b, s, width), BF16),
        grid=(b, nb),
        in_specs=[
            pl.BlockSpec(memory_space=pltpu.SMEM),
            pl.BlockSpec((1, BLOCK, width), lambda bi, i: (bi, i, LAT_QA * LANES // width)),
            kspec(LAT_KA, -1), kspec(LAT_KA, 0), kspec(LAT_KA, 1),
            kspec(LAT_VA, -1), kspec(LAT_VA, 0), kspec(LAT_VA, 1),
            pl.BlockSpec((1, nctx, LANES), lambda bi, i: (bi, 0, CTX_KA)),
            pl.BlockSpec((1, nctx, LANES), lambda bi, i: (bi, 0, CTX_VA)),
        ],
        out_specs=pl.BlockSpec((1, BLOCK, width), lambda bi, i: (bi, i, 0)),
        compiler_params=_cparams(("parallel", "parallel")),
        name="attn_a",
    )(sink, lat, lat, lat, lat, lat, lat, lat, ctxp, ctxp)


def _attn_b_kernel(lamv_ref, sub_ref, q_ref, k_ref, v_ref, kx_ref, vx_ref, o_ref, acc_ref, m_ref, l_ref, *, tk):
    tq = q_ref.shape[1]
    s_len = k_ref.shape[1]
    qf = q_ref[0].astype(F32)
    lane = lax.broadcasted_iota(I32, (tq, LANES), 1)
    qj = [jnp.where(lane < HEAD_DIM, qf, 0.0).astype(BF16), jnp.where(lane >= HEAD_DIM, qf, 0.0).astype(BF16)]

    m_ref[...] = jnp.full(m_ref.shape, NEG_BIG, F32)
    l_ref[...] = jnp.zeros(l_ref.shape, F32)
    acc_ref[...] = jnp.zeros(acc_ref.shape, F32)

    def step(kc, vc):
        for j in range(2):
            s = _nt_dot(qj[j], kc)
            m_old = m_ref[j]
            m_new = jnp.maximum(m_old, jnp.max(s, axis=-1, keepdims=True))
            alpha = jnp.exp(m_old - m_new)
            p = jnp.exp(s - m_new)
            l_ref[j] = alpha * l_ref[j] + jnp.sum(p, axis=-1, keepdims=True)
            acc_ref[j] = alpha * acc_ref[j] + _dot(p.astype(BF16), vc)
            m_ref[j] = m_new

    def body(ci, carry):
        off = pl.multiple_of(ci * tk, tk)
        step(k_ref[0, pl.ds(off, tk), :], v_ref[0, pl.ds(off, tk), :])
        return carry

    lax.fori_loop(0, s_len // tk, body, 0)
    step(kx_ref[0], vx_ref[0])

    lv = lamv_ref[...]
    lam = (jnp.exp(jnp.sum(lv[0:1] * lv[1:2], axis=-1, keepdims=True))
           - jnp.exp(jnp.sum(lv[2:3] * lv[3:4], axis=-1, keepdims=True)) + LAM_INIT)
    o = acc_ref[0] / l_ref[0] - lam * (acc_ref[1] / l_ref[1])
    ms = jnp.mean(o * o, axis=-1, keepdims=True)
    o_ref[0] = (o * lax.rsqrt(ms + EPS) * sub_ref[...] * (1.0 - LAM_INIT)).astype(BF16)


def _attn_b(lamv, subln, lat, ctxp):
    b, s, _ = lat.shape
    nctx = ctxp.shape[1]
    tq = _tile(s, 512)
    tk = _tile(s, 512)
    return pl.pallas_call(
        functools.partial(_attn_b_kernel, tk=tk),
        out_shape=jax.ShapeDtypeStruct((b, s, B_HEADS * LANES), BF16),
        grid=(b, B_HEADS, s // tq),
        in_specs=[
            pl.BlockSpec((4, HEAD_DIM), lambda bi, h, qi: (0, 0)),
            pl.BlockSpec((1, LANES), lambda bi, h, qi: (0, 0)),
            pl.BlockSpec((1, tq, LANES), lambda bi, h, qi: (bi, qi, LAT_QB + h)),
            pl.BlockSpec((1, s, LANES), lambda bi, h, qi: (bi, 0, LAT_KB + h)),
            pl.BlockSpec((1, s, LANES), lambda bi, h, qi: (bi, 0, LAT_VB + h)),
            pl.BlockSpec((1, nctx, LANES), lambda bi, h, qi: (bi, 0, CTX_KB + h)),
            pl.BlockSpec((1, nctx, LANES), lambda bi, h, qi: (bi, 0, CTX_VB + h)),
        ],
        out_specs=pl.BlockSpec((1, tq, LANES), lambda bi, h, qi: (bi, qi, h)),
        scratch_shapes=[pltpu.VMEM((2, tq, LANES), F32), pltpu.VMEM((2, tq, 1), F32), pltpu.VMEM((2, tq, 1), F32)],
        compiler_params=_cparams(("parallel", "parallel", "arbitrary")),
        name="attn_b",
    )(lamv, subln, lat, lat, lat, ctxp, ctxp)


def _pack_rows(h):
    n = h.shape[1] // 2
    bits = pltpu.bitcast(h.astype(BF16).astype(F32), U32)
    return (bits[:, :n] >> 16) | (bits[:, n:] & jnp.uint32(0xFFFF0000))


def _unpack_rows(w):
    lo = pltpu.bitcast(w << 16, F32)
    hi = pltpu.bitcast(w & jnp.uint32(0xFFFF0000), F32)
    return jnp.concatenate([lo, hi], axis=1).astype(BF16)


def _merge_kernel(x_ref, mod_ref, n1_ref, n2_ref, oa_ref, ob_ref, wg_ref, wpa_ref, wpb_ref, wo_ref, wr_ref,
                  x1_ref, h2_ref, lg_ref):
    d = x_ref.shape[2]
    xf = x_ref[0]
    h = _modulated_norm(xf, n1_ref[...], mod_ref[0, 0:1, :], mod_ref[0, 1:2, :]).astype(BF16)
    gates = _sigmoid(_dot(h, wg_ref[...]))
    ya = _dot(oa_ref[0], wpa_ref[...])
    yb = _dot(ob_ref[0], wpb_ref[...])
    z = gates[:, :d] * ya + gates[:, d:] * yb
    x1 = xf + mod_ref[0, 2:3, :] * _dot(z.astype(BF16), wo_ref[...])
    x1_ref[0] = x1
    h2 = _modulated_norm(x1, n2_ref[...], mod_ref[0, 3:4, :], mod_ref[0, 4:5, :])
    lg_ref[0] = _dot(h2.astype(BF16), wr_ref[...])
    h2_ref[0] = _pack_rows(h2)


def _merge(x, mod, n1, n2, oa, ob, wg, wpa, wpb, wo, wr):
    b, s, d = x.shape
    tm = _tile(s, 256)
    ne = wr.shape[1]
    const = lambda bi, si: (0, 0)
    row = lambda bi, si: (bi, si, 0)
    return pl.pallas_call(
        _merge_kernel,
        out_shape=(jax.ShapeDtypeStruct((b, s, d), F32),
                   jax.ShapeDtypeStruct((b, s, d // 2), U32),
                   jax.ShapeDtypeStruct((b, s, ne), F32)),
        grid=(b, s // tm),
        in_specs=[
            pl.BlockSpec((1, tm, d), row),
            pl.BlockSpec((1, 6, d), lambda bi, si: (bi, 0, 0)),
            pl.BlockSpec((1, d), const), pl.BlockSpec((1, d), const),
            pl.BlockSpec((1, tm, d), row), pl.BlockSpec((1, tm, d), row),
            pl.BlockSpec((d, 2 * d), const), pl.BlockSpec((d, d), const),
            pl.BlockSpec((d, d), const), pl.BlockSpec((d, d), const),
            pl.BlockSpec((d, ne), const),
        ],
        out_specs=(pl.BlockSpec((1, tm, d), row), pl.BlockSpec((1, tm, d // 2), row),
                   pl.BlockSpec((1, tm, ne), row)),
        compiler_params=_cparams(("parallel", "parallel")),
        name="merge",
    )(x, mod, n1, n2, oa, ob, wg, wpa, wpb, wo, wr)


def _router_kernel(lg_ref, bias_ref, idx_ref, w_ref, rank_ref, cnt_ref, run_ref):
    i = pl.program_id(0)
    tr = lg_ref.shape[0]

    @pl.when(i == 0)
    def _():
        run_ref[...] = jnp.zeros(run_ref.shape, F32)

    scores = _sigmoid(lg_ref[...])
    biased = scores + bias_ref[...]
    lane = lax.broadcasted_iota(I32, scores.shape, 1)
    neg_inf = -jnp.inf

    def first_argmax(vals):
        m = jnp.max(vals, axis=-1, keepdims=True)
        idx = jnp.min(jnp.where(vals == m, lane, N_EXPERTS), axis=-1, keepdims=True)
        return m, idx

    gscore = []
    for g in range(N_GROUPS):
        in_g = (lane >= g * GROUP_SIZE) & (lane < (g + 1) * GROUP_SIZE)
        vals = jnp.where(in_g, biased, neg_inf)
        m1, i1 = first_argmax(vals)
        m2 = jnp.max(jnp.where(lane == i1, neg_inf, vals), axis=-1, keepdims=True)
        gscore.append(m1 + m2)
    emask = jnp.zeros(scores.shape, jnp.bool_)
    for g in range(N_GROUPS):
        beaten = jnp.zeros((tr, 1), F32)
        for g2 in range(N_GROUPS):
            if g2 == g:
                continue
            wins = (gscore[g2] > gscore[g]) | ((gscore[g2] == gscore[g]) & (g2 < g))
            beaten = beaten + jnp.where(wins, 1.0, 0.0)
        in_g = (lane >= g * GROUP_SIZE) & (lane < (g + 1) * GROUP_SIZE)
        emask = emask | (in_g & (beaten < TOPK_GROUPS))

    masked = jnp.where(emask, biased, neg_inf)
    sel = jnp.zeros(scores.shape, F32)
    idxs, ws = [], []
    for _ in range(TOP_K):
        _, ik = first_argmax(masked)
        hit = lane == ik
        idxs.append(ik)
        ws.append(jnp.sum(jnp.where(hit, scores, 0.0), axis=-1, keepdims=True))
        sel = jnp.where(hit, 1.0, sel)
        masked = jnp.where(hit, neg_inf, masked)
    wsum = ws[0]
    for k in range(1, TOP_K):
        wsum = wsum + ws[k]

    rr = lax.broadcasted_iota(I32, (tr, tr), 0)
    cc = lax.broadcasted_iota(I32, (tr, tr), 1)
    lower = jnp.where(cc < rr, 1.0, 0.0).astype(BF16)
    before = _dot(lower, sel.astype(BF16)) + run_ref[...]
    ranks = [jnp.sum(jnp.where(lane == idxs[k], before, 0.0), axis=-1, keepdims=True) for k in range(TOP_K)]
    run_ref[...] = run_ref[...] + jnp.sum(sel, axis=0, keepdims=True)
    cnt_ref[...] = run_ref[...]

    idx_ref[...] = jnp.concatenate(idxs, axis=1)
    w_ref[...] = jnp.concatenate([w / wsum * ROUTED_SCALE for w in ws], axis=1)
    rank_ref[...] = jnp.concatenate(ranks, axis=1).astype(I32)


def _router(logits, bias):
    t, ne = logits.shape
    tr = _tile(t, 512)
    return pl.pallas_call(
        _router_kernel,
        out_shape=(jax.ShapeDtypeStruct((t, TOP_K), I32), jax.ShapeDtypeStruct((t, TOP_K), F32),
                   jax.ShapeDtypeStruct((t, TOP_K), I32), jax.ShapeDtypeStruct((1, ne), F32)),
        grid=(t // tr,),
        in_specs=[pl.BlockSpec((tr, ne), lambda i: (i, 0)), pl.BlockSpec((1, ne), lambda i: (0, 0))],
        out_specs=(pl.BlockSpec((tr, TOP_K), lambda i: (i, 0)), pl.BlockSpec((tr, TOP_K), lambda i: (i, 0)),
                   pl.BlockSpec((tr, TOP_K), lambda i: (i, 0)), pl.BlockSpec((1, ne), lambda i: (0, 0))),
        scratch_shapes=[pltpu.VMEM((1, ne), F32)],
        compiler_params=_cparams(("arbitrary",)),
        name="router",
    )(logits, bias)


def _dest_kernel(idx_ref, rank_ref, pstart_ref, dest_ref):
    idx = idx_ref[...]
    lane = lax.broadcasted_iota(I32, (idx.shape[0], N_EXPERTS), 1)
    cols = []
    for k in range(TOP_K):
        start = jnp.sum(jnp.where(lane == idx[:, k:k + 1], pstart_ref[...], 0.0), axis=-1, keepdims=True)
        cols.append(start.astype(I32) + rank_ref[:, k:k + 1])
    dest_ref[...] = jnp.concatenate(cols, axis=1)


def _dest(idx, rank, pstart):
    t = idx.shape[0]
    tr = _tile(t, 512)
    return pl.pallas_call(
        _dest_kernel,
        out_shape=jax.ShapeDtypeStruct((t, TOP_K), I32),
        grid=(t // tr,),
        in_specs=[pl.BlockSpec((tr, TOP_K), lambda i: (i, 0)), pl.BlockSpec((tr, TOP_K), lambda i: (i, 0)),
                  pl.BlockSpec((1, N_EXPERTS), lambda i: (0, 0))],
        out_specs=pl.BlockSpec((tr, TOP_K), lambda i: (i, 0)),
        compiler_params=_cparams(("parallel",)),
        name="dest",
    )(idx, rank, pstart)


def _dispatch_kernel(dest_hbm, h2_ref, xs_in, xs_out, dest_smem, sem_idx, sem_rows):
    del xs_in
    i = pl.program_id(0)
    td = h2_ref.shape[0]
    n = td * TOP_K
    cp = pltpu.make_async_copy(dest_hbm.at[pl.ds(i * n, n)], dest_smem, sem_idx)
    cp.start()
    cp.wait()

    def row_copy(t, d):
        return pltpu.make_async_copy(h2_ref.at[pl.ds(t, 1), :], xs_out.at[pl.ds(d, 1), :], sem_rows)

    def issue(t, carry):
        for k in range(TOP_K):
            row_copy(t, dest_smem[t * TOP_K + k]).start()
        return carry

    lax.fori_loop(0, td, issue, 0)

    def drain(t, carry):
        for k in range(TOP_K):
            row_copy(0, 0).wait()
        return carry

    lax.fori_loop(0, td, drain, 0)


def _dispatch(dest_flat, h2p, xs_init):
    t, half = h2p.shape
    td = _tile(t, 256)
    return pl.pallas_call(
        _dispatch_kernel,
        out_shape=jax.ShapeDtypeStruct(xs_init.shape, U32),
        grid=(t // td,),
        in_specs=[pl.BlockSpec(memory_space=pl.ANY),
                  pl.BlockSpec((td, half), lambda i: (i, 0)),
                  pl.BlockSpec(memory_space=pl.ANY)],
        out_specs=pl.BlockSpec(memory_space=pl.ANY),
        scratch_shapes=[pltpu.SMEM((td * TOP_K,), I32), pltpu.SemaphoreType.DMA, pltpu.SemaphoreType.DMA],
        input_output_aliases={2: 0},
        compiler_params=pltpu.CompilerParams(dimension_semantics=("arbitrary",), vmem_limit_bytes=VMEM_LIMIT,
                                             has_side_effects=True),
        name="dispatch",
    )(dest_flat, h2p, xs_init)


def _experts_kernel(be_ref, nused_ref, xs_ref, wg_ref, wu_ref, wd_ref, ys_ref):
    del be_ref
    blk = pl.program_id(0)

    @pl.when(blk < nused_ref[0])
    def _():
        x = _unpack_rows(xs_ref[...])
        g = _dot(x, wg_ref[0].astype(BF16))
        u = _dot(x, wu_ref[0].astype(BF16))
        a = (_silu(g) * u).astype(BF16)
        ys_ref[...] = _dot(a, wd_ref[0].astype(BF16))

    @pl.when(blk >= nused_ref[0])
    def _():
        ys_ref[...] = jnp.zeros(ys_ref.shape, F32)


def _experts(blk_expert, nused, xs, wg, wu, wd):
    p, half = xs.shape
    d = 2 * half
    de = wg.shape[2]
    nblk = p // MOE_BLOCK
    grid_spec = pltpu.PrefetchScalarGridSpec(
        num_scalar_prefetch=2,
        grid=(nblk,),
        in_specs=[
            pl.BlockSpec((MOE_BLOCK, half), lambda i, be, nu: (i, 0)),
            pl.BlockSpec((1, d, de), lambda i, be, nu: (be[i], 0, 0)),
            pl.BlockSpec((1, d, de), lambda i, be, nu: (be[i], 0, 0)),
            pl.BlockSpec((1, de, d), lambda i, be, nu: (be[i], 0, 0)),
        ],
        out_specs=pl.BlockSpec((MOE_BLOCK, d), lambda i, be, nu: (i, 0)),
    )
    return pl.pallas_call(
        _experts_kernel,
        out_shape=jax.ShapeDtypeStruct((p, d), F32),
        grid_spec=grid_spec,
        compiler_params=_cparams(("arbitrary",)),
        name="experts",
    )(blk_expert, nused, xs, wg, wu, wd)


def _combine_kernel(dest_hbm, w_ref, ys_hbm, h2_ref, x1_ref, mod_ref, wgs_ref, wus_ref, wds_ref, o_ref,
                    dest_smem, buf, sem_idx, sem_rows):
    i = pl.program_id(0)
    tc = h2_ref.shape[0]
    n = tc * TOP_K
    cp = pltpu.make_async_copy(dest_hbm.at[pl.ds(i * n, n)], dest_smem, sem_idx)
    cp.start()
    cp.wait()

    def row_copy(t, k, d):
        return pltpu.make_async_copy(ys_hbm.at[pl.ds(d, 1), :], buf.at[k, pl.ds(t, 1), :], sem_rows)

    def issue(t, carry):
        for k in range(TOP_K):
            row_copy(t, k, dest_smem[t * TOP_K + k]).start()
        return carry

    lax.fori_loop(0, tc, issue, 0)

    x = _unpack_rows(h2_ref[...])
    a = (_silu(_dot(x, wgs_ref[...])) * _dot(x, wus_ref[...])).astype(BF16)
    y = _dot(a, wds_ref[...])

    def drain(t, carry):
        for k in range(TOP_K):
            row_copy(0, k, 0).wait()
        return carry

    lax.fori_loop(0, tc, drain, 0)

    for k in range(TOP_K):
        y = y + buf[k] * w_ref[:, k:k + 1]
    o_ref[...] = x1_ref[...] + mod_ref[0, 5:6, :] * y


def _combine(dest_flat, w, ys, h2p, x1, mod, wgs, wus, wds, seq):
    t, d = x1.shape
    half = d // 2
    tc = _tile(seq, 128)
    de = wgs.shape[1]
    per_batch = seq // tc
    const = lambda i: (0, 0)
    return pl.pallas_call(
        _combine_kernel,
        out_shape=jax.ShapeDtypeStruct((t, d), F32),
        grid=(t // tc,),
        in_specs=[
            pl.BlockSpec(memory_space=pl.ANY),
            pl.BlockSpec((tc, TOP_K), lambda i: (i, 0)),
            pl.BlockSpec(memory_space=pl.ANY),
            pl.BlockSpec((tc, half), lambda i: (i, 0)),
            pl.BlockSpec((tc, d), lambda i: (i, 0)),
            pl.BlockSpec((1, 6, d), lambda i: (i // per_batch, 0, 0)),
            pl.BlockSpec((d, de), const), pl.BlockSpec((d, de), const), pl.BlockSpec((de, d), const),
        ],
        out_specs=pl.BlockSpec((tc, d), lambda i: (i, 0)),
        scratch_shapes=[pltpu.SMEM((tc * TOP_K,), I32), pltpu.VMEM((TOP_K, tc, d), F32),
                        pltpu.SemaphoreType.DMA, pltpu.SemaphoreType.DMA],
        compiler_params=_cparams(("arbitrary",)),
        name="combine",
    )(dest_flat, w, ys, h2p, x1, mod, wgs, wus, wds)


def _rope_tables(n):
    rows = n // GRID_W
    row = jnp.broadcast_to(jnp.arange(rows)[:, None], (rows, GRID_W)).reshape(-1)
    col = jnp.broadcast_to(jnp.arange(GRID_W)[None, :], (rows, GRID_W)).reshape(-1)
    freqs = ROPE_BASE ** (-jnp.arange(ROPE_FREQS, dtype=F32) / ROPE_FREQS)
    pos = jnp.stack([row, col], axis=-1).astype(F32)
    ang = pos[:, :, None] * freqs
    cos, sin = jnp.cos(ang), jnp.sin(ang)
    cos64 = jnp.concatenate([cos[:, 0], cos[:, 0], cos[:, 1], cos[:, 1]], axis=-1)
    sin64 = jnp.concatenate([-sin[:, 0], sin[:, 0], -sin[:, 1], sin[:, 1]], axis=-1)
    reps = MXU_N // HEAD_DIM
    return jnp.tile(cos64, (1, reps)), jnp.tile(sin64, (1, reps))


def _gain_rows(qn_a, kn_a, qn_b, kn_b):
    reps = MXU_N // HEAD_DIM
    ones = jnp.ones((LANES,), F32)
    kb = jnp.tile(kn_b, reps)
    qa = jnp.tile(qn_a, reps) * ATTN_SCALE
    qb = jnp.tile(qn_b, reps) * ATTN_SCALE
    mixed = jnp.concatenate([jnp.tile(kn_a, LANES // HEAD_DIM), ones])
    mixed_flag = jnp.concatenate([ones, 0.0 * ones])
    full_flag = jnp.ones((MXU_N,), F32)
    lat_g = jnp.stack([kb] * 4 + [qa] * 4 + [qb] * 4 + [mixed])[:, None, :]
    lat_f = jnp.stack([full_flag] * 12 + [mixed_flag])[:, None, :]
    ctx_g = jnp.stack([kb] * 4 + [mixed])[:, None, :]
    ctx_f = jnp.stack([full_flag] * 4 + [mixed_flag])[:, None, :]
    return lat_g, lat_f, ctx_g, ctx_f


def kernel(x, c, ctx, c_ctx, w_ada, b_ada, norm1_g, norm2_g, w_in, qnorm_a, knorm_a, sink_a, qnorm_b, knorm_b,
           lam_q1, lam_k1, lam_q2, lam_k2, subln_g, w_pa, w_pb, w_o, w_router, router_bias, w_gate_e, w_up_e,
           w_down_e, w_gate_s, w_up_s, w_down_s):
    b, s, d = x.shape
    nctx = ctx.shape[1]
    assert w_ada.shape[0] == 1 and d == 1024 and s % BLOCK == 0 and s % GRID_W == 0
    t = b * s

    cc = jnp.concatenate([c, c_ctx[None, :], jnp.zeros((16 - b - 1, d), F32)], axis=0)
    mod_all = _ada(cc, w_ada[0], b_ada[0])
    mod = mod_all[:b].reshape(b, 6, d)
    mod_c = jnp.broadcast_to(mod_all[b].reshape(1, 6, d), (b, 6, d))

    w = w_in[0]
    ka_w, va_w = w[:, 0:128], w[:, 128:256]
    kb_w, vb_w = w[:, 256:1280], w[:, 1280:2304]
    qa_w, qb_w = w[:, 2304:3328], w[:, 3328:4352]
    gate_w = w[:, 4352:6400].astype(BF16)
    w_lat = jnp.concatenate([kb_w, qa_w, qb_w, ka_w, va_w, vb_w], axis=1).astype(BF16)
    w_ctx = jnp.concatenate([kb_w, ka_w, va_w, vb_w], axis=1).astype(BF16)

    lat_g, lat_f, ctx_g, ctx_f = _gain_rows(qnorm_a[0], knorm_a[0], qnorm_b[0], knorm_b[0])
    hid = jnp.arange(MXU_N) // HEAD_DIM
    seg = (hid[:, None] == hid[None, :]).astype(BF16)
    cos_t, sin_t = _rope_tables(s)
    n1 = norm1_g[0].reshape(1, d)
    n2 = norm2_g[0].reshape(1, d)

    lat = _proj(x, mod, n1, w_lat, lat_g, lat_f, seg, cos_t, sin_t, nqk=LAT_NQK, use_rope=True)
    ctxp = _proj(ctx, mod_c, n1, w_ctx, ctx_g, ctx_f, seg, cos_t[:nctx], sin_t[:nctx], nqk=CTX_NQK, use_rope=False)

    oa = _attn_a(sink_a[0], lat, ctxp)
    lamv = jnp.stack([lam_q1[0], lam_k1[0], lam_q2[0], lam_k2[0]])
    ob = _attn_b(lamv, subln_g[0].reshape(1, LANES), lat, ctxp)

    x1, h2p, logits = _merge(x, mod, n1, n2, oa, ob, gate_w, w_pa[0].astype(BF16), w_pb[0].astype(BF16),
                             w_o[0].astype(BF16), w_router[0].astype(BF16))
    x1 = x1.reshape(t, d)
    h2p = h2p.reshape(t, d // 2)
    logits = logits.reshape(t, N_EXPERTS)

    idx, wts, rank, counts = _router(logits, router_bias[0].reshape(1, N_EXPERTS))

    cnt = counts[0].astype(I32)
    padded = (cnt + MOE_BLOCK - 1) // MOE_BLOCK * MOE_BLOCK
    pend = jnp.cumsum(padded)
    pstart = pend - padded
    nblk = -(-(t * TOP_K) // MOE_BLOCK) + N_EXPERTS
    blk_expert = jnp.minimum(jnp.searchsorted(pend, jnp.arange(nblk, dtype=I32) * MOE_BLOCK, side='right'),
                             N_EXPERTS - 1).astype(I32)
    nused = (pend[-1:] // MOE_BLOCK).astype(I32)

    dest = _dest(idx, rank, pstart.astype(F32).reshape(1, N_EXPERTS)).reshape(t * TOP_K)
    xs = _dispatch(dest, h2p, jnp.zeros((nblk * MOE_BLOCK, d // 2), U32))
    ys = _experts(blk_expert, nused, xs, w_gate_e[0], w_up_e[0], w_down_e[0])
    out = _combine(dest, wts, ys, h2p, x1, mod, w_gate_s[0].astype(BF16), w_up_s[0].astype(BF16),
                   w_down_s[0].astype(BF16), s)
    return out.reshape(b, s, d)
```

```python
import functools
import math

import jax
import jax.numpy as jnp
from jax import lax
from jax.experimental import pallas as pl
from jax.experimental.pallas import tpu as pltpu

F32 = jnp.float32
BF16 = jnp.bfloat16
I32 = jnp.int32
U32 = jnp.uint32

HEAD_DIM = 64
GRID_W = 64
ROPE_FREQS = HEAD_DIM // 4
ROPE_BASE = 10000.0
EPS = 1e-6
ATTN_SCALE = HEAD_DIM ** -0.5
BLOCK = 128
A_Q_HEADS = 16
A_KV_HEADS = 2
A_GROUP = A_Q_HEADS // A_KV_HEADS
B_HEADS = 8
N_EXPERTS = 256
TOP_K = 8
N_GROUPS = 8
TOPK_GROUPS = 4
GROUP_SIZE = N_EXPERTS // N_GROUPS
ROUTED_SCALE = 2.5
MOE_BLOCK = 128
LAM_INIT = 0.8 - 0.6 * math.exp(-0.3 * 0)

LANES = 128
MXU_N = 256
VMEM_LIMIT = 56 * 1024 * 1024
NEG_BIG = -1e30

LAT_KB, LAT_QA, LAT_QB, LAT_KA, LAT_VA = 0, 8, 16, 24, 25
LAT_NQK = 13
CTX_KB, CTX_KA, CTX_VA = 0, 8, 9
CTX_NQK = 5


def _tile(n, pref):
    return pref if n % pref == 0 else n


def _cparams(sem):
    return pltpu.CompilerParams(dimension_semantics=sem, vmem_limit_bytes=VMEM_LIMIT)


def _nt_dot(a, b):
    return lax.dot_general(a, b, (((1,), (1,)), ((), ())), preferred_element_type=F32)


def _dot(a, b):
    return jnp.dot(a, b, preferred_element_type=F32)


def _silu(x):
    return x * (1.0 / (1.0 + jnp.exp(-x)))


def _sigmoid(x):
    return 1.0 / (1.0 + jnp.exp(-x))


def _modulated_norm(xf, g_row, shift_row, scale_row):
    ms = jnp.mean(xf * xf, axis=-1, keepdims=True)
    y = xf * lax.rsqrt(ms + EPS) * g_row
    return y * (1.0 + scale_row) + shift_row


def _ada_kernel(c_ref, w_ref, b_ref, o_ref):
    o_ref[...] = _dot(_silu(c_ref[...]).astype(BF16), w_ref[...].astype(BF16)) + b_ref[...]


def _ada(cc, w_ada, b_ada):
    rows, d = cc.shape
    n = w_ada.shape[1]
    tn = _tile(n, 512)
    return pl.pallas_call(
        _ada_kernel,
        out_shape=jax.ShapeDtypeStruct((rows, n), F32),
        grid=(n // tn,),
        in_specs=[pl.BlockSpec((rows, d), lambda j: (0, 0)),
                  pl.BlockSpec((d, tn), lambda j: (0, j)),
                  pl.BlockSpec((1, tn), lambda j: (0, j))],
        out_specs=pl.BlockSpec((rows, tn), lambda j: (0, j)),
        compiler_params=_cparams(("parallel",)),
        name="ada",
    )(cc, w_ada, b_ada.reshape(1, n))


def _proj_kernel(x_ref, mod_ref, n1_ref, w_ref, gain_ref, flag_ref, seg_ref, cos_ref, sin_ref,
                 o_ref, vt_ref, h_ref, *, nqk, use_rope):
    j = pl.program_id(2)

    @pl.when(j == 0)
    def _():
        h = _modulated_norm(x_ref[0], n1_ref[...], mod_ref[0, 0:1, :], mod_ref[0, 1:2, :])
        h_ref[...] = h.astype(BF16)

    acc = _dot(h_ref[...], w_ref[...])

    @pl.when(j < nqk)
    def _():
        sq = acc * acc
        hi = sq.astype(BF16)
        lo = (sq - hi.astype(F32)).astype(BF16)
        ssum = _dot(hi, seg_ref[...]) + _dot(lo, seg_ref[...])
        y = acc * lax.rsqrt(ssum * (1.0 / HEAD_DIM) + EPS) * gain_ref[0]
        if use_rope:
            lane = lax.broadcasted_iota(I32, y.shape, 1)
            first = (lane % (2 * ROPE_FREQS)) < ROPE_FREQS
            rot = jnp.where(first, pltpu.roll(y, MXU_N - ROPE_FREQS, 1), pltpu.roll(y, ROPE_FREQS, 1))
            y = y * cos_ref[...] + rot * sin_ref[...]
        o_ref[0] = jnp.where(flag_ref[0] > 0.0, y, acc).astype(BF16)

    @pl.when(j >= nqk)
    def _():
        vt_ref[0] = acc.T.astype(BF16)


def _proj(x, mod, n1, w, gains, flags, seg, cos_t, sin_t, *, nqk, use_rope):
    b, s, d = x.shape
    ncols = w.shape[1]
    tm = _tile(s, 1024)
    nj = ncols // MXU_N
    return pl.pallas_call(
        functools.partial(_proj_kernel, nqk=nqk, use_rope=use_rope),
        out_shape=(jax.ShapeDtypeStruct((b, s, nqk * MXU_N), BF16),
                   jax.ShapeDtypeStruct((b, (nj - nqk) * MXU_N, s), BF16)),
        grid=(b, s // tm, nj),
        in_specs=[
            pl.BlockSpec((1, tm, d), lambda bi, si, j: (bi, si, 0)),
            pl.BlockSpec((1, 6, d), lambda bi, si, j: (bi, 0, 0)),
            pl.BlockSpec((1, d), lambda bi, si, j: (0, 0)),
            pl.BlockSpec((d, MXU_N), lambda bi, si, j: (0, j)),
            pl.BlockSpec((1, 1, MXU_N), lambda bi, si, j: (jnp.minimum(j, nqk - 1), 0, 0)),
            pl.BlockSpec((1, 1, MXU_N), lambda bi, si, j: (jnp.minimum(j, nqk - 1), 0, 0)),
            pl.BlockSpec((MXU_N, MXU_N), lambda bi, si, j: (0, 0)),
            pl.BlockSpec((tm, MXU_N), lambda bi, si, j: (si, 0)),
            pl.BlockSpec((tm, MXU_N), lambda bi, si, j: (si, 0)),
        ],
        out_specs=(pl.BlockSpec((1, tm, MXU_N), lambda bi, si, j: (bi, si, jnp.minimum(j, nqk - 1))),
                   pl.BlockSpec((1, MXU_N, tm), lambda bi, si, j: (bi, jnp.maximum(j - nqk, 0), si))),
        scratch_shapes=[pltpu.VMEM((tm, d), BF16)],
        compiler_params=_cparams(("parallel", "parallel", "arbitrary")),
        name="proj_rope" if use_rope else "proj_ctx",
    )(x, mod, n1, w, gains, flags, seg, cos_t, sin_t)


def _attn_a_kernel(sink_ref, q_ref, kp_ref, kc_ref, kn_ref, vp_ref, vc_ref, vn_ref, kx_ref, vx_ref, o_ref):
    i = pl.program_id(1)
    nb = pl.num_programs(1)
    nctx = kx_ref.shape[1]
    span = 3 * BLOCK + nctx
    kall = jnp.concatenate([kp_ref[0], kc_ref[0], kn_ref[0], kx_ref[0]], axis=0)
    vall = jnp.concatenate([vp_ref[0], vc_ref[0], vn_ref[0], vx_ref[0]], axis=0)

    r = lax.broadcasted_iota(I32, (BLOCK, span), 0)
    c = lax.broadcasted_iota(I32, (BLOCK, span), 1)
    prev_ok = (c < BLOCK) & (c >= r) & (i > 0)
    cur_ok = (c >= BLOCK) & (c < 2 * BLOCK)
    next_ok = (c >= 2 * BLOCK) & (c < 3 * BLOCK) & (c - 2 * BLOCK <= r) & (i < nb - 1)
    valid = prev_ok | cur_ok | next_ok | (c >= 3 * BLOCK)
    bias = jnp.where(valid, 0.0, NEG_BIG)

    lane = lax.broadcasted_iota(I32, (BLOCK, LANES), 1)
    for kv in range(A_KV_HEADS):
        in_half = (lane >= kv * HEAD_DIM) & (lane < (kv + 1) * HEAD_DIM)
        qs = []
        for g in range(A_GROUP):
            hh = kv * A_GROUP + g
            blk = q_ref[0, :, (hh // 2) * LANES:(hh // 2 + 1) * LANES].astype(F32)
            if hh % 2 != kv:
                blk = pltpu.roll(blk, HEAD_DIM, 1)
            qs.append(jnp.where(in_half, blk, 0.0).astype(BF16))
        s_all = _nt_dot(jnp.concatenate(qs, axis=0), kall)
        ps, ls = [], []
        for g in range(A_GROUP):
            sink = sink_ref[kv, g]
            s = s_all[g * BLOCK:(g + 1) * BLOCK] + bias
            m = jnp.maximum(jnp.max(s, axis=-1, keepdims=True), sink)
            p = jnp.exp(s - m)
            ls.append(jnp.sum(p, axis=-1, keepdims=True) + jnp.exp(sink - m))
            ps.append(p.astype(BF16))
        o_all = _dot(jnp.concatenate(ps, axis=0), vall)
        os_ = [o_all[g * BLOCK:(g + 1) * BLOCK] / ls[g] for g in range(A_GROUP)]
        for pair in range(A_GROUP // 2):
            o0, o1 = os_[2 * pair], os_[2 * pair + 1]
            if kv == 0:
                o1 = pltpu.roll(o1, HEAD_DIM, 1)
            else:
                o0 = pltpu.roll(o0, HEAD_DIM, 1)
            col = (kv * (A_GROUP // 2) + pair) * LANES
            o_ref[0, :, col:col + LANES] = jnp.where(lane < HEAD_DIM, o0, o1).astype(BF16)


def _attn_a(sink, lat, ctxp):
    b, s, _ = lat.shape
    nctx = ctxp.shape[1]
    nb = s // BLOCK
    width = A_Q_HEADS * HEAD_DIM

    def kspec(col, shift):
        return pl.BlockSpec((1, BLOCK, LANES),
                            lambda bi, i: (bi, jnp.clip(i + shift, 0, nb - 1), col))

    return pl.pallas_call(
        _attn_a_kernel,
        out_shape=jax.ShapeDtypeStruct((b, s, width), BF16),
        grid=(b, nb),
        in_specs=[
            pl.BlockSpec(memory_space=pltpu.SMEM),
            pl.BlockSpec((1, BLOCK, width), lambda bi, i: (bi, i, LAT_QA * LANES // width)),
            kspec(LAT_KA, -1), kspec(LAT_KA, 0), kspec(LAT_KA, 1),
            kspec(LAT_VA, -1), kspec(LAT_VA, 0), kspec(LAT_VA, 1),
            pl.BlockSpec((1, nctx, LANES), lambda bi, i: (bi, 0, CTX_KA)),
            pl.BlockSpec((1, nctx, LANES), lambda bi, i: (bi, 0, CTX_VA)),
        ],
        out_specs=pl.BlockSpec((1, BLOCK, width), lambda bi, i: (bi, i, 0)),
        compiler_params=_cparams(("parallel", "parallel")),
        name="attn_a",
    )(sink, lat, lat, lat, lat, lat, lat, lat, ctxp, ctxp)


def _attn_b_kernel(lamv_ref, sub_ref, q_ref, k_ref, vt_ref, kx_ref, vtx_ref, o_ref, acc_ref, s0_ref, s1_ref,
                   sx_ref, *, tk):
    tq = q_ref.shape[1]
    s_len = k_ref.shape[1]
    qt = q_ref[0].astype(F32).T
    row = lax.broadcasted_iota(I32, (LANES, tq), 0)
    qts = [jnp.where(row < HEAD_DIM, qt, 0.0).astype(BF16), jnp.where(row >= HEAD_DIM, qt, 0.0).astype(BF16)]
    acc_ref[...] = jnp.zeros(acc_ref.shape, F32)
    n = s_len // tk

    def scores(kc, s_ref):
        cms = []
        for j in range(2):
            s = _dot(kc, qts[j])
            s_ref[j] = s
            cms.append(jnp.max(s, axis=0, keepdims=True))
        return tuple(cms)

    def accumulate(s_ref, cms, vtc, stats):
        new_stats = []
        for j in range(2):
            m_old, l_old = stats[j]
            m_new = jnp.maximum(m_old, cms[j])
            alpha = jnp.exp(m_old - m_new)
            p = jnp.exp(s_ref[j] - m_new)
            l_new = alpha * l_old + jnp.sum(p, axis=0, keepdims=True)
            acc_ref[j] = alpha * acc_ref[j] + _dot(vtc, p.astype(BF16))
            new_stats.append((m_new, l_new))
        return tuple(new_stats)

    def kchunk(i):
        return k_ref[0, pl.ds(pl.multiple_of(i * tk, tk), tk), :]

    def vchunk(i):
        return vt_ref[0, :, pl.ds(pl.multiple_of(i * tk, tk), tk)]

    init = tuple((jnp.full((1, tq), NEG_BIG, F32), jnp.zeros((1, tq), F32)) for _ in range(2))
    cmx = scores(kx_ref[0], sx_ref)
    cm0 = scores(kchunk(0), s0_ref)
    stats = accumulate(sx_ref, cmx, vtx_ref[0], init)

    def body(pair, carry):
        stats, cm0 = carry
        i = 2 * pair
        cm1 = scores(kchunk(i + 1), s1_ref)
        stats = accumulate(s0_ref, cm0, vchunk(i), stats)
        cm0 = scores(kchunk(jnp.minimum(i + 2, n - 1)), s0_ref)
        stats = accumulate(s1_ref, cm1, vchunk(i + 1), stats)
        return stats, cm0

    ((_, l0), (_, l1)), _ = lax.fori_loop(0, n // 2, body, (stats, cm0))

    lv = lamv_ref[...]
    lam = (jnp.exp(jnp.sum(lv[0:1] * lv[1:2], axis=-1, keepdims=True))
           - jnp.exp(jnp.sum(lv[2:3] * lv[3:4], axis=-1, keepdims=True)) + LAM_INIT)
    o = acc_ref[0] * (1.0 / l0) - acc_ref[1] * (lam / l1)
    ms = jnp.mean(o * o, axis=0, keepdims=True)
    o = o * lax.rsqrt(ms + EPS) * (sub_ref[...] * (1.0 - LAM_INIT))
    o_ref[0] = o.T.astype(BF16)


def _attn_b(lamv, subln_col, lat, vt, ctxp, vtx):
    b, s, _ = lat.shape
    nctx = ctxp.shape[1]
    tq = _tile(s, 256)
    tk = min(512, s // 2)
    assert s % (2 * tk) == 0
    return pl.pallas_call(
        functools.partial(_attn_b_kernel, tk=tk),
        out_shape=jax.ShapeDtypeStruct((b, s, B_HEADS * LANES), BF16),
        grid=(b, B_HEADS, s // tq),
        in_specs=[
            pl.BlockSpec((4, HEAD_DIM), lambda bi, h, qi: (0, 0)),
            pl.BlockSpec((LANES, 1), lambda bi, h, qi: (0, 0)),
            pl.BlockSpec((1, tq, LANES), lambda bi, h, qi: (bi, qi, LAT_QB + h)),
            pl.BlockSpec((1, s, LANES), lambda bi, h, qi: (bi, 0, LAT_KB + h)),
            pl.BlockSpec((1, LANES, s), lambda bi, h, qi: (bi, h, 0)),
            pl.BlockSpec((1, nctx, LANES), lambda bi, h, qi: (bi, 0, CTX_KB + h)),
            pl.BlockSpec((1, LANES, nctx), lambda bi, h, qi: (bi, h, 0)),
        ],
        out_specs=pl.BlockSpec((1, tq, LANES), lambda bi, h, qi: (bi, qi, h)),
        scratch_shapes=[pltpu.VMEM((2, LANES, tq), F32), pltpu.VMEM((2, tk, tq), F32),
                        pltpu.VMEM((2, tk, tq), F32), pltpu.VMEM((2, nctx, tq), F32)],
        compiler_params=_cparams(("parallel", "parallel", "arbitrary")),
        name="attn_b",
    )(lamv, subln_col, lat, lat, vt, ctxp, vtx)


def _pack_rows(h):
    n = h.shape[1] // 2
    bits = pltpu.bitcast(h.astype(BF16).astype(F32), U32)
    return (bits[:, :n] >> 16) | (bits[:, n:] & jnp.uint32(0xFFFF0000))


def _unpack_rows(w):
    lo = pltpu.bitcast(w << 16, F32)
    hi = pltpu.bitcast(w & jnp.uint32(0xFFFF0000), F32)
    return jnp.concatenate([lo, hi], axis=1).astype(BF16)


def _merge_kernel(x_ref, mod_ref, n1_ref, n2_ref, oa_ref, ob_ref, wg_ref, wpa_ref, wpb_ref, wo_ref, wr_ref,
                  x1_ref, h2_ref, lg_ref):
    d = x_ref.shape[2]
    xf = x_ref[0]
    h = _modulated_norm(xf, n1_ref[...], mod_ref[0, 0:1, :], mod_ref[0, 1:2, :]).astype(BF16)
    gates = _sigmoid(_dot(h, wg_ref[...]))
    ya = _dot(oa_ref[0], wpa_ref[...])
    yb = _dot(ob_ref[0], wpb_ref[...])
    z = gates[:, :d] * ya + gates[:, d:] * yb
    x1 = xf + mod_ref[0, 2:3, :] * _dot(z.astype(BF16), wo_ref[...])
    x1_ref[0] = x1
    h2 = _modulated_norm(x1, n2_ref[...], mod_ref[0, 3:4, :], mod_ref[0, 4:5, :])
    lg_ref[0] = _dot(h2.astype(BF16), wr_ref[...])
    h2_ref[0] = _pack_rows(h2)


def _merge(x, mod, n1, n2, oa, ob, wg, wpa, wpb, wo, wr):
    b, s, d = x.shape
    tm = _tile(s, 256)
    ne = wr.shape[1]
    const = lambda bi, si: (0, 0)
    row = lambda bi, si: (bi, si, 0)
    return pl.pallas_call(
        _merge_kernel,
        out_shape=(jax.ShapeDtypeStruct((b, s, d), F32),
                   jax.ShapeDtypeStruct((b, s, d // 2), U32),
                   jax.ShapeDtypeStruct((b, s, ne), F32)),
        grid=(b, s // tm),
        in_specs=[
            pl.BlockSpec((1, tm, d), row),
            pl.BlockSpec((1, 6, d), lambda bi, si: (bi, 0, 0)),
            pl.BlockSpec((1, d), const), pl.BlockSpec((1, d), const),
            pl.BlockSpec((1, tm, d), row), pl.BlockSpec((1, tm, d), row),
            pl.BlockSpec((d, 2 * d), const), pl.BlockSpec((d, d), const),
            pl.BlockSpec((d, d), const), pl.BlockSpec((d, d), const),
            pl.BlockSpec((d, ne), const),
        ],
        out_specs=(pl.BlockSpec((1, tm, d), row), pl.BlockSpec((1, tm, d // 2), row),
                   pl.BlockSpec((1, tm, ne), row)),
        compiler_params=_cparams(("parallel", "parallel")),
        name="merge",
    )(x, mod, n1, n2, oa, ob, wg, wpa, wpb, wo, wr)


def _router_kernel(lg_ref, bias_ref, idx_ref, w_ref, rank_ref, cnt_ref, run_ref):
    i = pl.program_id(0)
    tr = lg_ref.shape[0]

    @pl.when(i == 0)
    def _():
        run_ref[...] = jnp.zeros(run_ref.shape, F32)

    scores = _sigmoid(lg_ref[...])
    biased = scores + bias_ref[...]
    lane = lax.broadcasted_iota(I32, scores.shape, 1)
    neg_inf = -jnp.inf

    def first_argmax(vals):
        m = jnp.max(vals, axis=-1, keepdims=True)
        idx = jnp.min(jnp.where(vals == m, lane, N_EXPERTS), axis=-1, keepdims=True)
        return m, idx

    gscore = []
    for g in range(N_GROUPS):
        in_g = (lane >= g * GROUP_SIZE) & (lane < (g + 1) * GROUP_SIZE)
        vals = jnp.where(in_g, biased, neg_inf)
        m1, i1 = first_argmax(vals)
        m2 = jnp.max(jnp.where(lane == i1, neg_inf, vals), axis=-1, keepdims=True)
        gscore.append(m1 + m2)
    emask = jnp.zeros(scores.shape, jnp.bool_)
    for g in range(N_GROUPS):
        beaten = jnp.zeros((tr, 1), F32)
        for g2 in range(N_GROUPS):
            if g2 == g:
                continue
            wins = (gscore[g2] > gscore[g]) | ((gscore[g2] == gscore[g]) & (g2 < g))
            beaten = beaten + jnp.where(wins, 1.0, 0.0)
        in_g = (lane >= g * GROUP_SIZE) & (lane < (g + 1) * GROUP_SIZE)
        emask = emask | (in_g & (beaten < TOPK_GROUPS))

    masked = jnp.where(emask, biased, neg_inf)
    sel = jnp.zeros(scores.shape, F32)
    idxs, ws = [], []
    for _ in range(TOP_K):
        _, ik = first_argmax(masked)
        hit = lane == ik
        idxs.append(ik)
        ws.append(jnp.sum(jnp.where(hit, scores, 0.0), axis=-1, keepdims=True))
        sel = jnp.where(hit, 1.0, sel)
        masked = jnp.where(hit, neg_inf, masked)
    wsum = ws[0]
    for k in range(1, TOP_K):
        wsum = wsum + ws[k]

    rr = lax.broadcasted_iota(I32, (tr, tr), 0)
    cc = lax.broadcasted_iota(I32, (tr, tr), 1)
    lower = jnp.where(cc < rr, 1.0, 0.0).astype(BF16)
    before = _dot(lower, sel.astype(BF16)) + run_ref[...]
    ranks = [jnp.sum(jnp.where(lane == idxs[k], before, 0.0), axis=-1, keepdims=True) for k in range(TOP_K)]
    run_ref[...] = run_ref[...] + jnp.sum(sel, axis=0, keepdims=True)
    cnt_ref[...] = run_ref[...]

    idx_ref[...] = jnp.concatenate(idxs, axis=1)
    w_ref[...] = jnp.concatenate([w / wsum * ROUTED_SCALE for w in ws], axis=1)
    rank_ref[...] = jnp.concatenate(ranks, axis=1).astype(I32)


def _router(logits, bias):
    t, ne = logits.shape
    tr = _tile(t, 512)
    return pl.pallas_call(
        _router_kernel,
        out_shape=(jax.ShapeDtypeStruct((t, TOP_K), I32), jax.ShapeDtypeStruct((t, TOP_K), F32),
                   jax.ShapeDtypeStruct((t, TOP_K), I32), jax.ShapeDtypeStruct((1, ne), F32)),
        grid=(t // tr,),
        in_specs=[pl.BlockSpec((tr, ne), lambda i: (i, 0)), pl.BlockSpec((1, ne), lambda i: (0, 0))],
        out_specs=(pl.BlockSpec((tr, TOP_K), lambda i: (i, 0)), pl.BlockSpec((tr, TOP_K), lambda i: (i, 0)),
                   pl.BlockSpec((tr, TOP_K), lambda i: (i, 0)), pl.BlockSpec((1, ne), lambda i: (0, 0))),
        scratch_shapes=[pltpu.VMEM((1, ne), F32)],
        compiler_params=_cparams(("arbitrary",)),
        name="router",
    )(logits, bias)


def _dest_kernel(idx_ref, rank_ref, pstart_ref, dest_ref):
    idx = idx_ref[...]
    lane = lax.broadcasted_iota(I32, (idx.shape[0], N_EXPERTS), 1)
    cols = []
    for k in range(TOP_K):
        start = jnp.sum(jnp.where(lane == idx[:, k:k + 1], pstart_ref[...], 0.0), axis=-1, keepdims=True)
        cols.append(start.astype(I32) + rank_ref[:, k:k + 1])
    dest_ref[...] = jnp.concatenate(cols, axis=1)


def _dest(idx, rank, pstart):
    t = idx.shape[0]
    tr = _tile(t, 512)
    return pl.pallas_call(
        _dest_kernel,
        out_shape=jax.ShapeDtypeStruct((t, TOP_K), I32),
        grid=(t // tr,),
        in_specs=[pl.BlockSpec((tr, TOP_K), lambda i: (i, 0)), pl.BlockSpec((tr, TOP_K), lambda i: (i, 0)),
                  pl.BlockSpec((1, N_EXPERTS), lambda i: (0, 0))],
        out_specs=pl.BlockSpec((tr, TOP_K), lambda i: (i, 0)),
        compiler_params=_cparams(("parallel",)),
        name="dest",
    )(idx, rank, pstart)


def _dispatch_kernel(dest_hbm, h2_ref, xs_in, xs_out, dest_smem, sem_idx, sem_rows):
    del xs_in
    i = pl.program_id(0)
    td = h2_ref.shape[0]
    n = td * TOP_K
    cp = pltpu.make_async_copy(dest_hbm.at[pl.ds(i * n, n)], dest_smem, sem_idx)
    cp.start()
    cp.wait()

    def row_copy(t, d):
        return pltpu.make_async_copy(h2_ref.at[pl.ds(t, 1), :], xs_out.at[pl.ds(d, 1), :], sem_rows)

    def issue(t, carry):
        for k in range(TOP_K):
            row_copy(t, dest_smem[t * TOP_K + k]).start()
        return carry

    lax.fori_loop(0, td, issue, 0)

    def drain(t, carry):
        for k in range(TOP_K):
            row_copy(0, 0).wait()
        return carry

    lax.fori_loop(0, td, drain, 0)


def _dispatch(dest_flat, h2p, xs_init):
    t, half = h2p.shape
    td = _tile(t, 256)
    return pl.pallas_call(
        _dispatch_kernel,
        out_shape=jax.ShapeDtypeStruct(xs_init.shape, U32),
        grid=(t // td,),
        in_specs=[pl.BlockSpec(memory_space=pl.ANY),
                  pl.BlockSpec((td, half), lambda i: (i, 0)),
                  pl.BlockSpec(memory_space=pl.ANY)],
        out_specs=pl.BlockSpec(memory_space=pl.ANY),
        scratch_shapes=[pltpu.SMEM((td * TOP_K,), I32), pltpu.SemaphoreType.DMA, pltpu.SemaphoreType.DMA],
        input_output_aliases={2: 0},
        compiler_params=pltpu.CompilerParams(dimension_semantics=("arbitrary",), vmem_limit_bytes=VMEM_LIMIT,
                                             has_side_effects=True),
        name="dispatch",
    )(dest_flat, h2p, xs_init)


def _experts_kernel(be_ref, nused_ref, xs_ref, wg_ref, wu_ref, wd_ref, ys_ref):
    del be_ref
    blk = pl.program_id(0)

    @pl.when(blk < nused_ref[0])
    def _():
        x = _unpack_rows(xs_ref[...])
        g = _dot(x, wg_ref[0].astype(BF16))
        u = _dot(x, wu_ref[0].astype(BF16))
        a = (_silu(g) * u).astype(BF16)
        ys_ref[...] = _dot(a, wd_ref[0].astype(BF16))

    @pl.when(blk >= nused_ref[0])
    def _():
        ys_ref[...] = jnp.zeros(ys_ref.shape, F32)


def _experts(blk_expert, nused, xs, wg, wu, wd):
    p, half = xs.shape
    d = 2 * half
    de = wg.shape[2]
    nblk = p // MOE_BLOCK
    grid_spec = pltpu.PrefetchScalarGridSpec(
        num_scalar_prefetch=2,
        grid=(nblk,),
        in_specs=[
            pl.BlockSpec((MOE_BLOCK, half), lambda i, be, nu: (i, 0)),
            pl.BlockSpec((1, d, de), lambda i, be, nu: (be[i], 0, 0)),
            pl.BlockSpec((1, d, de), lambda i, be, nu: (be[i], 0, 0)),
            pl.BlockSpec((1, de, d), lambda i, be, nu: (be[i], 0, 0)),
        ],
        out_specs=pl.BlockSpec((MOE_BLOCK, d), lambda i, be, nu: (i, 0)),
    )
    return pl.pallas_call(
        _experts_kernel,
        out_shape=jax.ShapeDtypeStruct((p, d), F32),
        grid_spec=grid_spec,
        compiler_params=_cparams(("arbitrary",)),
        name="experts",
    )(blk_expert, nused, xs, wg, wu, wd)


def _combine_kernel(dest_hbm, w_ref, ys_hbm, h2_ref, x1_ref, mod_ref, wgs_ref, wus_ref, wds_ref, o_ref,
                    dest_smem, buf, sem_idx, sem_rows):
    i = pl.program_id(0)
    tc = h2_ref.shape[0]
    n = tc * TOP_K
    cp = pltpu.make_async_copy(dest_hbm.at[pl.ds(i * n, n)], dest_smem, sem_idx)
    cp.start()
    cp.wait()

    def row_copy(t, k, d):
        return pltpu.make_async_copy(ys_hbm.at[pl.ds(d, 1), :], buf.at[k, pl.ds(t, 1), :], sem_rows)

    def issue(t, carry):
        for k in range(TOP_K):
            row_copy(t, k, dest_smem[t * TOP_K + k]).start()
        return carry

    lax.fori_loop(0, tc, issue, 0)

    x = _unpack_rows(h2_ref[...])
    a = (_silu(_dot(x, wgs_ref[...])) * _dot(x, wus_ref[...])).astype(BF16)
    y = _dot(a, wds_ref[...])

    def drain(t, carry):
        for k in range(TOP_K):
            row_copy(0, k, 0).wait()
        return carry

    lax.fori_loop(0, tc, drain, 0)

    for k in range(TOP_K):
        y = y + buf[k] * w_ref[:, k:k + 1]
    o_ref[...] = x1_ref[...] + mod_ref[0, 5:6, :] * y


def _combine(dest_flat, w, ys, h2p, x1, mod, wgs, wus, wds, seq):
    t, d = x1.shape
    half = d // 2
    tc = _tile(seq, 128)
    de = wgs.shape[1]
    per_batch = seq // tc
    const = lambda i: (0, 0)
    return pl.pallas_call(
        _combine_kernel,
        out_shape=jax.ShapeDtypeStruct((t, d), F32),
        grid=(t // tc,),
        in_specs=[
            pl.BlockSpec(memory_space=pl.ANY),
            pl.BlockSpec((tc, TOP_K), lambda i: (i, 0)),
            pl.BlockSpec(memory_space=pl.ANY),
            pl.BlockSpec((tc, half), lambda i: (i, 0)),
            pl.BlockSpec((tc, d), lambda i: (i, 0)),
            pl.BlockSpec((1, 6, d), lambda i: (i // per_batch, 0, 0)),
            pl.BlockSpec((d, de), const), pl.BlockSpec((d, de), const), pl.BlockSpec((de, d), const),
        ],
        out_specs=pl.BlockSpec((tc, d), lambda i: (i, 0)),
        scratch_shapes=[pltpu.SMEM((tc * TOP_K,), I32), pltpu.VMEM((TOP_K, tc, d), F32),
                        pltpu.SemaphoreType.DMA, pltpu.SemaphoreType.DMA],
        compiler_params=_cparams(("arbitrary",)),
        name="combine",
    )(dest_flat, w, ys, h2p, x1, mod, wgs, wus, wds)


def _rope_tables(n):
    rows = n // GRID_W
    row = jnp.broadcast_to(jnp.arange(rows)[:, None], (rows, GRID_W)).reshape(-1)
    col = jnp.broadcast_to(jnp.arange(GRID_W)[None, :], (rows, GRID_W)).reshape(-1)
    freqs = ROPE_BASE ** (-jnp.arange(ROPE_FREQS, dtype=F32) / ROPE_FREQS)
    pos = jnp.stack([row, col], axis=-1).astype(F32)
    ang = pos[:, :, None] * freqs
    cos, sin = jnp.cos(ang), jnp.sin(ang)
    cos64 = jnp.concatenate([cos[:, 0], cos[:, 0], cos[:, 1], cos[:, 1]], axis=-1)
    sin64 = jnp.concatenate([-sin[:, 0], sin[:, 0], -sin[:, 1], sin[:, 1]], axis=-1)
    reps = MXU_N // HEAD_DIM
    return jnp.tile(cos64, (1, reps)), jnp.tile(sin64, (1, reps))


def _gain_rows(qn_a, kn_a, qn_b, kn_b):
    reps = MXU_N // HEAD_DIM
    ones = jnp.ones((LANES,), F32)
    kb = jnp.tile(kn_b, reps)
    qa = jnp.tile(qn_a, reps) * ATTN_SCALE
    qb = jnp.tile(qn_b, reps) * ATTN_SCALE
    mixed = jnp.concatenate([jnp.tile(kn_a, LANES // HEAD_DIM), ones])
    mixed_flag = jnp.concatenate([ones, 0.0 * ones])
    full_flag = jnp.ones((MXU_N,), F32)
    lat_g = jnp.stack([kb] * 4 + [qa] * 4 + [qb] * 4 + [mixed])[:, None, :]
    lat_f = jnp.stack([full_flag] * 12 + [mixed_flag])[:, None, :]
    ctx_g = jnp.stack([kb] * 4 + [mixed])[:, None, :]
    ctx_f = jnp.stack([full_flag] * 4 + [mixed_flag])[:, None, :]
    return lat_g, lat_f, ctx_g, ctx_f


def kernel(x, c, ctx, c_ctx, w_ada, b_ada, norm1_g, norm2_g, w_in, qnorm_a, knorm_a, sink_a, qnorm_b, knorm_b,
           lam_q1, lam_k1, lam_q2, lam_k2, subln_g, w_pa, w_pb, w_o, w_router, router_bias, w_gate_e, w_up_e,
           w_down_e, w_gate_s, w_up_s, w_down_s):
    b, s, d = x.shape
    nctx = ctx.shape[1]
    assert w_ada.shape[0] == 1 and d == 1024 and s % BLOCK == 0 and s % GRID_W == 0
    t = b * s

    cc = jnp.concatenate([c, c_ctx[None, :], jnp.zeros((16 - b - 1, d), F32)], axis=0)
    mod_all = _ada(cc, w_ada[0], b_ada[0])
    mod = mod_all[:b].reshape(b, 6, d)
    mod_c = jnp.broadcast_to(mod_all[b].reshape(1, 6, d), (b, 6, d))

    w = w_in[0]
    ka_w, va_w = w[:, 0:128], w[:, 128:256]
    kb_w, vb_w = w[:, 256:1280], w[:, 1280:2304]
    qa_w, qb_w = w[:, 2304:3328], w[:, 3328:4352]
    gate_w = w[:, 4352:6400].astype(BF16)
    w_lat = jnp.concatenate([kb_w, qa_w, qb_w, ka_w, va_w, vb_w], axis=1).astype(BF16)
    w_ctx = jnp.concatenate([kb_w, ka_w, va_w, vb_w], axis=1).astype(BF16)

    lat_g, lat_f, ctx_g, ctx_f = _gain_rows(qnorm_a[0], knorm_a[0], qnorm_b[0], knorm_b[0])
    hid = jnp.arange(MXU_N) // HEAD_DIM
    seg = (hid[:, None] == hid[None, :]).astype(BF16)
    cos_t, sin_t = _rope_tables(s)
    n1 = norm1_g[0].reshape(1, d)
    n2 = norm2_g[0].reshape(1, d)

    lat, vt = _proj(x, mod, n1, w_lat, lat_g, lat_f, seg, cos_t, sin_t, nqk=LAT_NQK, use_rope=True)
    ctxp, vtx = _proj(ctx, mod_c, n1, w_ctx, ctx_g, ctx_f, seg, cos_t[:nctx], sin_t[:nctx], nqk=CTX_NQK,
                      use_rope=False)

    oa = _attn_a(sink_a[0], lat, ctxp)
    lamv = jnp.stack([lam_q1[0], lam_k1[0], lam_q2[0], lam_k2[0]])
    ob = _attn_b(lamv, subln_g[0].reshape(LANES, 1), lat, vt, ctxp, vtx)

    x1, h2p, logits = _merge(x, mod, n1, n2, oa, ob, gate_w, w_pa[0].astype(BF16), w_pb[0].astype(BF16),
                             w_o[0].astype(BF16), w_router[0].astype(BF16))
    x1 = x1.reshape(t, d)
    h2p = h2p.reshape(t, d // 2)
    logits = logits.reshape(t, N_EXPERTS)

    idx, wts, rank, counts = _router(logits, router_bias[0].reshape(1, N_EXPERTS))

    cnt = counts[0].astype(I32)
    padded = (cnt + MOE_BLOCK - 1) // MOE_BLOCK * MOE_BLOCK
    pend = jnp.cumsum(padded)
    pstart = pend - padded
    nblk = -(-(t * TOP_K) // MOE_BLOCK) + N_EXPERTS
    blk_expert = jnp.minimum(jnp.searchsorted(pend, jnp.arange(nblk, dtype=I32) * MOE_BLOCK, side='right'),
                             N_EXPERTS - 1).astype(I32)
    nused = (pend[-1:] // MOE_BLOCK).astype(I32)

    dest = _dest(idx, rank, pstart.astype(F32).reshape(1, N_EXPERTS)).reshape(t * TOP_K)
    xs = _dispatch(dest, h2p, jnp.zeros((nblk * MOE_BLOCK, d // 2), U32))
    ys = _experts(blk_expert, nused, xs, w_gate_e[0], w_up_e[0], w_down_e[0])
    out = _combine(dest, wts, ys, h2p, x1, mod, w_gate_s[0].astype(BF16), w_up_s[0].astype(BF16),
                   w_down_s[0].astype(BF16), s)
    return out.reshape(b, s, d)
```

```python
import functools
import math

import jax
import jax.numpy as jnp
from jax import lax
from jax.experimental import pallas as pl
from jax.experimental.pallas import tpu as pltpu

F32 = jnp.float32
BF16 = jnp.bfloat16
I32 = jnp.int32
U32 = jnp.uint32

HEAD_DIM = 64
GRID_W = 64
ROPE_FREQS = HEAD_DIM // 4
ROPE_BASE = 10000.0
EPS = 1e-6
ATTN_SCALE = HEAD_DIM ** -0.5
BLOCK = 128
A_Q_HEADS = 16
A_KV_HEADS = 2
A_GROUP = A_Q_HEADS // A_KV_HEADS
B_HEADS = 8
N_EXPERTS = 256
TOP_K = 8
N_GROUPS = 8
TOPK_GROUPS = 4
GROUP_SIZE = N_EXPERTS // N_GROUPS
ROUTED_SCALE = 2.5
MOE_BLOCK = 256
LAM_INIT = 0.8 - 0.6 * math.exp(-0.3 * 0)

LANES = 128
MXU_N = 256
VMEM_LIMIT = 56 * 1024 * 1024
NEG_BIG = -1e30
LOG2E = math.log2(math.e)
MAX_FIXED_SHIFT = 60.0

LAT_KB, LAT_QA, LAT_QB, LAT_KA, LAT_VA = 0, 8, 16, 24, 25
LAT_NQK = 13
CTX_KB, CTX_KA, CTX_VA = 0, 8, 9
CTX_NQK = 5


def _tile(n, pref):
    return pref if n % pref == 0 else n


def _cparams(sem):
    return pltpu.CompilerParams(dimension_semantics=sem, vmem_limit_bytes=VMEM_LIMIT)


def _nt_dot(a, b):
    return lax.dot_general(a, b, (((1,), (1,)), ((), ())), preferred_element_type=F32)


def _dot(a, b):
    return jnp.dot(a, b, preferred_element_type=F32)


def _silu(x):
    return x * (1.0 / (1.0 + jnp.exp(-x)))


def _sigmoid(x):
    return 1.0 / (1.0 + jnp.exp(-x))


def _modulated_norm(xf, g_row, shift_row, scale_row):
    ms = jnp.mean(xf * xf, axis=-1, keepdims=True)
    y = xf * lax.rsqrt(ms + EPS) * g_row
    return y * (1.0 + scale_row) + shift_row


def _ada_kernel(c_ref, w_ref, b_ref, o_ref):
    o_ref[...] = _dot(_silu(c_ref[...]).astype(BF16), w_ref[...].astype(BF16)) + b_ref[...]


def _ada(cc, w_ada, b_ada):
    rows, d = cc.shape
    n = w_ada.shape[1]
    tn = _tile(n, 512)
    return pl.pallas_call(
        _ada_kernel,
        out_shape=jax.ShapeDtypeStruct((rows, n), F32),
        grid=(n // tn,),
        in_specs=[pl.BlockSpec((rows, d), lambda j: (0, 0)),
                  pl.BlockSpec((d, tn), lambda j: (0, j)),
                  pl.BlockSpec((1, tn), lambda j: (0, j))],
        out_specs=pl.BlockSpec((rows, tn), lambda j: (0, j)),
        compiler_params=_cparams(("parallel",)),
        name="ada",
    )(cc, w_ada, b_ada.reshape(1, n))


def _proj_kernel(x_ref, mod_ref, n1_ref, w_ref, gain_ref, flag_ref, seg_ref, cos_ref, sin_ref,
                 o_ref, vt_ref, h_ref, *, nqk, use_rope):
    j = pl.program_id(2)

    @pl.when(j == 0)
    def _():
        h = _modulated_norm(x_ref[0], n1_ref[...], mod_ref[0, 0:1, :], mod_ref[0, 1:2, :])
        h_ref[...] = h.astype(BF16)

    acc = _dot(h_ref[...], w_ref[...])

    @pl.when(j < nqk)
    def _():
        sq = acc * acc
        hi = sq.astype(BF16)
        lo = (sq - hi.astype(F32)).astype(BF16)
        ssum = _dot(hi, seg_ref[...]) + _dot(lo, seg_ref[...])
        y = acc * lax.rsqrt(ssum * (1.0 / HEAD_DIM) + EPS) * gain_ref[0]
        if use_rope:
            lane = lax.broadcasted_iota(I32, y.shape, 1)
            first = (lane % (2 * ROPE_FREQS)) < ROPE_FREQS
            rot = jnp.where(first, pltpu.roll(y, MXU_N - ROPE_FREQS, 1), pltpu.roll(y, ROPE_FREQS, 1))
            y = y * cos_ref[...] + rot * sin_ref[...]
        o_ref[0] = jnp.where(flag_ref[0] > 0.0, y, acc).astype(BF16)

    @pl.when(j >= nqk)
    def _():
        vt_ref[0] = acc.T.astype(BF16)


def _proj(x, mod, n1, w, gains, flags, seg, cos_t, sin_t, *, nqk, use_rope):
    b, s, d = x.shape
    ncols = w.shape[1]
    tm = _tile(s, 1024)
    nj = ncols // MXU_N
    return pl.pallas_call(
        functools.partial(_proj_kernel, nqk=nqk, use_rope=use_rope),
        out_shape=(jax.ShapeDtypeStruct((b, s, nqk * MXU_N), BF16),
                   jax.ShapeDtypeStruct((b, (nj - nqk) * MXU_N, s), BF16)),
        grid=(b, s // tm, nj),
        in_specs=[
            pl.BlockSpec((1, tm, d), lambda bi, si, j: (bi, si, 0)),
            pl.BlockSpec((1, 6, d), lambda bi, si, j: (bi, 0, 0)),
            pl.BlockSpec((1, d), lambda bi, si, j: (0, 0)),
            pl.BlockSpec((d, MXU_N), lambda bi, si, j: (0, j)),
            pl.BlockSpec((1, 1, MXU_N), lambda bi, si, j: (jnp.minimum(j, nqk - 1), 0, 0)),
            pl.BlockSpec((1, 1, MXU_N), lambda bi, si, j: (jnp.minimum(j, nqk - 1), 0, 0)),
            pl.BlockSpec((MXU_N, MXU_N), lambda bi, si, j: (0, 0)),
            pl.BlockSpec((tm, MXU_N), lambda bi, si, j: (si, 0)),
            pl.BlockSpec((tm, MXU_N), lambda bi, si, j: (si, 0)),
        ],
        out_specs=(pl.BlockSpec((1, tm, MXU_N), lambda bi, si, j: (bi, si, jnp.minimum(j, nqk - 1))),
                   pl.BlockSpec((1, MXU_N, tm), lambda bi, si, j: (bi, jnp.maximum(j - nqk, 0), si))),
        scratch_shapes=[pltpu.VMEM((tm, d), BF16)],
        compiler_params=_cparams(("parallel", "parallel", "arbitrary")),
        name="proj_rope" if use_rope else "proj_ctx",
    )(x, mod, n1, w, gains, flags, seg, cos_t, sin_t)


def _attn_a_kernel(sink_ref, q_ref, kp_ref, kc_ref, kn_ref, vp_ref, vc_ref, vn_ref, kx_ref, vx_ref, o_ref):
    i = pl.program_id(1)
    nb = pl.num_programs(1)
    nctx = kx_ref.shape[1]
    span = 3 * BLOCK + nctx
    kall = jnp.concatenate([kp_ref[0], kc_ref[0], kn_ref[0], kx_ref[0]], axis=0)
    vall = jnp.concatenate([vp_ref[0], vc_ref[0], vn_ref[0], vx_ref[0]], axis=0)

    r = lax.broadcasted_iota(I32, (BLOCK, span), 0)
    c = lax.broadcasted_iota(I32, (BLOCK, span), 1)
    prev_ok = (c < BLOCK) & (c >= r) & (i > 0)
    cur_ok = (c >= BLOCK) & (c < 2 * BLOCK)
    next_ok = (c >= 2 * BLOCK) & (c < 3 * BLOCK) & (c - 2 * BLOCK <= r) & (i < nb - 1)
    valid = prev_ok | cur_ok | next_ok | (c >= 3 * BLOCK)
    bias = jnp.where(valid, 0.0, NEG_BIG)

    lane = lax.broadcasted_iota(I32, (BLOCK, LANES), 1)
    for kv in range(A_KV_HEADS):
        in_half = (lane >= kv * HEAD_DIM) & (lane < (kv + 1) * HEAD_DIM)
        qs = []
        for g in range(A_GROUP):
            hh = kv * A_GROUP + g
            blk = q_ref[0, :, (hh // 2) * LANES:(hh // 2 + 1) * LANES].astype(F32)
            if hh % 2 != kv:
                blk = pltpu.roll(blk, HEAD_DIM, 1)
            qs.append(jnp.where(in_half, blk, 0.0).astype(BF16))
        s_all = _nt_dot(jnp.concatenate(qs, axis=0), kall)
        ps, ls = [], []
        for g in range(A_GROUP):
            sink = sink_ref[kv, g]
            s = s_all[g * BLOCK:(g + 1) * BLOCK] + bias
            m = jnp.maximum(jnp.max(s, axis=-1, keepdims=True), sink)
            p = jnp.exp(s - m)
            ls.append(jnp.sum(p, axis=-1, keepdims=True) + jnp.exp(sink - m))
            ps.append(p.astype(BF16))
        o_all = _dot(jnp.concatenate(ps, axis=0), vall)
        os_ = [o_all[g * BLOCK:(g + 1) * BLOCK] / ls[g] for g in range(A_GROUP)]
        for pair in range(A_GROUP // 2):
            o0, o1 = os_[2 * pair], os_[2 * pair + 1]
            if kv == 0:
                o1 = pltpu.roll(o1, HEAD_DIM, 1)
            else:
                o0 = pltpu.roll(o0, HEAD_DIM, 1)
            col = (kv * (A_GROUP // 2) + pair) * LANES
            o_ref[0, :, col:col + LANES] = jnp.where(lane < HEAD_DIM, o0, o1).astype(BF16)


def _attn_a(sink, lat, ctxp):
    b, s, _ = lat.shape
    nctx = ctxp.shape[1]
    nb = s // BLOCK
    width = A_Q_HEADS * HEAD_DIM

    def kspec(col, shift):
        return pl.BlockSpec((1, BLOCK, LANES),
                            lambda bi, i: (bi, jnp.clip(i + shift, 0, nb - 1), col))

    return pl.pallas_call(
        _attn_a_kernel,
        out_shape=jax.ShapeDtypeStruct((b, s, width), BF16),
        grid=(b, nb),
        in_specs=[
            pl.BlockSpec(memory_space=pltpu.SMEM),
            pl.BlockSpec((1, BLOCK, width), lambda bi, i: (bi, i, LAT_QA * LANES // width)),
            kspec(LAT_KA, -1), kspec(LAT_KA, 0), kspec(LAT_KA, 1),
            kspec(LAT_VA, -1), kspec(LAT_VA, 0), kspec(LAT_VA, 1),
            pl.BlockSpec((1, nctx, LANES), lambda bi, i: (bi, 0, CTX_KA)),
            pl.BlockSpec((1, nctx, LANES), lambda bi, i: (bi, 0, CTX_VA)),
        ],
        out_specs=pl.BlockSpec((1, BLOCK, width), lambda bi, i: (bi, i, 0)),
        compiler_params=_cparams(("parallel", "parallel")),
        name="attn_a",
    )(sink, lat, lat, lat, lat, lat, lat, lat, ctxp, ctxp)


def _attn_b_kernel(bound_ref, lamv_ref, sub_ref, q_ref, k_ref, vt_ref, kx_ref, vtx_ref, o_ref, acc_ref, l_ref,
                   s0_ref, s1_ref, sx_ref, *, tk):
    tq = q_ref.shape[1]
    s_len = k_ref.shape[1]
    qt = q_ref[0].astype(F32).T
    row = lax.broadcasted_iota(I32, (LANES, tq), 0)
    qts = [jnp.where(row < HEAD_DIM, qt, 0.0).astype(BF16), jnp.where(row >= HEAD_DIM, qt, 0.0).astype(BF16)]
    acc_ref[...] = jnp.zeros(acc_ref.shape, F32)
    n = s_len // tk
    bound = bound_ref[0]

    def kchunk(i):
        return k_ref[0, pl.ds(pl.multiple_of(i * tk, tk), tk), :]

    def vchunk(i):
        return vt_ref[0, :, pl.ds(pl.multiple_of(i * tk, tk), tk)]

    def run(fixed):
        def scores(j, kc, s_ref):
            s = _dot(kc, qts[j])
            s_ref[j] = s
            return jnp.zeros((1, tq), F32) if fixed else jnp.max(s, axis=0, keepdims=True)

        def accumulate(j, s_ref, cm, vtc, stat):
            m_old, l_old = stat
            if fixed:
                p = jnp.exp2(s_ref[j] - bound)
                acc_ref[j] += _dot(vtc, p.astype(BF16))
                return m_old, l_old + jnp.sum(p, axis=0, keepdims=True)
            m_new = jnp.maximum(m_old, cm)
            alpha = jnp.exp2(m_old - m_new)
            p = jnp.exp2(s_ref[j] - m_new)
            acc_ref[j] = alpha * acc_ref[j] + _dot(vtc, p.astype(BF16))
            return m_new, alpha * l_old + jnp.sum(p, axis=0, keepdims=True)

        def half_step(k_next, s_next, s_cur, cms, vtc, stats):
            new_cms, new_stats = [], []
            for j in range(2):
                new_cms.append(scores(j, k_next, s_next))
                new_stats.append(accumulate(j, s_cur, cms[j], vtc, stats[j]))
            return tuple(new_cms), tuple(new_stats)

        m0 = jnp.zeros((1, tq), F32) if fixed else jnp.full((1, tq), NEG_BIG, F32)
        stats = ((m0, jnp.zeros((1, tq), F32)),) * 2
        cmx = tuple(scores(j, kx_ref[0], sx_ref) for j in range(2))
        cms, stats = half_step(kchunk(0), s0_ref, sx_ref, cmx, vtx_ref[0], stats)

        def body(pair, carry):
            cms, stats = carry
            i = 2 * pair
            cms, stats = half_step(kchunk(i + 1), s1_ref, s0_ref, cms, vchunk(i), stats)
            return half_step(kchunk(jnp.minimum(i + 2, n - 1)), s0_ref, s1_ref, cms, vchunk(i + 1), stats)

        _, ((_, l0), (_, l1)) = lax.fori_loop(0, n // 2, body, (cms, stats))
        l_ref[0] = l0
        l_ref[1] = l1

    pl.when(bound <= MAX_FIXED_SHIFT)(lambda: run(True))
    pl.when(bound > MAX_FIXED_SHIFT)(lambda: run(False))


    lv = lamv_ref[...]
    lam = (jnp.exp(jnp.sum(lv[0:1] * lv[1:2], axis=-1, keepdims=True))
           - jnp.exp(jnp.sum(lv[2:3] * lv[3:4], axis=-1, keepdims=True)) + LAM_INIT)
    o = acc_ref[0] * (1.0 / l_ref[0]) - acc_ref[1] * (lam / l_ref[1])
    ms = jnp.mean(o * o, axis=0, keepdims=True)
    o = o * lax.rsqrt(ms + EPS) * (sub_ref[...] * (1.0 - LAM_INIT))
    o_ref[0] = o.T.astype(BF16)


def _attn_b(bound, lamv, subln_col, lat, vt, ctxp, vtx):
    b, s, _ = lat.shape
    nctx = ctxp.shape[1]
    tq = _tile(s, 256)
    tk = min(512, s // 2)
    assert s % (2 * tk) == 0
    return pl.pallas_call(
        functools.partial(_attn_b_kernel, tk=tk),
        out_shape=jax.ShapeDtypeStruct((b, s, B_HEADS * LANES), BF16),
        grid=(b, B_HEADS, s // tq),
        in_specs=[
            pl.BlockSpec(memory_space=pltpu.SMEM),
            pl.BlockSpec((4, HEAD_DIM), lambda bi, h, qi: (0, 0)),
            pl.BlockSpec((LANES, 1), lambda bi, h, qi: (0, 0)),
            pl.BlockSpec((1, tq, LANES), lambda bi, h, qi: (bi, qi, LAT_QB + h)),
            pl.BlockSpec((1, s, LANES), lambda bi, h, qi: (bi, 0, LAT_KB + h)),
            pl.BlockSpec((1, LANES, s), lambda bi, h, qi: (bi, h, 0)),
            pl.BlockSpec((1, nctx, LANES), lambda bi, h, qi: (bi, 0, CTX_KB + h)),
            pl.BlockSpec((1, LANES, nctx), lambda bi, h, qi: (bi, h, 0)),
        ],
        out_specs=pl.BlockSpec((1, tq, LANES), lambda bi, h, qi: (bi, qi, h)),
        scratch_shapes=[pltpu.VMEM((2, LANES, tq), F32), pltpu.VMEM((2, 1, tq), F32),
                        pltpu.VMEM((2, tk, tq), F32), pltpu.VMEM((2, tk, tq), F32),
                        pltpu.VMEM((2, nctx, tq), F32)],
        compiler_params=_cparams(("parallel", "parallel", "arbitrary")),
        name="attn_b",
    )(bound, lamv, subln_col, lat, lat, vt, ctxp, vtx)


def _pack_rows(h):
    n = h.shape[1] // 2
    bits = pltpu.bitcast(h.astype(BF16).astype(F32), U32)
    return (bits[:, :n] >> 16) | (bits[:, n:] & jnp.uint32(0xFFFF0000))


def _unpack_rows(w):
    lo = pltpu.bitcast(w << 16, F32)
    hi = pltpu.bitcast(w & jnp.uint32(0xFFFF0000), F32)
    return jnp.concatenate([lo, hi], axis=1).astype(BF16)


def _merge_kernel(x_ref, mod_ref, n1_ref, n2_ref, oa_ref, ob_ref, wg_ref, wpa_ref, wpb_ref, wo_ref, wr_ref,
                  x1_ref, h2_ref, lg_ref):
    d = x_ref.shape[2]
    xf = x_ref[0]
    h = _modulated_norm(xf, n1_ref[...], mod_ref[0, 0:1, :], mod_ref[0, 1:2, :]).astype(BF16)
    gates = _sigmoid(_dot(h, wg_ref[...]))
    ya = _dot(oa_ref[0], wpa_ref[...])
    yb = _dot(ob_ref[0], wpb_ref[...])
    z = gates[:, :d] * ya + gates[:, d:] * yb
    x1 = xf + mod_ref[0, 2:3, :] * _dot(z.astype(BF16), wo_ref[...])
    x1_ref[0] = x1
    h2 = _modulated_norm(x1, n2_ref[...], mod_ref[0, 3:4, :], mod_ref[0, 4:5, :])
    lg_ref[0] = _dot(h2.astype(BF16), wr_ref[...])
    h2_ref[0] = _pack_rows(h2)


def _merge(x, mod, n1, n2, oa, ob, wg, wpa, wpb, wo, wr):
    b, s, d = x.shape
    tm = _tile(s, 256)
    ne = wr.shape[1]
    const = lambda bi, si: (0, 0)
    row = lambda bi, si: (bi, si, 0)
    return pl.pallas_call(
        _merge_kernel,
        out_shape=(jax.ShapeDtypeStruct((b, s, d), F32),
                   jax.ShapeDtypeStruct((b, s, d // 2), U32),
                   jax.ShapeDtypeStruct((b, s, ne), F32)),
        grid=(b, s // tm),
        in_specs=[
            pl.BlockSpec((1, tm, d), row),
            pl.BlockSpec((1, 6, d), lambda bi, si: (bi, 0, 0)),
            pl.BlockSpec((1, d), const), pl.BlockSpec((1, d), const),
            pl.BlockSpec((1, tm, d), row), pl.BlockSpec((1, tm, d), row),
            pl.BlockSpec((d, 2 * d), const), pl.BlockSpec((d, d), const),
            pl.BlockSpec((d, d), const), pl.BlockSpec((d, d), const),
            pl.BlockSpec((d, ne), const),
        ],
        out_specs=(pl.BlockSpec((1, tm, d), row), pl.BlockSpec((1, tm, d // 2), row),
                   pl.BlockSpec((1, tm, ne), row)),
        compiler_params=_cparams(("parallel", "parallel")),
        name="merge",
    )(x, mod, n1, n2, oa, ob, wg, wpa, wpb, wo, wr)


def _router_kernel(lg_ref, bias_ref, idx_ref, w_ref, rank_ref, cnt_ref, run_ref):
    i = pl.program_id(0)
    tr = lg_ref.shape[0]

    @pl.when(i == 0)
    def _():
        run_ref[...] = jnp.zeros(run_ref.shape, F32)

    scores = _sigmoid(lg_ref[...])
    biased = scores + bias_ref[...]
    lane = lax.broadcasted_iota(I32, scores.shape, 1)
    neg_inf = -jnp.inf

    def first_argmax(vals):
        m = jnp.max(vals, axis=-1, keepdims=True)
        idx = jnp.min(jnp.where(vals == m, lane, N_EXPERTS), axis=-1, keepdims=True)
        return m, idx

    gscore = []
    for g in range(N_GROUPS):
        in_g = (lane >= g * GROUP_SIZE) & (lane < (g + 1) * GROUP_SIZE)
        vals = jnp.where(in_g, biased, neg_inf)
        m1, i1 = first_argmax(vals)
        m2 = jnp.max(jnp.where(lane == i1, neg_inf, vals), axis=-1, keepdims=True)
        gscore.append(m1 + m2)
    emask = jnp.zeros(scores.shape, jnp.bool_)
    for g in range(N_GROUPS):
        beaten = jnp.zeros((tr, 1), F32)
        for g2 in range(N_GROUPS):
            if g2 == g:
                continue
            wins = (gscore[g2] > gscore[g]) | ((gscore[g2] == gscore[g]) & (g2 < g))
            beaten = beaten + jnp.where(wins, 1.0, 0.0)
        in_g = (lane >= g * GROUP_SIZE) & (lane < (g + 1) * GROUP_SIZE)
        emask = emask | (in_g & (beaten < TOPK_GROUPS))

    masked = jnp.where(emask, biased, neg_inf)
    sel = jnp.zeros(scores.shape, F32)
    idxs, ws = [], []
    for _ in range(TOP_K):
        _, ik = first_argmax(masked)
        hit = lane == ik
        idxs.append(ik)
        ws.append(jnp.sum(jnp.where(hit, scores, 0.0), axis=-1, keepdims=True))
        sel = jnp.where(hit, 1.0, sel)
        masked = jnp.where(hit, neg_inf, masked)
    wsum = ws[0]
    for k in range(1, TOP_K):
        wsum = wsum + ws[k]

    rr = lax.broadcasted_iota(I32, (tr, tr), 0)
    cc = lax.broadcasted_iota(I32, (tr, tr), 1)
    lower = jnp.where(cc < rr, 1.0, 0.0).astype(BF16)
    before = _dot(lower, sel.astype(BF16)) + run_ref[...]
    ranks = [jnp.sum(jnp.where(lane == idxs[k], before, 0.0), axis=-1, keepdims=True) for k in range(TOP_K)]
    run_ref[...] = run_ref[...] + jnp.sum(sel, axis=0, keepdims=True)
    cnt_ref[...] = run_ref[...]

    idx_ref[...] = jnp.concatenate(idxs, axis=1)
    w_ref[...] = jnp.concatenate([w / wsum * ROUTED_SCALE for w in ws], axis=1)
    rank_ref[...] = jnp.concatenate(ranks, axis=1).astype(I32)


def _router(logits, bias):
    t, ne = logits.shape
    tr = _tile(t, 512)
    return pl.pallas_call(
        _router_kernel,
        out_shape=(jax.ShapeDtypeStruct((t, TOP_K), I32), jax.ShapeDtypeStruct((t, TOP_K), F32),
                   jax.ShapeDtypeStruct((t, TOP_K), I32), jax.ShapeDtypeStruct((1, ne), F32)),
        grid=(t // tr,),
        in_specs=[pl.BlockSpec((tr, ne), lambda i: (i, 0)), pl.BlockSpec((1, ne), lambda i: (0, 0))],
        out_specs=(pl.BlockSpec((tr, TOP_K), lambda i: (i, 0)), pl.BlockSpec((tr, TOP_K), lambda i: (i, 0)),
                   pl.BlockSpec((tr, TOP_K), lambda i: (i, 0)), pl.BlockSpec((1, ne), lambda i: (0, 0))),
        scratch_shapes=[pltpu.VMEM((1, ne), F32)],
        compiler_params=_cparams(("arbitrary",)),
        name="router",
    )(logits, bias)


def _dest_kernel(idx_ref, rank_ref, pstart_ref, dest_ref):
    idx = idx_ref[...]
    lane = lax.broadcasted_iota(I32, (idx.shape[0], N_EXPERTS), 1)
    cols = []
    for k in range(TOP_K):
        start = jnp.sum(jnp.where(lane == idx[:, k:k + 1], pstart_ref[...], 0.0), axis=-1, keepdims=True)
        cols.append(start.astype(I32) + rank_ref[:, k:k + 1])
    dest_ref[...] = jnp.concatenate(cols, axis=1)


def _dest(idx, rank, pstart):
    t = idx.shape[0]
    tr = _tile(t, 512)
    return pl.pallas_call(
        _dest_kernel,
        out_shape=jax.ShapeDtypeStruct((t, TOP_K), I32),
        grid=(t // tr,),
        in_specs=[pl.BlockSpec((tr, TOP_K), lambda i: (i, 0)), pl.BlockSpec((tr, TOP_K), lambda i: (i, 0)),
                  pl.BlockSpec((1, N_EXPERTS), lambda i: (0, 0))],
        out_specs=pl.BlockSpec((tr, TOP_K), lambda i: (i, 0)),
        compiler_params=_cparams(("parallel",)),
        name="dest",
    )(idx, rank, pstart)


def _dispatch_kernel(dest_hbm, h2_ref, xs_in, xs_out, dest_smem, sem_idx, sem_rows):
    del xs_in
    i = pl.program_id(0)
    td = h2_ref.shape[0]
    n = td * TOP_K
    cp = pltpu.make_async_copy(dest_hbm.at[pl.ds(i * n, n)], dest_smem, sem_idx)
    cp.start()
    cp.wait()

    def row_copy(t, d):
        return pltpu.make_async_copy(h2_ref.at[pl.ds(t, 1), :], xs_out.at[pl.ds(d, 1), :], sem_rows)

    def issue(t, carry):
        for k in range(TOP_K):
            row_copy(t, dest_smem[t * TOP_K + k]).start(priority=k % 2)
        return carry

    lax.fori_loop(0, td, issue, 0)

    def drain(t, carry):
        for k in range(TOP_K):
            row_copy(0, 0).wait()
        return carry

    lax.fori_loop(0, td, drain, 0)


def _dispatch(dest_flat, h2p, xs_init):
    t, half = h2p.shape
    td = _tile(t, 256)
    return pl.pallas_call(
        _dispatch_kernel,
        out_shape=jax.ShapeDtypeStruct(xs_init.shape, U32),
        grid=(t // td,),
        in_specs=[pl.BlockSpec(memory_space=pl.ANY),
                  pl.BlockSpec((td, half), lambda i: (i, 0)),
                  pl.BlockSpec(memory_space=pl.ANY)],
        out_specs=pl.BlockSpec(memory_space=pl.ANY),
        scratch_shapes=[pltpu.SMEM((td * TOP_K,), I32), pltpu.SemaphoreType.DMA, pltpu.SemaphoreType.DMA],
        input_output_aliases={2: 0},
        compiler_params=pltpu.CompilerParams(dimension_semantics=("arbitrary",), vmem_limit_bytes=VMEM_LIMIT,
                                             has_side_effects=True),
        name="dispatch",
    )(dest_flat, h2p, xs_init)


def _experts_kernel(be_ref, nused_ref, xs_ref, wg_ref, wu_ref, wd_ref, ys_ref, wgb_ref, wub_ref, wdb_ref):
    blk = pl.program_id(0)
    used = blk < nused_ref[0]

    @pl.when(used & ((blk == 0) | (be_ref[blk] != be_ref[jnp.maximum(blk - 1, 0)])))
    def _():
        wgb_ref[...] = wg_ref[0].astype(BF16)
        wub_ref[...] = wu_ref[0].astype(BF16)
        wdb_ref[...] = wd_ref[0].astype(BF16)

    @pl.when(used)
    def _():
        x = _unpack_rows(xs_ref[...])
        g = _dot(x, wgb_ref[...])
        u = _dot(x, wub_ref[...])
        a = (_silu(g) * u).astype(BF16)
        ys_ref[...] = _dot(a, wdb_ref[...])

    @pl.when(blk >= nused_ref[0])
    def _():
        ys_ref[...] = jnp.zeros(ys_ref.shape, F32)


def _experts(blk_expert, nused, xs, wg, wu, wd):
    p, half = xs.shape
    d = 2 * half
    de = wg.shape[2]
    nblk = p // MOE_BLOCK
    grid_spec = pltpu.PrefetchScalarGridSpec(
        num_scalar_prefetch=2,
        grid=(nblk,),
        in_specs=[
            pl.BlockSpec((MOE_BLOCK, half), lambda i, be, nu: (i, 0)),
            pl.BlockSpec((1, d, de), lambda i, be, nu: (be[i], 0, 0)),
            pl.BlockSpec((1, d, de), lambda i, be, nu: (be[i], 0, 0)),
            pl.BlockSpec((1, de, d), lambda i, be, nu: (be[i], 0, 0)),
        ],
        out_specs=pl.BlockSpec((MOE_BLOCK, d), lambda i, be, nu: (i, 0)),
        scratch_shapes=[pltpu.VMEM((d, de), BF16), pltpu.VMEM((d, de), BF16), pltpu.VMEM((de, d), BF16)],
    )
    return pl.pallas_call(
        _experts_kernel,
        out_shape=jax.ShapeDtypeStruct((p, d), F32),
        grid_spec=grid_spec,
        compiler_params=_cparams(("arbitrary",)),
        name="experts",
    )(blk_expert, nused, xs, wg, wu, wd)


def _combine_kernel(dest_hbm, w_ref, ys_hbm, h2_ref, x1_ref, mod_ref, wgs_ref, wus_ref, wds_ref, o_ref,
                    dest_smem, buf, sem_idx, sem_rows):
    i = pl.program_id(0)
    tc = h2_ref.shape[0]
    n = tc * TOP_K
    cp = pltpu.make_async_copy(dest_hbm.at[pl.ds(i * n, n)], dest_smem, sem_idx)
    cp.start()
    cp.wait()

    def row_copy(t, k, d):
        return pltpu.make_async_copy(ys_hbm.at[pl.ds(d, 1), :], buf.at[k, pl.ds(t, 1), :], sem_rows)

    def issue(t, carry):
        for k in range(TOP_K):
            row_copy(t, k, dest_smem[t * TOP_K + k]).start(priority=k % 2)
        return carry

    lax.fori_loop(0, tc, issue, 0)

    x = _unpack_rows(h2_ref[...])
    a = (_silu(_dot(x, wgs_ref[...])) * _dot(x, wus_ref[...])).astype(BF16)
    y = _dot(a, wds_ref[...])

    def drain(t, carry):
        for k in range(TOP_K):
            row_copy(0, k, 0).wait()
        return carry

    lax.fori_loop(0, tc, drain, 0)

    for k in range(TOP_K):
        y = y + buf[k] * w_ref[:, k:k + 1]
    o_ref[...] = x1_ref[...] + mod_ref[0, 5:6, :] * y


def _combine(dest_flat, w, ys, h2p, x1, mod, wgs, wus, wds, seq):
    t, d = x1.shape
    half = d // 2
    tc = _tile(seq, 128)
    de = wgs.shape[1]
    per_batch = seq // tc
    const = lambda i: (0, 0)
    return pl.pallas_call(
        _combine_kernel,
        out_shape=jax.ShapeDtypeStruct((t, d), F32),
        grid=(t // tc,),
        in_specs=[
            pl.BlockSpec(memory_space=pl.ANY),
            pl.BlockSpec((tc, TOP_K), lambda i: (i, 0)),
            pl.BlockSpec(memory_space=pl.ANY),
            pl.BlockSpec((tc, half), lambda i: (i, 0)),
            pl.BlockSpec((tc, d), lambda i: (i, 0)),
            pl.BlockSpec((1, 6, d), lambda i: (i // per_batch, 0, 0)),
            pl.BlockSpec((d, de), const), pl.BlockSpec((d, de), const), pl.BlockSpec((de, d), const),
        ],
        out_specs=pl.BlockSpec((tc, d), lambda i: (i, 0)),
        scratch_shapes=[pltpu.SMEM((tc * TOP_K,), I32), pltpu.VMEM((TOP_K, tc, d), F32),
                        pltpu.SemaphoreType.DMA, pltpu.SemaphoreType.DMA],
        compiler_params=_cparams(("arbitrary",)),
        name="combine",
    )(dest_flat, w, ys, h2p, x1, mod, wgs, wus, wds)


def _rope_tables(n):
    rows = n // GRID_W
    row = jnp.broadcast_to(jnp.arange(rows)[:, None], (rows, GRID_W)).reshape(-1)
    col = jnp.broadcast_to(jnp.arange(GRID_W)[None, :], (rows, GRID_W)).reshape(-1)
    freqs = ROPE_BASE ** (-jnp.arange(ROPE_FREQS, dtype=F32) / ROPE_FREQS)
    pos = jnp.stack([row, col], axis=-1).astype(F32)
    ang = pos[:, :, None] * freqs
    cos, sin = jnp.cos(ang), jnp.sin(ang)
    cos64 = jnp.concatenate([cos[:, 0], cos[:, 0], cos[:, 1], cos[:, 1]], axis=-1)
    sin64 = jnp.concatenate([-sin[:, 0], sin[:, 0], -sin[:, 1], sin[:, 1]], axis=-1)
    reps = MXU_N // HEAD_DIM
    return jnp.tile(cos64, (1, reps)), jnp.tile(sin64, (1, reps))


def _gain_rows(qn_a, kn_a, qn_b, kn_b):
    reps = MXU_N // HEAD_DIM
    ones = jnp.ones((LANES,), F32)
    kb = jnp.tile(kn_b, reps)
    qa = jnp.tile(qn_a, reps) * ATTN_SCALE
    qb = jnp.tile(qn_b, reps) * (ATTN_SCALE * LOG2E)
    mixed = jnp.concatenate([jnp.tile(kn_a, LANES // HEAD_DIM), ones])
    mixed_flag = jnp.concatenate([ones, 0.0 * ones])
    full_flag = jnp.ones((MXU_N,), F32)
    lat_g = jnp.stack([kb] * 4 + [qa] * 4 + [qb] * 4 + [mixed])[:, None, :]
    lat_f = jnp.stack([full_flag] * 12 + [mixed_flag])[:, None, :]
    ctx_g = jnp.stack([kb] * 4 + [mixed])[:, None, :]
    ctx_f = jnp.stack([full_flag] * 4 + [mixed_flag])[:, None, :]
    return lat_g, lat_f, ctx_g, ctx_f


def kernel(x, c, ctx, c_ctx, w_ada, b_ada, norm1_g, norm2_g, w_in, qnorm_a, knorm_a, sink_a, qnorm_b, knorm_b,
           lam_q1, lam_k1, lam_q2, lam_k2, subln_g, w_pa, w_pb, w_o, w_router, router_bias, w_gate_e, w_up_e,
           w_down_e, w_gate_s, w_up_s, w_down_s):
    b, s, d = x.shape
    nctx = ctx.shape[1]
    assert w_ada.shape[0] == 1 and d == 1024 and s % BLOCK == 0 and s % GRID_W == 0
    t = b * s

    cc = jnp.concatenate([c, c_ctx[None, :], jnp.zeros((16 - b - 1, d), F32)], axis=0)
    mod_all = _ada(cc, w_ada[0], b_ada[0])
    mod = mod_all[:b].reshape(b, 6, d)
    mod_c = jnp.broadcast_to(mod_all[b].reshape(1, 6, d), (b, 6, d))

    w = w_in[0]
    ka_w, va_w = w[:, 0:128], w[:, 128:256]
    kb_w, vb_w = w[:, 256:1280], w[:, 1280:2304]
    qa_w, qb_w = w[:, 2304:3328], w[:, 3328:4352]
    gate_w = w[:, 4352:6400].astype(BF16)
    w_lat = jnp.concatenate([kb_w, qa_w, qb_w, ka_w, va_w, vb_w], axis=1).astype(BF16)
    w_ctx = jnp.concatenate([kb_w, ka_w, va_w, vb_w], axis=1).astype(BF16)

    lat_g, lat_f, ctx_g, ctx_f = _gain_rows(qnorm_a[0], knorm_a[0], qnorm_b[0], knorm_b[0])
    hid = jnp.arange(MXU_N) // HEAD_DIM
    seg = (hid[:, None] == hid[None, :]).astype(BF16)
    cos_t, sin_t = _rope_tables(s)
    n1 = norm1_g[0].reshape(1, d)
    n2 = norm2_g[0].reshape(1, d)

    lat, vt = _proj(x, mod, n1, w_lat, lat_g, lat_f, seg, cos_t, sin_t, nqk=LAT_NQK, use_rope=True)
    ctxp, vtx = _proj(ctx, mod_c, n1, w_ctx, ctx_g, ctx_f, seg, cos_t[:nctx], sin_t[:nctx], nqk=CTX_NQK,
                      use_rope=False)

    oa = _attn_a(sink_a[0], lat, ctxp)
    lamv = jnp.stack([lam_q1[0], lam_k1[0], lam_q2[0], lam_k2[0]])
    bound = (HEAD_DIM * ATTN_SCALE * LOG2E * 1.01) * jnp.max(jnp.abs(qnorm_b[0])) * jnp.max(jnp.abs(knorm_b[0]))
    ob = _attn_b(bound.reshape(1), lamv, subln_g[0].reshape(LANES, 1), lat, vt, ctxp, vtx)

    x1, h2p, logits = _merge(x, mod, n1, n2, oa, ob, gate_w, w_pa[0].astype(BF16), w_pb[0].astype(BF16),
                             w_o[0].astype(BF16), w_router[0].astype(BF16))
    x1 = x1.reshape(t, d)
    h2p = h2p.reshape(t, d // 2)
    logits = logits.reshape(t, N_EXPERTS)

    idx, wts, rank, counts = _router(logits, router_bias[0].reshape(1, N_EXPERTS))

    cnt = counts[0].astype(I32)
    padded = (cnt + MOE_BLOCK - 1) // MOE_BLOCK * MOE_BLOCK
    pend = jnp.cumsum(padded)
    pstart = pend - padded
    nblk = -(-(t * TOP_K) // MOE_BLOCK) + N_EXPERTS
    blk_row0 = jnp.arange(nblk, dtype=I32) * MOE_BLOCK
    blk_expert = jnp.minimum(jnp.sum((pend[None, :] <= blk_row0[:, None]).astype(I32), axis=1), N_EXPERTS - 1)
    nused = (pend[-1:] // MOE_BLOCK).astype(I32)

    dest = _dest(idx, rank, pstart.astype(F32).reshape(1, N_EXPERTS)).reshape(t * TOP_K)
    xs = _dispatch(dest, h2p, jnp.zeros((nblk * MOE_BLOCK, d // 2), U32))
    ys = _experts(blk_expert, nused, xs, w_gate_e[0], w_up_e[0], w_down_e[0])
    out = _combine(dest, wts, ys, h2p, x1, mod, w_gate_s[0].astype(BF16), w_up_s[0].astype(BF16),
                   w_down_s[0].astype(BF16), s)
    return out.reshape(b, s, d)
```

```python
import functools
import math

import jax
import jax.numpy as jnp
from jax import lax
from jax.experimental import pallas as pl
from jax.experimental.pallas import tpu as pltpu

F32 = jnp.float32
BF16 = jnp.bfloat16
I32 = jnp.int32
U32 = jnp.uint32

HEAD_DIM = 64
GRID_W = 64
ROPE_FREQS = HEAD_DIM // 4
ROPE_BASE = 10000.0
EPS = 1e-6
ATTN_SCALE = HEAD_DIM ** -0.5
BLOCK = 128
A_Q_HEADS = 16
A_KV_HEADS = 2
A_GROUP = A_Q_HEADS // A_KV_HEADS
B_HEADS = 8
N_EXPERTS = 256
TOP_K = 8
N_GROUPS = 8
TOPK_GROUPS = 4
GROUP_SIZE = N_EXPERTS // N_GROUPS
ROUTED_SCALE = 2.5
MOE_BLOCK = 256
LAM_INIT = 0.8 - 0.6 * math.exp(-0.3 * 0)

LANES = 128
MXU_N = 256
VMEM_LIMIT = 56 * 1024 * 1024
NEG_BIG = -1e30
LOG2E = math.log2(math.e)
MAX_FIXED_SHIFT = 60.0

LAT_KB, LAT_QA, LAT_QB, LAT_KA, LAT_VA = 0, 8, 16, 24, 25
LAT_NQK = 13
CTX_KB, CTX_KA, CTX_VA = 0, 8, 9
CTX_NQK = 5


def _tile(n, pref):
    return pref if n % pref == 0 else n


def _cparams(sem):
    return pltpu.CompilerParams(dimension_semantics=sem, vmem_limit_bytes=VMEM_LIMIT)


def _nt_dot(a, b):
    return lax.dot_general(a, b, (((1,), (1,)), ((), ())), preferred_element_type=F32)


def _dot(a, b):
    return jnp.dot(a, b, preferred_element_type=F32)


def _silu(x):
    return x * (1.0 / (1.0 + jnp.exp(-x)))


def _sigmoid(x):
    return 1.0 / (1.0 + jnp.exp(-x))


def _modulated_norm(xf, g_row, shift_row, scale_row):
    ms = jnp.mean(xf * xf, axis=-1, keepdims=True)
    y = xf * lax.rsqrt(ms + EPS) * g_row
    return y * (1.0 + scale_row) + shift_row


def _ada_kernel(c_ref, w_ref, b_ref, o_ref):
    o_ref[...] = _dot(_silu(c_ref[...]).astype(BF16), w_ref[...].astype(BF16)) + b_ref[...]


def _ada(cc, w_ada, b_ada):
    rows, d = cc.shape
    n = w_ada.shape[1]
    tn = _tile(n, 512)
    return pl.pallas_call(
        _ada_kernel,
        out_shape=jax.ShapeDtypeStruct((rows, n), F32),
        grid=(n // tn,),
        in_specs=[pl.BlockSpec((rows, d), lambda j: (0, 0)),
                  pl.BlockSpec((d, tn), lambda j: (0, j)),
                  pl.BlockSpec((1, tn), lambda j: (0, j))],
        out_specs=pl.BlockSpec((rows, tn), lambda j: (0, j)),
        compiler_params=_cparams(("parallel",)),
        name="ada",
    )(cc, w_ada, b_ada.reshape(1, n))


def _proj_kernel(x_ref, mod_ref, n1_ref, w_ref, gain_ref, seg_ref, cos_ref, sin_ref, o_ref, vt_ref,
                 *, nqk, use_rope):
    nj = w_ref.shape[1] // MXU_N
    h = _modulated_norm(x_ref[0], n1_ref[...], mod_ref[0, 0:1, :], mod_ref[0, 1:2, :]).astype(BF16)
    lane = lax.broadcasted_iota(I32, (1, MXU_N), 1)
    first = (lane % (2 * ROPE_FREQS)) < ROPE_FREQS

    def matmul(j):
        return _dot(h, w_ref[:, j * MXU_N:(j + 1) * MXU_N])

    def finish(j, acc):
        if j >= nqk:
            vt_ref[0, (j - nqk) * MXU_N:(j - nqk + 1) * MXU_N, :] = acc.T.astype(BF16)
            return
        ms = _dot((acc * acc).astype(BF16), seg_ref[...])
        y = acc * lax.rsqrt(ms + EPS) * gain_ref[j]
        if use_rope:
            rot = jnp.where(first, pltpu.roll(y, MXU_N - ROPE_FREQS, 1), pltpu.roll(y, ROPE_FREQS, 1))
            y = y * cos_ref[...] + rot * sin_ref[...]
        if j == nqk - 1:
            y = jnp.where(lane < LANES, y, acc)
        o_ref[0, :, j * MXU_N:(j + 1) * MXU_N] = y.astype(BF16)

    acc = matmul(0)
    for j in range(nj):
        nxt = matmul(j + 1) if j + 1 < nj else None
        finish(j, acc)
        acc = nxt


def _proj(x, mod, n1, w, gains, seg, cos_t, sin_t, *, nqk, use_rope):
    b, s, d = x.shape
    ncols = w.shape[1]
    tm = _tile(s, 512)
    nj = ncols // MXU_N
    const2 = lambda bi, si: (0, 0)
    return pl.pallas_call(
        functools.partial(_proj_kernel, nqk=nqk, use_rope=use_rope),
        out_shape=(jax.ShapeDtypeStruct((b, s, nqk * MXU_N), BF16),
                   jax.ShapeDtypeStruct((b, (nj - nqk) * MXU_N, s), BF16)),
        grid=(b, s // tm),
        in_specs=[
            pl.BlockSpec((1, tm, d), lambda bi, si: (bi, si, 0)),
            pl.BlockSpec((1, 6, d), lambda bi, si: (bi, 0, 0)),
            pl.BlockSpec((1, d), const2),
            pl.BlockSpec((d, ncols), const2),
            pl.BlockSpec((nqk, 1, MXU_N), lambda bi, si: (0, 0, 0)),
            pl.BlockSpec((MXU_N, MXU_N), const2),
            pl.BlockSpec((tm, MXU_N), lambda bi, si: (si, 0)),
            pl.BlockSpec((tm, MXU_N), lambda bi, si: (si, 0)),
        ],
        out_specs=(pl.BlockSpec((1, tm, nqk * MXU_N), lambda bi, si: (bi, si, 0)),
                   pl.BlockSpec((1, (nj - nqk) * MXU_N, tm), lambda bi, si: (bi, 0, si))),
        compiler_params=_cparams(("parallel", "parallel")),
        name="proj_rope" if use_rope else "proj_ctx",
    )(x, mod, n1, w, gains, seg, cos_t, sin_t)


def _attn_a_kernel(sink_ref, q_ref, kp_ref, kc_ref, kn_ref, vp_ref, vc_ref, vn_ref, kx_ref, vx_ref, o_ref):
    i = pl.program_id(1)
    nb = pl.num_programs(1)
    nctx = kx_ref.shape[1]
    span = 3 * BLOCK + nctx
    kall = jnp.concatenate([kp_ref[0], kc_ref[0], kn_ref[0], kx_ref[0]], axis=0)
    vall = jnp.concatenate([vp_ref[0], vc_ref[0], vn_ref[0], vx_ref[0]], axis=0)

    r = lax.broadcasted_iota(I32, (BLOCK, span), 0)
    c = lax.broadcasted_iota(I32, (BLOCK, span), 1)
    prev_ok = (c < BLOCK) & (c >= r) & (i > 0)
    cur_ok = (c >= BLOCK) & (c < 2 * BLOCK)
    next_ok = (c >= 2 * BLOCK) & (c < 3 * BLOCK) & (c - 2 * BLOCK <= r) & (i < nb - 1)
    valid = prev_ok | cur_ok | next_ok | (c >= 3 * BLOCK)
    bias = jnp.where(valid, 0.0, NEG_BIG)

    lane = lax.broadcasted_iota(I32, (BLOCK, LANES), 1)
    for kv in range(A_KV_HEADS):
        in_half = (lane >= kv * HEAD_DIM) & (lane < (kv + 1) * HEAD_DIM)
        qs = []
        for g in range(A_GROUP):
            hh = kv * A_GROUP + g
            blk = q_ref[0, :, (hh // 2) * LANES:(hh // 2 + 1) * LANES].astype(F32)
            if hh % 2 != kv:
                blk = pltpu.roll(blk, HEAD_DIM, 1)
            qs.append(jnp.where(in_half, blk, 0.0).astype(BF16))
        s_all = _nt_dot(jnp.concatenate(qs, axis=0), kall)
        ps, ls = [], []
        for g in range(A_GROUP):
            sink = sink_ref[kv, g]
            s = s_all[g * BLOCK:(g + 1) * BLOCK] + bias
            m = jnp.maximum(jnp.max(s, axis=-1, keepdims=True), sink)
            p = jnp.exp(s - m)
            ls.append(jnp.sum(p, axis=-1, keepdims=True) + jnp.exp(sink - m))
            ps.append(p.astype(BF16))
        o_all = _dot(jnp.concatenate(ps, axis=0), vall)
        os_ = [o_all[g * BLOCK:(g + 1) * BLOCK] / ls[g] for g in range(A_GROUP)]
        for pair in range(A_GROUP // 2):
            o0, o1 = os_[2 * pair], os_[2 * pair + 1]
            if kv == 0:
                o1 = pltpu.roll(o1, HEAD_DIM, 1)
            else:
                o0 = pltpu.roll(o0, HEAD_DIM, 1)
            col = (kv * (A_GROUP // 2) + pair) * LANES
            o_ref[0, :, col:col + LANES] = jnp.where(lane < HEAD_DIM, o0, o1).astype(BF16)


def _attn_a(sink, lat, ctxp):
    b, s, _ = lat.shape
    nctx = ctxp.shape[1]
    nb = s // BLOCK
    width = A_Q_HEADS * HEAD_DIM

    def kspec(col, shift):
        return pl.BlockSpec((1, BLOCK, LANES),
                            lambda bi, i: (bi, jnp.clip(i + shift, 0, nb - 1), col))

    return pl.pallas_call(
        _attn_a_kernel,
        out_shape=jax.ShapeDtypeStruct((b, s, width), BF16),
        grid=(b, nb),
        in_specs=[
            pl.BlockSpec(memory_space=pltpu.SMEM),
            pl.BlockSpec((1, BLOCK, width), lambda bi, i: (bi, i, LAT_QA * LANES // width)),
            kspec(LAT_KA, -1), kspec(LAT_KA, 0), kspec(LAT_KA, 1),
            kspec(LAT_VA, -1), kspec(LAT_VA, 0), kspec(LAT_VA, 1),
            pl.BlockSpec((1, nctx, LANES), lambda bi, i: (bi, 0, CTX_KA)),
            pl.BlockSpec((1, nctx, LANES), lambda bi, i: (bi, 0, CTX_VA)),
        ],
        out_specs=pl.BlockSpec((1, BLOCK, width), lambda bi, i: (bi, i, 0)),
        compiler_params=_cparams(("parallel", "parallel")),
        name="attn_a",
    )(sink, lat, lat, lat, lat, lat, lat, lat, ctxp, ctxp)


def _attn_b_kernel(bound_ref, lamv_ref, sub_ref, q_ref, k_ref, vt_ref, kx_ref, vtx_ref, o_ref, acc_ref, l_ref,
                   s0_ref, s1_ref, sx_ref, *, tk):
    tq = q_ref.shape[1]
    s_len = k_ref.shape[1]
    qt = q_ref[0].astype(F32).T
    row = lax.broadcasted_iota(I32, (LANES, tq), 0)
    qts = [jnp.where(row < HEAD_DIM, qt, 0.0).astype(BF16), jnp.where(row >= HEAD_DIM, qt, 0.0).astype(BF16)]
    acc_ref[...] = jnp.zeros(acc_ref.shape, F32)
    n = s_len // tk
    bound = bound_ref[0]

    def kchunk(i):
        return k_ref[0, pl.ds(pl.multiple_of(i * tk, tk), tk), :]

    def vchunk(i):
        return vt_ref[0, :, pl.ds(pl.multiple_of(i * tk, tk), tk)]

    def run(fixed):
        def scores(j, kc, s_ref):
            s = _dot(kc, qts[j])
            s_ref[j] = s
            return jnp.zeros((1, tq), F32) if fixed else jnp.max(s, axis=0, keepdims=True)

        def accumulate(j, s_ref, cm, vtc, stat):
            m_old, l_old = stat
            if fixed:
                p = jnp.exp2(s_ref[j] - bound)
                acc_ref[j] += _dot(vtc, p.astype(BF16))
                return m_old, l_old + jnp.sum(p, axis=0, keepdims=True)
            m_new = jnp.maximum(m_old, cm)
            alpha = jnp.exp2(m_old - m_new)
            p = jnp.exp2(s_ref[j] - m_new)
            acc_ref[j] = alpha * acc_ref[j] + _dot(vtc, p.astype(BF16))
            return m_new, alpha * l_old + jnp.sum(p, axis=0, keepdims=True)

        def half_step(k_next, s_next, s_cur, cms, vtc, stats):
            new_cms, new_stats = [], []
            for j in range(2):
                new_cms.append(scores(j, k_next, s_next))
                new_stats.append(accumulate(j, s_cur, cms[j], vtc, stats[j]))
            return tuple(new_cms), tuple(new_stats)

        m0 = jnp.zeros((1, tq), F32) if fixed else jnp.full((1, tq), NEG_BIG, F32)
        stats = ((m0, jnp.zeros((1, tq), F32)),) * 2
        cmx = tuple(scores(j, kx_ref[0], sx_ref) for j in range(2))
        cms, stats = half_step(kchunk(0), s0_ref, sx_ref, cmx, vtx_ref[0], stats)

        def body(pair, carry):
            cms, stats = carry
            i = 2 * pair
            cms, stats = half_step(kchunk(i + 1), s1_ref, s0_ref, cms, vchunk(i), stats)
            return half_step(kchunk(i + 2), s0_ref, s1_ref, cms, vchunk(i + 1), stats)

        cms, stats = lax.fori_loop(0, n // 2 - 1, body, (cms, stats))
        cms, stats = half_step(kchunk(n - 1), s1_ref, s0_ref, cms, vchunk(n - 2), stats)
        for j in range(2):
            l_ref[j] = accumulate(j, s1_ref, cms[j], vchunk(n - 1), stats[j])[1]

    pl.when(bound <= MAX_FIXED_SHIFT)(lambda: run(True))
    pl.when(bound > MAX_FIXED_SHIFT)(lambda: run(False))


    lv = lamv_ref[...]
    lam = (jnp.exp(jnp.sum(lv[0:1] * lv[1:2], axis=-1, keepdims=True))
           - jnp.exp(jnp.sum(lv[2:3] * lv[3:4], axis=-1, keepdims=True)) + LAM_INIT)
    o = acc_ref[0] * (1.0 / l_ref[0]) - acc_ref[1] * (lam / l_ref[1])
    ms = jnp.mean(o * o, axis=0, keepdims=True)
    o = o * lax.rsqrt(ms + EPS) * (sub_ref[...] * (1.0 - LAM_INIT))
    o_ref[0] = o.T.astype(BF16)


def _attn_b(bound, lamv, subln_col, lat, vt, ctxp, vtx):
    b, s, _ = lat.shape
    nctx = ctxp.shape[1]
    tq = _tile(s, 512)
    tk = min(512, s // 2)
    assert s % (2 * tk) == 0
    return pl.pallas_call(
        functools.partial(_attn_b_kernel, tk=tk),
        out_shape=jax.ShapeDtypeStruct((b, s, B_HEADS * LANES), BF16),
        grid=(b, B_HEADS, s // tq),
        in_specs=[
            pl.BlockSpec(memory_space=pltpu.SMEM),
            pl.BlockSpec((4, HEAD_DIM), lambda bi, h, qi: (0, 0)),
            pl.BlockSpec((LANES, 1), lambda bi, h, qi: (0, 0)),
            pl.BlockSpec((1, tq, LANES), lambda bi, h, qi: (bi, qi, LAT_QB + h)),
            pl.BlockSpec((1, s, LANES), lambda bi, h, qi: (bi, 0, LAT_KB + h)),
            pl.BlockSpec((1, LANES, s), lambda bi, h, qi: (bi, h, 0)),
            pl.BlockSpec((1, nctx, LANES), lambda bi, h, qi: (bi, 0, CTX_KB + h)),
            pl.BlockSpec((1, LANES, nctx), lambda bi, h, qi: (bi, h, 0)),
        ],
        out_specs=pl.BlockSpec((1, tq, LANES), lambda bi, h, qi: (bi, qi, h)),
        scratch_shapes=[pltpu.VMEM((2, LANES, tq), F32), pltpu.VMEM((2, 1, tq), F32),
                        pltpu.VMEM((2, tk, tq), F32), pltpu.VMEM((2, tk, tq), F32),
                        pltpu.VMEM((2, nctx, tq), F32)],
        compiler_params=_cparams(("parallel", "parallel", "arbitrary")),
        name="attn_b",
    )(bound, lamv, subln_col, lat, lat, vt, ctxp, vtx)


def _pack_rows(h):
    n = h.shape[1] // 2
    bits = pltpu.bitcast(h.astype(BF16).astype(F32), U32)
    return (bits[:, :n] >> 16) | (bits[:, n:] & jnp.uint32(0xFFFF0000))


def _unpack_rows(w):
    lo = pltpu.bitcast(w << 16, F32)
    hi = pltpu.bitcast(w & jnp.uint32(0xFFFF0000), F32)
    return jnp.concatenate([lo, hi], axis=1).astype(BF16)


def _merge_kernel(x_ref, mod_ref, n1_ref, n2_ref, oa_ref, ob_ref, wg_ref, wpa_ref, wpb_ref, wo_ref, wr_ref,
                  x1_ref, h2_ref, lg_ref):
    d = x_ref.shape[2]
    xf = x_ref[0]
    h = _modulated_norm(xf, n1_ref[...], mod_ref[0, 0:1, :], mod_ref[0, 1:2, :]).astype(BF16)
    gates = _sigmoid(_dot(h, wg_ref[...]))
    ya = _dot(oa_ref[0], wpa_ref[...])
    yb = _dot(ob_ref[0], wpb_ref[...])
    z = gates[:, :d] * ya + gates[:, d:] * yb
    x1 = xf + mod_ref[0, 2:3, :] * _dot(z.astype(BF16), wo_ref[...])
    x1_ref[0] = x1
    h2 = _modulated_norm(x1, n2_ref[...], mod_ref[0, 3:4, :], mod_ref[0, 4:5, :])
    lg_ref[0] = _dot(h2.astype(BF16), wr_ref[...])
    h2_ref[0] = _pack_rows(h2)


def _merge(x, mod, n1, n2, oa, ob, wg, wpa, wpb, wo, wr):
    b, s, d = x.shape
    tm = _tile(s, 256)
    ne = wr.shape[1]
    const = lambda bi, si: (0, 0)
    row = lambda bi, si: (bi, si, 0)
    return pl.pallas_call(
        _merge_kernel,
        out_shape=(jax.ShapeDtypeStruct((b, s, d), F32),
                   jax.ShapeDtypeStruct((b, s, d // 2), U32),
                   jax.ShapeDtypeStruct((b, s, ne), F32)),
        grid=(b, s // tm),
        in_specs=[
            pl.BlockSpec((1, tm, d), row),
            pl.BlockSpec((1, 6, d), lambda bi, si: (bi, 0, 0)),
            pl.BlockSpec((1, d), const), pl.BlockSpec((1, d), const),
            pl.BlockSpec((1, tm, d), row), pl.BlockSpec((1, tm, d), row),
            pl.BlockSpec((d, 2 * d), const), pl.BlockSpec((d, d), const),
            pl.BlockSpec((d, d), const), pl.BlockSpec((d, d), const),
            pl.BlockSpec((d, ne), const),
        ],
        out_specs=(pl.BlockSpec((1, tm, d), row), pl.BlockSpec((1, tm, d // 2), row),
                   pl.BlockSpec((1, tm, ne), row)),
        compiler_params=_cparams(("parallel", "parallel")),
        name="merge",
    )(x, mod, n1, n2, oa, ob, wg, wpa, wpb, wo, wr)


def _router_kernel(lg_ref, bias_ref, idx_ref, w_ref, rank_ref, cnt_ref, run_ref):
    i = pl.program_id(0)
    tr = lg_ref.shape[0]

    @pl.when(i == 0)
    def _():
        run_ref[...] = jnp.zeros(run_ref.shape, F32)

    scores = _sigmoid(lg_ref[...])
    biased = scores + bias_ref[...]
    lane = lax.broadcasted_iota(I32, scores.shape, 1)
    neg_inf = -jnp.inf

    def first_argmax(vals):
        m = jnp.max(vals, axis=-1, keepdims=True)
        idx = jnp.min(jnp.where(vals == m, lane, N_EXPERTS), axis=-1, keepdims=True)
        return m, idx

    gscore = []
    for g in range(N_GROUPS):
        in_g = (lane >= g * GROUP_SIZE) & (lane < (g + 1) * GROUP_SIZE)
        vals = jnp.where(in_g, biased, neg_inf)
        m1, i1 = first_argmax(vals)
        m2 = jnp.max(jnp.where(lane == i1, neg_inf, vals), axis=-1, keepdims=True)
        gscore.append(m1 + m2)
    emask = jnp.zeros(scores.shape, jnp.bool_)
    for g in range(N_GROUPS):
        beaten = jnp.zeros((tr, 1), F32)
        for g2 in range(N_GROUPS):
            if g2 == g:
                continue
            wins = (gscore[g2] > gscore[g]) | ((gscore[g2] == gscore[g]) & (g2 < g))
            beaten = beaten + jnp.where(wins, 1.0, 0.0)
        in_g = (lane >= g * GROUP_SIZE) & (lane < (g + 1) * GROUP_SIZE)
        emask = emask | (in_g & (beaten < TOPK_GROUPS))

    masked = jnp.where(emask, biased, neg_inf)
    sel = jnp.zeros(scores.shape, F32)
    idxs, ws = [], []
    for _ in range(TOP_K):
        _, ik = first_argmax(masked)
        hit = lane == ik
        idxs.append(ik)
        ws.append(jnp.sum(jnp.where(hit, scores, 0.0), axis=-1, keepdims=True))
        sel = jnp.where(hit, 1.0, sel)
        masked = jnp.where(hit, neg_inf, masked)
    wsum = ws[0]
    for k in range(1, TOP_K):
        wsum = wsum + ws[k]

    rr = lax.broadcasted_iota(I32, (tr, tr), 0)
    cc = lax.broadcasted_iota(I32, (tr, tr), 1)
    lower = jnp.where(cc < rr, 1.0, 0.0).astype(BF16)
    before = _dot(lower, sel.astype(BF16)) + run_ref[...]
    ranks = [jnp.sum(jnp.where(lane == idxs[k], before, 0.0), axis=-1, keepdims=True) for k in range(TOP_K)]
    run_ref[...] = run_ref[...] + jnp.sum(sel, axis=0, keepdims=True)
    cnt_ref[...] = run_ref[...]

    idx_ref[...] = jnp.concatenate(idxs, axis=1)
    w_ref[...] = jnp.concatenate([w / wsum * ROUTED_SCALE for w in ws], axis=1)
    rank_ref[...] = jnp.concatenate(ranks, axis=1).astype(I32)


def _router(logits, bias):
    t, ne = logits.shape
    tr = _tile(t, 512)
    return pl.pallas_call(
        _router_kernel,
        out_shape=(jax.ShapeDtypeStruct((t, TOP_K), I32), jax.ShapeDtypeStruct((t, TOP_K), F32),
                   jax.ShapeDtypeStruct((t, TOP_K), I32), jax.ShapeDtypeStruct((1, ne), F32)),
        grid=(t // tr,),
        in_specs=[pl.BlockSpec((tr, ne), lambda i: (i, 0)), pl.BlockSpec((1, ne), lambda i: (0, 0))],
        out_specs=(pl.BlockSpec((tr, TOP_K), lambda i: (i, 0)), pl.BlockSpec((tr, TOP_K), lambda i: (i, 0)),
                   pl.BlockSpec((tr, TOP_K), lambda i: (i, 0)), pl.BlockSpec((1, ne), lambda i: (0, 0))),
        scratch_shapes=[pltpu.VMEM((1, ne), F32)],
        compiler_params=_cparams(("arbitrary",)),
        name="router",
    )(logits, bias)


def _dest_kernel(idx_ref, rank_ref, pstart_ref, dest_ref):
    idx = idx_ref[...]
    lane = lax.broadcasted_iota(I32, (idx.shape[0], N_EXPERTS), 1)
    cols = []
    for k in range(TOP_K):
        start = jnp.sum(jnp.where(lane == idx[:, k:k + 1], pstart_ref[...], 0.0), axis=-1, keepdims=True)
        cols.append(start.astype(I32) + rank_ref[:, k:k + 1])
    dest_ref[...] = jnp.concatenate(cols, axis=1)


def _dest(idx, rank, pstart):
    t = idx.shape[0]
    tr = _tile(t, 512)
    return pl.pallas_call(
        _dest_kernel,
        out_shape=jax.ShapeDtypeStruct((t, TOP_K), I32),
        grid=(t // tr,),
        in_specs=[pl.BlockSpec((tr, TOP_K), lambda i: (i, 0)), pl.BlockSpec((tr, TOP_K), lambda i: (i, 0)),
                  pl.BlockSpec((1, N_EXPERTS), lambda i: (0, 0))],
        out_specs=pl.BlockSpec((tr, TOP_K), lambda i: (i, 0)),
        compiler_params=_cparams(("parallel",)),
        name="dest",
    )(idx, rank, pstart)


def _dispatch_kernel(dest_hbm, h2_ref, xs_in, xs_out, dest_smem, sem_idx, sem_rows):
    del xs_in
    i = pl.program_id(0)
    td = h2_ref.shape[0]
    n = td * TOP_K
    cp = pltpu.make_async_copy(dest_hbm.at[pl.ds(i * n, n)], dest_smem, sem_idx)
    cp.start()
    cp.wait()

    def row_copy(t, d):
        return pltpu.make_async_copy(h2_ref.at[pl.ds(t, 1), :], xs_out.at[pl.ds(d, 1), :], sem_rows)

    def issue(t, carry):
        for k in range(TOP_K):
            row_copy(t, dest_smem[t * TOP_K + k]).start(priority=k % 2)
        return carry

    lax.fori_loop(0, td, issue, 0)

    def drain(t, carry):
        for k in range(TOP_K):
            row_copy(0, 0).wait()
        return carry

    lax.fori_loop(0, td, drain, 0)


def _dispatch(dest_flat, h2p, xs_init):
    t, half = h2p.shape
    td = _tile(t, 256)
    return pl.pallas_call(
        _dispatch_kernel,
        out_shape=jax.ShapeDtypeStruct(xs_init.shape, U32),
        grid=(t // td,),
        in_specs=[pl.BlockSpec(memory_space=pl.ANY),
                  pl.BlockSpec((td, half), lambda i: (i, 0)),
                  pl.BlockSpec(memory_space=pl.ANY)],
        out_specs=pl.BlockSpec(memory_space=pl.ANY),
        scratch_shapes=[pltpu.SMEM((td * TOP_K,), I32), pltpu.SemaphoreType.DMA, pltpu.SemaphoreType.DMA],
        input_output_aliases={2: 0},
        compiler_params=pltpu.CompilerParams(dimension_semantics=("arbitrary",), vmem_limit_bytes=VMEM_LIMIT,
                                             has_side_effects=True),
        name="dispatch",
    )(dest_flat, h2p, xs_init)


def _experts_kernel(be_ref, nused_ref, xs_ref, wg_ref, wu_ref, wd_ref, ys_ref, wgb_ref, wub_ref, wdb_ref):
    blk = pl.program_id(0)
    used = blk < nused_ref[0]

    @pl.when(used & ((blk == 0) | (be_ref[blk] != be_ref[jnp.maximum(blk - 1, 0)])))
    def _():
        wgb_ref[...] = wg_ref[0].astype(BF16)
        wub_ref[...] = wu_ref[0].astype(BF16)
        wdb_ref[...] = wd_ref[0].astype(BF16)

    @pl.when(used)
    def _():
        x = _unpack_rows(xs_ref[...])
        g = _dot(x, wgb_ref[...])
        u = _dot(x, wub_ref[...])
        a = (_silu(g) * u).astype(BF16)
        ys_ref[...] = _dot(a, wdb_ref[...])

    @pl.when(blk >= nused_ref[0])
    def _():
        ys_ref[...] = jnp.zeros(ys_ref.shape, F32)


def _experts(blk_expert, nused, xs, wg, wu, wd):
    p, half = xs.shape
    d = 2 * half
    de = wg.shape[2]
    nblk = p // MOE_BLOCK
    grid_spec = pltpu.PrefetchScalarGridSpec(
        num_scalar_prefetch=2,
        grid=(nblk,),
        in_specs=[
            pl.BlockSpec((MOE_BLOCK, half), lambda i, be, nu: (i, 0)),
            pl.BlockSpec((1, d, de), lambda i, be, nu: (be[i], 0, 0)),
            pl.BlockSpec((1, d, de), lambda i, be, nu: (be[i], 0, 0)),
            pl.BlockSpec((1, de, d), lambda i, be, nu: (be[i], 0, 0)),
        ],
        out_specs=pl.BlockSpec((MOE_BLOCK, d), lambda i, be, nu: (i, 0)),
        scratch_shapes=[pltpu.VMEM((d, de), BF16), pltpu.VMEM((d, de), BF16), pltpu.VMEM((de, d), BF16)],
    )
    return pl.pallas_call(
        _experts_kernel,
        out_shape=jax.ShapeDtypeStruct((p, d), F32),
        grid_spec=grid_spec,
        compiler_params=_cparams(("arbitrary",)),
        name="experts",
    )(blk_expert, nused, xs, wg, wu, wd)


def _combine_kernel(dest_hbm, w_ref, ys_hbm, h2_ref, x1_ref, mod_ref, wgs_ref, wus_ref, wds_ref, o_ref,
                    dest_smem, buf, sem_idx, sem_rows):
    i = pl.program_id(0)
    tc = h2_ref.shape[0]
    n = tc * TOP_K
    cp = pltpu.make_async_copy(dest_hbm.at[pl.ds(i * n, n)], dest_smem, sem_idx)
    cp.start()
    cp.wait()

    def row_copy(t, k, d):
        return pltpu.make_async_copy(ys_hbm.at[pl.ds(d, 1), :], buf.at[k, pl.ds(t, 1), :], sem_rows)

    def issue(t, carry):
        for k in range(TOP_K):
            row_copy(t, k, dest_smem[t * TOP_K + k]).start(priority=k % 2)
        return carry

    lax.fori_loop(0, tc, issue, 0)

    x = _unpack_rows(h2_ref[...])
    a = (_silu(_dot(x, wgs_ref[...])) * _dot(x, wus_ref[...])).astype(BF16)
    y = _dot(a, wds_ref[...])

    def drain(t, carry):
        for k in range(TOP_K):
            row_copy(0, k, 0).wait()
        return carry

    lax.fori_loop(0, tc, drain, 0)

    for k in range(TOP_K):
        y = y + buf[k] * w_ref[:, k:k + 1]
    o_ref[...] = x1_ref[...] + mod_ref[0, 5:6, :] * y


def _combine(dest_flat, w, ys, h2p, x1, mod, wgs, wus, wds, seq):
    t, d = x1.shape
    half = d // 2
    tc = _tile(seq, 128)
    de = wgs.shape[1]
    per_batch = seq // tc
    const = lambda i: (0, 0)
    return pl.pallas_call(
        _combine_kernel,
        out_shape=jax.ShapeDtypeStruct((t, d), F32),
        grid=(t // tc,),
        in_specs=[
            pl.BlockSpec(memory_space=pl.ANY),
            pl.BlockSpec((tc, TOP_K), lambda i: (i, 0)),
            pl.BlockSpec(memory_space=pl.ANY),
            pl.BlockSpec((tc, half), lambda i: (i, 0)),
            pl.BlockSpec((tc, d), lambda i: (i, 0)),
            pl.BlockSpec((1, 6, d), lambda i: (i // per_batch, 0, 0)),
            pl.BlockSpec((d, de), const), pl.BlockSpec((d, de), const), pl.BlockSpec((de, d), const),
        ],
        out_specs=pl.BlockSpec((tc, d), lambda i: (i, 0)),
        scratch_shapes=[pltpu.SMEM((tc * TOP_K,), I32), pltpu.VMEM((TOP_K, tc, d), F32),
                        pltpu.SemaphoreType.DMA, pltpu.SemaphoreType.DMA],
        compiler_params=_cparams(("arbitrary",)),
        name="combine",
    )(dest_flat, w, ys, h2p, x1, mod, wgs, wus, wds)


def _rope_tables(n):
    rows = n // GRID_W
    row = jnp.broadcast_to(jnp.arange(rows)[:, None], (rows, GRID_W)).reshape(-1)
    col = jnp.broadcast_to(jnp.arange(GRID_W)[None, :], (rows, GRID_W)).reshape(-1)
    freqs = ROPE_BASE ** (-jnp.arange(ROPE_FREQS, dtype=F32) / ROPE_FREQS)
    pos = jnp.stack([row, col], axis=-1).astype(F32)
    ang = pos[:, :, None] * freqs
    cos, sin = jnp.cos(ang), jnp.sin(ang)
    cos64 = jnp.concatenate([cos[:, 0], cos[:, 0], cos[:, 1], cos[:, 1]], axis=-1)
    sin64 = jnp.concatenate([-sin[:, 0], sin[:, 0], -sin[:, 1], sin[:, 1]], axis=-1)
    reps = MXU_N // HEAD_DIM
    return jnp.tile(cos64, (1, reps)), jnp.tile(sin64, (1, reps))


def _gain_rows(qn_a, kn_a, qn_b, kn_b):
    reps = MXU_N // HEAD_DIM
    ones = jnp.ones((LANES,), F32)
    kb = jnp.tile(kn_b, reps)
    qa = jnp.tile(qn_a, reps) * ATTN_SCALE
    qb = jnp.tile(qn_b, reps) * (ATTN_SCALE * LOG2E)
    mixed = jnp.concatenate([jnp.tile(kn_a, LANES // HEAD_DIM), ones])
    lat_g = jnp.stack([kb] * 4 + [qa] * 4 + [qb] * 4 + [mixed])[:, None, :]
    ctx_g = jnp.stack([kb] * 4 + [mixed])[:, None, :]
    return lat_g, ctx_g


def kernel(x, c, ctx, c_ctx, w_ada, b_ada, norm1_g, norm2_g, w_in, qnorm_a, knorm_a, sink_a, qnorm_b, knorm_b,
           lam_q1, lam_k1, lam_q2, lam_k2, subln_g, w_pa, w_pb, w_o, w_router, router_bias, w_gate_e, w_up_e,
           w_down_e, w_gate_s, w_up_s, w_down_s):
    b, s, d = x.shape
    nctx = ctx.shape[1]
    assert w_ada.shape[0] == 1 and d == 1024 and s % BLOCK == 0 and s % GRID_W == 0
    t = b * s

    cc = jnp.concatenate([c, c_ctx[None, :], jnp.zeros((16 - b - 1, d), F32)], axis=0)
    mod_all = _ada(cc, w_ada[0], b_ada[0])
    mod = mod_all[:b].reshape(b, 6, d)
    mod_c = jnp.broadcast_to(mod_all[b].reshape(1, 6, d), (b, 6, d))

    w = w_in[0]
    ka_w, va_w = w[:, 0:128], w[:, 128:256]
    kb_w, vb_w = w[:, 256:1280], w[:, 1280:2304]
    qa_w, qb_w = w[:, 2304:3328], w[:, 3328:4352]
    gate_w = w[:, 4352:6400].astype(BF16)
    w_lat = jnp.concatenate([kb_w, qa_w, qb_w, ka_w, va_w, vb_w], axis=1).astype(BF16)
    w_ctx = jnp.concatenate([kb_w, ka_w, va_w, vb_w], axis=1).astype(BF16)

    lat_g, ctx_g = _gain_rows(qnorm_a[0], knorm_a[0], qnorm_b[0], knorm_b[0])
    hid = jnp.arange(MXU_N) // HEAD_DIM
    seg = jnp.where(hid[:, None] == hid[None, :], 1.0 / HEAD_DIM, 0.0).astype(BF16)
    cos_t, sin_t = _rope_tables(s)
    n1 = norm1_g[0].reshape(1, d)
    n2 = norm2_g[0].reshape(1, d)

    lat, vt = _proj(x, mod, n1, w_lat, lat_g, seg, cos_t, sin_t, nqk=LAT_NQK, use_rope=True)
    ctxp, vtx = _proj(ctx, mod_c, n1, w_ctx, ctx_g, seg, cos_t[:nctx], sin_t[:nctx], nqk=CTX_NQK, use_rope=False)

    oa = _attn_a(sink_a[0], lat, ctxp)
    lamv = jnp.stack([lam_q1[0], lam_k1[0], lam_q2[0], lam_k2[0]])
    bound = (HEAD_DIM * ATTN_SCALE * LOG2E * 1.01) * jnp.max(jnp.abs(qnorm_b[0])) * jnp.max(jnp.abs(knorm_b[0]))
    ob = _attn_b(bound.reshape(1), lamv, subln_g[0].reshape(LANES, 1), lat, vt, ctxp, vtx)

    x1, h2p, logits = _merge(x, mod, n1, n2, oa, ob, gate_w, w_pa[0].astype(BF16), w_pb[0].astype(BF16),
                             w_o[0].astype(BF16), w_router[0].astype(BF16))
    x1 = x1.reshape(t, d)
    h2p = h2p.reshape(t, d // 2)
    logits = logits.reshape(t, N_EXPERTS)

    idx, wts, rank, counts = _router(logits, router_bias[0].reshape(1, N_EXPERTS))

    cnt = counts[0].astype(I32)
    padded = (cnt + MOE_BLOCK - 1) // MOE_BLOCK * MOE_BLOCK
    pend = jnp.cumsum(padded)
    pstart = pend - padded
    nblk = -(-(t * TOP_K) // MOE_BLOCK) + N_EXPERTS
    blk_row0 = jnp.arange(nblk, dtype=I32) * MOE_BLOCK
    blk_expert = jnp.minimum(jnp.sum((pend[None, :] <= blk_row0[:, None]).astype(I32), axis=1), N_EXPERTS - 1)
    nused = (pend[-1:] // MOE_BLOCK).astype(I32)

    dest = _dest(idx, rank, pstart.astype(F32).reshape(1, N_EXPERTS)).reshape(t * TOP_K)
    xs = _dispatch(dest, h2p, jnp.zeros((nblk * MOE_BLOCK, d // 2), U32))
    ys = _experts(blk_expert, nused, xs, w_gate_e[0], w_up_e[0], w_down_e[0])
    out = _combine(dest, wts, ys, h2p, x1, mod, w_gate_s[0].astype(BF16), w_up_s[0].astype(BF16),
                   w_down_s[0].astype(BF16), s)
    return out.reshape(b, s, d)
```

```python
import functools
import math

import jax
import jax.numpy as jnp
from jax import lax
from jax.experimental import pallas as pl
from jax.experimental.pallas import tpu as pltpu

F32 = jnp.float32
BF16 = jnp.bfloat16
I32 = jnp.int32
U32 = jnp.uint32

HEAD_DIM = 64
GRID_W = 64
ROPE_FREQS = HEAD_DIM // 4
ROPE_BASE = 10000.0
EPS = 1e-6
ATTN_SCALE = HEAD_DIM ** -0.5
BLOCK = 128
A_Q_HEADS = 16
A_KV_HEADS = 2
A_GROUP = A_Q_HEADS // A_KV_HEADS
B_HEADS = 8
N_EXPERTS = 256
TOP_K = 8
N_GROUPS = 8
TOPK_GROUPS = 4
GROUP_SIZE = N_EXPERTS // N_GROUPS
ROUTED_SCALE = 2.5
MOE_BLOCK = 256
LAM_INIT = 0.8 - 0.6 * math.exp(-0.3 * 0)

LANES = 128
MXU_N = 256
VMEM_LIMIT = 56 * 1024 * 1024
NEG_BIG = -1e30
LOG2E = math.log2(math.e)
MAX_FIXED_SHIFT = 60.0

LAT_KB, LAT_QA, LAT_QB, LAT_KA, LAT_VA = 0, 8, 16, 24, 25
LAT_NQK = 13
CTX_KB, CTX_KA, CTX_VA = 0, 8, 9
CTX_NQK = 5


def _tile(n, pref):
    return pref if n % pref == 0 else n


def _cparams(sem):
    return pltpu.CompilerParams(dimension_semantics=sem, vmem_limit_bytes=VMEM_LIMIT)


def _nt_dot(a, b):
    return lax.dot_general(a, b, (((1,), (1,)), ((), ())), preferred_element_type=F32)


def _dot(a, b):
    return jnp.dot(a, b, preferred_element_type=F32)


def _silu(x):
    return x * (1.0 / (1.0 + jnp.exp(-x)))


def _sigmoid(x):
    return 1.0 / (1.0 + jnp.exp(-x))


def _modulated_norm(xf, g_row, shift_row, scale_row):
    ms = jnp.mean(xf * xf, axis=-1, keepdims=True)
    y = xf * lax.rsqrt(ms + EPS) * g_row
    return y * (1.0 + scale_row) + shift_row


def _ada_kernel(c_ref, w_ref, b_ref, o_ref):
    o_ref[...] = _dot(_silu(c_ref[...]).astype(BF16), w_ref[...].astype(BF16)) + b_ref[...]


def _ada(cc, w_ada, b_ada):
    rows, d = cc.shape
    n = w_ada.shape[1]
    tn = _tile(n, 512)
    return pl.pallas_call(
        _ada_kernel,
        out_shape=jax.ShapeDtypeStruct((rows, n), F32),
        grid=(n // tn,),
        in_specs=[pl.BlockSpec((rows, d), lambda j: (0, 0)),
                  pl.BlockSpec((d, tn), lambda j: (0, j)),
                  pl.BlockSpec((1, tn), lambda j: (0, j))],
        out_specs=pl.BlockSpec((rows, tn), lambda j: (0, j)),
        compiler_params=_cparams(("parallel",)),
        name="ada",
    )(cc, w_ada, b_ada.reshape(1, n))


def _proj_kernel(x_ref, mod_ref, n1_ref, w_ref, gain_ref, seg_ref, cos_ref, sin_ref, o_ref, vt_ref,
                 *, nqk, use_rope):
    nj = w_ref.shape[1] // MXU_N
    h = _modulated_norm(x_ref[0], n1_ref[...], mod_ref[0, 0:1, :], mod_ref[0, 1:2, :]).astype(BF16)
    lane = lax.broadcasted_iota(I32, (1, MXU_N), 1)
    first = (lane % (2 * ROPE_FREQS)) < ROPE_FREQS

    def matmul(j):
        return _dot(h, w_ref[:, j * MXU_N:(j + 1) * MXU_N])

    def finish(j, acc):
        if j >= nqk:
            vt_ref[0, (j - nqk) * MXU_N:(j - nqk + 1) * MXU_N, :] = acc.T.astype(BF16)
            return
        ms = _dot((acc * acc).astype(BF16), seg_ref[...])
        y = acc * lax.rsqrt(ms + EPS) * gain_ref[j]
        if use_rope:
            rot = jnp.where(first, pltpu.roll(y, MXU_N - ROPE_FREQS, 1), pltpu.roll(y, ROPE_FREQS, 1))
            y = y * cos_ref[...] + rot * sin_ref[...]
        if j == nqk - 1:
            y = jnp.where(lane < LANES, y, acc)
        o_ref[0, :, j * MXU_N:(j + 1) * MXU_N] = y.astype(BF16)

    acc = matmul(0)
    for j in range(nj):
        nxt = matmul(j + 1) if j + 1 < nj else None
        finish(j, acc)
        acc = nxt


def _proj(x, mod, n1, w, gains, seg, cos_t, sin_t, *, nqk, use_rope):
    b, s, d = x.shape
    ncols = w.shape[1]
    tm = _tile(s, 512)
    nj = ncols // MXU_N
    const2 = lambda bi, si: (0, 0)
    return pl.pallas_call(
        functools.partial(_proj_kernel, nqk=nqk, use_rope=use_rope),
        out_shape=(jax.ShapeDtypeStruct((b, s, nqk * MXU_N), BF16),
                   jax.ShapeDtypeStruct((b, (nj - nqk) * MXU_N, s), BF16)),
        grid=(b, s // tm),
        in_specs=[
            pl.BlockSpec((1, tm, d), lambda bi, si: (bi, si, 0)),
            pl.BlockSpec((1, 6, d), lambda bi, si: (bi, 0, 0)),
            pl.BlockSpec((1, d), const2),
            pl.BlockSpec((d, ncols), const2),
            pl.BlockSpec((nqk, 1, MXU_N), lambda bi, si: (0, 0, 0)),
            pl.BlockSpec((MXU_N, MXU_N), const2),
            pl.BlockSpec((tm, MXU_N), lambda bi, si: (si, 0)),
            pl.BlockSpec((tm, MXU_N), lambda bi, si: (si, 0)),
        ],
        out_specs=(pl.BlockSpec((1, tm, nqk * MXU_N), lambda bi, si: (bi, si, 0)),
                   pl.BlockSpec((1, (nj - nqk) * MXU_N, tm), lambda bi, si: (bi, 0, si))),
        compiler_params=_cparams(("parallel", "parallel")),
        name="proj_rope" if use_rope else "proj_ctx",
    )(x, mod, n1, w, gains, seg, cos_t, sin_t)


def _attn_a_kernel(sink_ref, q_ref, kp_ref, kc_ref, kn_ref, vp_ref, vc_ref, vn_ref, kx_ref, vx_ref, o_ref):
    i = pl.program_id(1)
    nb = pl.num_programs(1)
    nctx = kx_ref.shape[1]
    span = 3 * BLOCK + nctx
    kall = jnp.concatenate([kp_ref[0], kc_ref[0], kn_ref[0], kx_ref[0]], axis=0)
    vall = jnp.concatenate([vp_ref[0], vc_ref[0], vn_ref[0], vx_ref[0]], axis=0)

    r = lax.broadcasted_iota(I32, (BLOCK, span), 0)
    c = lax.broadcasted_iota(I32, (BLOCK, span), 1)
    prev_ok = (c < BLOCK) & (c >= r) & (i > 0)
    cur_ok = (c >= BLOCK) & (c < 2 * BLOCK)
    next_ok = (c >= 2 * BLOCK) & (c < 3 * BLOCK) & (c - 2 * BLOCK <= r) & (i < nb - 1)
    valid = prev_ok | cur_ok | next_ok | (c >= 3 * BLOCK)
    bias = jnp.where(valid, 0.0, NEG_BIG)

    lane = lax.broadcasted_iota(I32, (BLOCK, LANES), 1)
    for kv in range(A_KV_HEADS):
        in_half = (lane >= kv * HEAD_DIM) & (lane < (kv + 1) * HEAD_DIM)
        qs = []
        for g in range(A_GROUP):
            hh = kv * A_GROUP + g
            blk = q_ref[0, :, (hh // 2) * LANES:(hh // 2 + 1) * LANES].astype(F32)
            if hh % 2 != kv:
                blk = pltpu.roll(blk, HEAD_DIM, 1)
            qs.append(jnp.where(in_half, blk, 0.0).astype(BF16))
        s_all = _nt_dot(jnp.concatenate(qs, axis=0), kall)
        ps, ls = [], []
        for g in range(A_GROUP):
            sink = sink_ref[kv, g]
            s = s_all[g * BLOCK:(g + 1) * BLOCK] + bias
            m = jnp.maximum(jnp.max(s, axis=-1, keepdims=True), sink)
            p = jnp.exp(s - m)
            ls.append(jnp.sum(p, axis=-1, keepdims=True) + jnp.exp(sink - m))
            ps.append(p.astype(BF16))
        o_all = _dot(jnp.concatenate(ps, axis=0), vall)
        os_ = [o_all[g * BLOCK:(g + 1) * BLOCK] / ls[g] for g in range(A_GROUP)]
        for pair in range(A_GROUP // 2):
            o0, o1 = os_[2 * pair], os_[2 * pair + 1]
            if kv == 0:
                o1 = pltpu.roll(o1, HEAD_DIM, 1)
            else:
                o0 = pltpu.roll(o0, HEAD_DIM, 1)
            col = (kv * (A_GROUP // 2) + pair) * LANES
            o_ref[0, :, col:col + LANES] = jnp.where(lane < HEAD_DIM, o0, o1).astype(BF16)


def _attn_a(sink, lat, ctxp):
    b, s, _ = lat.shape
    nctx = ctxp.shape[1]
    nb = s // BLOCK
    width = A_Q_HEADS * HEAD_DIM

    def kspec(col, shift):
        return pl.BlockSpec((1, BLOCK, LANES),
                            lambda bi, i: (bi, jnp.clip(i + shift, 0, nb - 1), col))

    return pl.pallas_call(
        _attn_a_kernel,
        out_shape=jax.ShapeDtypeStruct((b, s, width), BF16),
        grid=(b, nb),
        in_specs=[
            pl.BlockSpec(memory_space=pltpu.SMEM),
            pl.BlockSpec((1, BLOCK, width), lambda bi, i: (bi, i, LAT_QA * LANES // width)),
            kspec(LAT_KA, -1), kspec(LAT_KA, 0), kspec(LAT_KA, 1),
            kspec(LAT_VA, -1), kspec(LAT_VA, 0), kspec(LAT_VA, 1),
            pl.BlockSpec((1, nctx, LANES), lambda bi, i: (bi, 0, CTX_KA)),
            pl.BlockSpec((1, nctx, LANES), lambda bi, i: (bi, 0, CTX_VA)),
        ],
        out_specs=pl.BlockSpec((1, BLOCK, width), lambda bi, i: (bi, i, 0)),
        compiler_params=_cparams(("parallel", "parallel")),
        name="attn_a",
    )(sink, lat, lat, lat, lat, lat, lat, lat, ctxp, ctxp)


def _attn_b_kernel(bound_ref, lamv_ref, sub_ref, q_ref, k_ref, vt_ref, kx_ref, vtx_ref, o_ref, acc_ref, l_ref,
                   s0_ref, s1_ref, sx_ref, *, tk):
    tq = q_ref.shape[1]
    s_len = k_ref.shape[1]
    qt = q_ref[0].astype(F32).T
    row = lax.broadcasted_iota(I32, (LANES, tq), 0)
    qts = [jnp.where(row < HEAD_DIM, qt, 0.0).astype(BF16), jnp.where(row >= HEAD_DIM, qt, 0.0).astype(BF16)]
    acc_ref[...] = jnp.zeros(acc_ref.shape, F32)
    n = s_len // tk
    bound = bound_ref[0]

    def kchunk(i):
        return k_ref[0, pl.ds(pl.multiple_of(i * tk, tk), tk), :]

    def vchunk(i):
        return vt_ref[0, :, pl.ds(pl.multiple_of(i * tk, tk), tk)]

    def run(fixed):
        def scores(j, kc, s_ref):
            s = _dot(kc, qts[j])
            s_ref[j] = s
            return jnp.zeros((1, tq), F32) if fixed else jnp.max(s, axis=0, keepdims=True)

        def accumulate(j, s_ref, cm, vtc, stat):
            m_old, l_old = stat
            if fixed:
                p = jnp.exp2(s_ref[j] - bound)
                acc_ref[j] += _dot(vtc, p.astype(BF16))
                return m_old, l_old + jnp.sum(p, axis=0, keepdims=True)
            m_new = jnp.maximum(m_old, cm)
            alpha = jnp.exp2(m_old - m_new)
            p = jnp.exp2(s_ref[j] - m_new)
            acc_ref[j] = alpha * acc_ref[j] + _dot(vtc, p.astype(BF16))
            return m_new, alpha * l_old + jnp.sum(p, axis=0, keepdims=True)

        def half_step(k_next, s_next, s_cur, cms, vtc, stats):
            new_cms, new_stats = [], []
            for j in range(2):
                new_cms.append(scores(j, k_next, s_next))
                new_stats.append(accumulate(j, s_cur, cms[j], vtc, stats[j]))
            return tuple(new_cms), tuple(new_stats)

        m0 = jnp.zeros((1, tq), F32) if fixed else jnp.full((1, tq), NEG_BIG, F32)
        stats = ((m0, jnp.zeros((1, tq), F32)),) * 2
        cmx = tuple(scores(j, kx_ref[0], sx_ref) for j in range(2))
        cms, stats = half_step(kchunk(0), s0_ref, sx_ref, cmx, vtx_ref[0], stats)

        def body(pair, carry):
            cms, stats = carry
            i = 2 * pair
            cms, stats = half_step(kchunk(i + 1), s1_ref, s0_ref, cms, vchunk(i), stats)
            return half_step(kchunk(i + 2), s0_ref, s1_ref, cms, vchunk(i + 1), stats)

        cms, stats = lax.fori_loop(0, n // 2 - 1, body, (cms, stats))
        cms, stats = half_step(kchunk(n - 1), s1_ref, s0_ref, cms, vchunk(n - 2), stats)
        for j in range(2):
            l_ref[j] = accumulate(j, s1_ref, cms[j], vchunk(n - 1), stats[j])[1]

    pl.when(bound <= MAX_FIXED_SHIFT)(lambda: run(True))
    pl.when(bound > MAX_FIXED_SHIFT)(lambda: run(False))


    lv = lamv_ref[...]
    lam = (jnp.exp(jnp.sum(lv[0:1] * lv[1:2], axis=-1, keepdims=True))
           - jnp.exp(jnp.sum(lv[2:3] * lv[3:4], axis=-1, keepdims=True)) + LAM_INIT)
    o = acc_ref[0] * (1.0 / l_ref[0]) - acc_ref[1] * (lam / l_ref[1])
    ms = jnp.mean(o * o, axis=0, keepdims=True)
    o = o * lax.rsqrt(ms + EPS) * (sub_ref[...] * (1.0 - LAM_INIT))
    o_ref[0] = o.T.astype(BF16)


def _attn_b(bound, lamv, subln_col, lat, vt, ctxp, vtx):
    b, s, _ = lat.shape
    nctx = ctxp.shape[1]
    tq = _tile(s, 512)
    tk = min(512, s // 2)
    assert s % (2 * tk) == 0
    return pl.pallas_call(
        functools.partial(_attn_b_kernel, tk=tk),
        out_shape=jax.ShapeDtypeStruct((b, s, B_HEADS * LANES), BF16),
        grid=(b, B_HEADS, s // tq),
        in_specs=[
            pl.BlockSpec(memory_space=pltpu.SMEM),
            pl.BlockSpec((4, HEAD_DIM), lambda bi, h, qi: (0, 0)),
            pl.BlockSpec((LANES, 1), lambda bi, h, qi: (0, 0)),
            pl.BlockSpec((1, tq, LANES), lambda bi, h, qi: (bi, qi, LAT_QB + h)),
            pl.BlockSpec((1, s, LANES), lambda bi, h, qi: (bi, 0, LAT_KB + h)),
            pl.BlockSpec((1, LANES, s), lambda bi, h, qi: (bi, h, 0)),
            pl.BlockSpec((1, nctx, LANES), lambda bi, h, qi: (bi, 0, CTX_KB + h)),
            pl.BlockSpec((1, LANES, nctx), lambda bi, h, qi: (bi, h, 0)),
        ],
        out_specs=pl.BlockSpec((1, tq, LANES), lambda bi, h, qi: (bi, qi, h)),
        scratch_shapes=[pltpu.VMEM((2, LANES, tq), F32), pltpu.VMEM((2, 1, tq), F32),
                        pltpu.VMEM((2, tk, tq), F32), pltpu.VMEM((2, tk, tq), F32),
                        pltpu.VMEM((2, nctx, tq), F32)],
        compiler_params=_cparams(("parallel", "parallel", "arbitrary")),
        name="attn_b",
    )(bound, lamv, subln_col, lat, lat, vt, ctxp, vtx)


def _pack_rows(h):
    n = h.shape[1] // 2
    bits = pltpu.bitcast(h.astype(BF16).astype(F32), U32)
    return (bits[:, :n] >> 16) | (bits[:, n:] & jnp.uint32(0xFFFF0000))


def _unpack_rows(w):
    lo = pltpu.bitcast(w << 16, F32)
    hi = pltpu.bitcast(w & jnp.uint32(0xFFFF0000), F32)
    return jnp.concatenate([lo, hi], axis=1).astype(BF16)


def _merge_kernel(x_ref, mod_ref, n1_ref, n2_ref, oa_ref, ob_ref, wg_ref, wpa_ref, wpb_ref, wo_ref, wr_ref,
                  x1_ref, h2_ref, lg_ref):
    d = x_ref.shape[2]
    xf = x_ref[0]
    h = _modulated_norm(xf, n1_ref[...], mod_ref[0, 0:1, :], mod_ref[0, 1:2, :]).astype(BF16)
    gates = _sigmoid(_dot(h, wg_ref[...]))
    ya = _dot(oa_ref[0], wpa_ref[...])
    yb = _dot(ob_ref[0], wpb_ref[...])
    z = gates[:, :d] * ya + gates[:, d:] * yb
    x1 = xf + mod_ref[0, 2:3, :] * _dot(z.astype(BF16), wo_ref[...])
    x1_ref[0] = x1
    h2 = _modulated_norm(x1, n2_ref[...], mod_ref[0, 3:4, :], mod_ref[0, 4:5, :])
    lg_ref[0] = _dot(h2.astype(BF16), wr_ref[...])
    h2_ref[0] = _pack_rows(h2)


def _merge(x, mod, n1, n2, oa, ob, wg, wpa, wpb, wo, wr):
    b, s, d = x.shape
    tm = _tile(s, 256)
    ne = wr.shape[1]
    const = lambda bi, si: (0, 0)
    row = lambda bi, si: (bi, si, 0)
    return pl.pallas_call(
        _merge_kernel,
        out_shape=(jax.ShapeDtypeStruct((b, s, d), F32),
                   jax.ShapeDtypeStruct((b, s, d // 2), U32),
                   jax.ShapeDtypeStruct((b, s, ne), F32)),
        grid=(b, s // tm),
        in_specs=[
            pl.BlockSpec((1, tm, d), row),
            pl.BlockSpec((1, 6, d), lambda bi, si: (bi, 0, 0)),
            pl.BlockSpec((1, d), const), pl.BlockSpec((1, d), const),
            pl.BlockSpec((1, tm, d), row), pl.BlockSpec((1, tm, d), row),
            pl.BlockSpec((d, 2 * d), const), pl.BlockSpec((d, d), const),
            pl.BlockSpec((d, d), const), pl.BlockSpec((d, d), const),
            pl.BlockSpec((d, ne), const),
        ],
        out_specs=(pl.BlockSpec((1, tm, d), row), pl.BlockSpec((1, tm, d // 2), row),
                   pl.BlockSpec((1, tm, ne), row)),
        compiler_params=_cparams(("parallel", "parallel")),
        name="merge",
    )(x, mod, n1, n2, oa, ob, wg, wpa, wpb, wo, wr)


def _router_kernel(lg_ref, bias_ref, idx_ref, w_ref, rank_ref, cnt_ref, run_ref):
    i = pl.program_id(0)
    tr = lg_ref.shape[0]

    @pl.when(i == 0)
    def _():
        run_ref[...] = jnp.zeros(run_ref.shape, F32)

    scores = _sigmoid(lg_ref[...])
    biased = scores + bias_ref[...]
    lane = lax.broadcasted_iota(I32, scores.shape, 1)
    neg_inf = -jnp.inf

    def first_argmax(vals):
        m = jnp.max(vals, axis=-1, keepdims=True)
        idx = jnp.min(jnp.where(vals == m, lane, N_EXPERTS), axis=-1, keepdims=True)
        return m, idx

    gscore = []
    for g in range(N_GROUPS):
        in_g = (lane >= g * GROUP_SIZE) & (lane < (g + 1) * GROUP_SIZE)
        vals = jnp.where(in_g, biased, neg_inf)
        m1, i1 = first_argmax(vals)
        m2 = jnp.max(jnp.where(lane == i1, neg_inf, vals), axis=-1, keepdims=True)
        gscore.append(m1 + m2)
    emask = jnp.zeros(scores.shape, jnp.bool_)
    for g in range(N_GROUPS):
        beaten = jnp.zeros((tr, 1), F32)
        for g2 in range(N_GROUPS):
            if g2 == g:
                continue
            wins = (gscore[g2] > gscore[g]) | ((gscore[g2] == gscore[g]) & (g2 < g))
            beaten = beaten + jnp.where(wins, 1.0, 0.0)
        in_g = (lane >= g * GROUP_SIZE) & (lane < (g + 1) * GROUP_SIZE)
        emask = emask | (in_g & (beaten < TOPK_GROUPS))

    masked = jnp.where(emask, biased, neg_inf)
    sel = jnp.zeros(scores.shape, F32)
    idxs, ws = [], []
    for _ in range(TOP_K):
        _, ik = first_argmax(masked)
        hit = lane == ik
        idxs.append(ik)
        ws.append(jnp.sum(jnp.where(hit, scores, 0.0), axis=-1, keepdims=True))
        sel = jnp.where(hit, 1.0, sel)
        masked = jnp.where(hit, neg_inf, masked)
    wsum = ws[0]
    for k in range(1, TOP_K):
        wsum = wsum + ws[k]

    rr = lax.broadcasted_iota(I32, (tr, tr), 0)
    cc = lax.broadcasted_iota(I32, (tr, tr), 1)
    lower = jnp.where(cc < rr, 1.0, 0.0).astype(BF16)
    before = _dot(lower, sel.astype(BF16)) + run_ref[...]
    ranks = [jnp.sum(jnp.where(lane == idxs[k], before, 0.0), axis=-1, keepdims=True) for k in range(TOP_K)]
    run_ref[...] = run_ref[...] + jnp.sum(sel, axis=0, keepdims=True)
    cnt_ref[...] = run_ref[...]

    idx_ref[...] = jnp.concatenate(idxs, axis=1)
    w_ref[...] = jnp.concatenate([w / wsum * ROUTED_SCALE for w in ws], axis=1)
    rank_ref[...] = jnp.concatenate(ranks, axis=1).astype(I32)


def _router(logits, bias):
    t, ne = logits.shape
    tr = _tile(t, 512)
    return pl.pallas_call(
        _router_kernel,
        out_shape=(jax.ShapeDtypeStruct((t, TOP_K), I32), jax.ShapeDtypeStruct((t, TOP_K), F32),
                   jax.ShapeDtypeStruct((t, TOP_K), I32), jax.ShapeDtypeStruct((1, ne), F32)),
        grid=(t // tr,),
        in_specs=[pl.BlockSpec((tr, ne), lambda i: (i, 0)), pl.BlockSpec((1, ne), lambda i: (0, 0))],
        out_specs=(pl.BlockSpec((tr, TOP_K), lambda i: (i, 0)), pl.BlockSpec((tr, TOP_K), lambda i: (i, 0)),
                   pl.BlockSpec((tr, TOP_K), lambda i: (i, 0)), pl.BlockSpec((1, ne), lambda i: (0, 0))),
        scratch_shapes=[pltpu.VMEM((1, ne), F32)],
        compiler_params=_cparams(("arbitrary",)),
        name="router",
    )(logits, bias)


def _dest_kernel(idx_ref, rank_ref, pstart_ref, dest_ref):
    idx = idx_ref[...]
    lane = lax.broadcasted_iota(I32, (idx.shape[0], N_EXPERTS), 1)
    cols = []
    for k in range(TOP_K):
        start = jnp.sum(jnp.where(lane == idx[:, k:k + 1], pstart_ref[...], 0.0), axis=-1, keepdims=True)
        cols.append(start.astype(I32) + rank_ref[:, k:k + 1])
    dest_ref[...] = jnp.concatenate(cols, axis=1)


def _dest(idx, rank, pstart):
    t = idx.shape[0]
    tr = _tile(t, 512)
    return pl.pallas_call(
        _dest_kernel,
        out_shape=jax.ShapeDtypeStruct((t, TOP_K), I32),
        grid=(t // tr,),
        in_specs=[pl.BlockSpec((tr, TOP_K), lambda i: (i, 0)), pl.BlockSpec((tr, TOP_K), lambda i: (i, 0)),
                  pl.BlockSpec((1, N_EXPERTS), lambda i: (0, 0))],
        out_specs=pl.BlockSpec((tr, TOP_K), lambda i: (i, 0)),
        compiler_params=_cparams(("parallel",)),
        name="dest",
    )(idx, rank, pstart)


def _dispatch_kernel(dest_hbm, h2_ref, xs_out, dest_smem, sem_idx, sem_rows):
    i = pl.program_id(0)
    td = h2_ref.shape[0]
    n = td * TOP_K
    cp = pltpu.make_async_copy(dest_hbm.at[pl.ds(i * n, n)], dest_smem, sem_idx)
    cp.start()
    cp.wait()

    def row_copy(t, d):
        return pltpu.make_async_copy(h2_ref.at[pl.ds(t, 1), :], xs_out.at[pl.ds(d, 1), :], sem_rows)

    def issue(t, carry):
        for k in range(TOP_K):
            row_copy(t, dest_smem[t * TOP_K + k]).start(priority=k % 2)
        return carry

    lax.fori_loop(0, td, issue, 0)

    def drain(t, carry):
        for k in range(TOP_K):
            row_copy(0, 0).wait()
        return carry

    lax.fori_loop(0, td, drain, 0)


def _dispatch(dest_flat, h2p, nrows):
    t, half = h2p.shape
    td = _tile(t, 256)
    return pl.pallas_call(
        _dispatch_kernel,
        out_shape=jax.ShapeDtypeStruct((nrows, half), U32),
        grid=(t // td,),
        in_specs=[pl.BlockSpec(memory_space=pl.ANY),
                  pl.BlockSpec((td, half), lambda i: (i, 0))],
        out_specs=pl.BlockSpec(memory_space=pl.ANY),
        scratch_shapes=[pltpu.SMEM((td * TOP_K,), I32), pltpu.SemaphoreType.DMA, pltpu.SemaphoreType.DMA],
        compiler_params=_cparams(("arbitrary",)),
        name="dispatch",
    )(dest_flat, h2p)


def _experts_kernel(be_ref, valid_ref, xs_ref, wg_ref, wu_ref, wd_ref, ys_ref, wgb_ref, wub_ref, wdb_ref):
    blk = pl.program_id(0)
    valid = valid_ref[blk]
    used = valid > 0

    @pl.when(used & ((blk == 0) | (be_ref[blk] != be_ref[jnp.maximum(blk - 1, 0)])))
    def _():
        wgb_ref[...] = wg_ref[0].astype(BF16)
        wub_ref[...] = wu_ref[0].astype(BF16)
        wdb_ref[...] = wd_ref[0].astype(BF16)

    @pl.when(used)
    def _():
        rows = lax.broadcasted_iota(I32, (xs_ref.shape[0], 1), 0)
        x = _unpack_rows(jnp.where(rows < valid, xs_ref[...], jnp.uint32(0)))
        g = _dot(x, wgb_ref[...])
        u = _dot(x, wub_ref[...])
        a = (_silu(g) * u).astype(BF16)
        ys_ref[...] = _pack_rows(_dot(a, wdb_ref[...]))

    @pl.when(jnp.logical_not(used))
    def _():
        ys_ref[...] = jnp.zeros(ys_ref.shape, U32)


def _experts(blk_expert, blk_valid, xs, wg, wu, wd):
    p, half = xs.shape
    d = 2 * half
    de = wg.shape[2]
    nblk = p // MOE_BLOCK
    grid_spec = pltpu.PrefetchScalarGridSpec(
        num_scalar_prefetch=2,
        grid=(nblk,),
        in_specs=[
            pl.BlockSpec((MOE_BLOCK, half), lambda i, be, nu: (i, 0)),
            pl.BlockSpec((1, d, de), lambda i, be, nu: (be[i], 0, 0)),
            pl.BlockSpec((1, d, de), lambda i, be, nu: (be[i], 0, 0)),
            pl.BlockSpec((1, de, d), lambda i, be, nu: (be[i], 0, 0)),
        ],
        out_specs=pl.BlockSpec((MOE_BLOCK, half), lambda i, be, nu: (i, 0)),
        scratch_shapes=[pltpu.VMEM((d, de), BF16), pltpu.VMEM((d, de), BF16), pltpu.VMEM((de, d), BF16)],
    )
    return pl.pallas_call(
        _experts_kernel,
        out_shape=jax.ShapeDtypeStruct((p, half), U32),
        grid_spec=grid_spec,
        compiler_params=_cparams(("arbitrary",)),
        name="experts",
    )(blk_expert, blk_valid, xs, wg, wu, wd)


def _combine_kernel(dest_hbm, w_ref, ys_hbm, h2_ref, x1_ref, mod_ref, wgs_ref, wus_ref, wds_ref, o_ref,
                    dest_smem, buf, sem_idx, sem_rows):
    i = pl.program_id(0)
    tc = h2_ref.shape[0]
    n = tc * TOP_K
    cp = pltpu.make_async_copy(dest_hbm.at[pl.ds(i * n, n)], dest_smem, sem_idx)
    cp.start()
    cp.wait()

    def row_copy(t, k, d):
        return pltpu.make_async_copy(ys_hbm.at[pl.ds(d, 1), :], buf.at[k, pl.ds(t, 1), :], sem_rows)

    def issue(t, carry):
        for k in range(TOP_K):
            row_copy(t, k, dest_smem[t * TOP_K + k]).start(priority=k % 2)
        return carry

    lax.fori_loop(0, tc, issue, 0)

    x = _unpack_rows(h2_ref[...])
    a = (_silu(_dot(x, wgs_ref[...])) * _dot(x, wus_ref[...])).astype(BF16)
    y = _dot(a, wds_ref[...])

    def drain(t, carry):
        for k in range(TOP_K):
            row_copy(0, k, 0).wait()
        return carry

    lax.fori_loop(0, tc, drain, 0)

    half = buf.shape[2]
    y_lo, y_hi = y[:, :half], y[:, half:]
    for k in range(TOP_K):
        wk = w_ref[:, k:k + 1]
        words = buf[k]
        y_lo = y_lo + pltpu.bitcast(words << 16, F32) * wk
        y_hi = y_hi + pltpu.bitcast(words & jnp.uint32(0xFFFF0000), F32) * wk
    o_ref[...] = x1_ref[...] + mod_ref[0, 5:6, :] * jnp.concatenate([y_lo, y_hi], axis=1)


def _combine(dest_flat, w, ys, h2p, x1, mod, wgs, wus, wds, seq):
    t, d = x1.shape
    half = d // 2
    tc = _tile(seq, 128)
    de = wgs.shape[1]
    per_batch = seq // tc
    const = lambda i: (0, 0)
    return pl.pallas_call(
        _combine_kernel,
        out_shape=jax.ShapeDtypeStruct((t, d), F32),
        grid=(t // tc,),
        in_specs=[
            pl.BlockSpec(memory_space=pl.ANY),
            pl.BlockSpec((tc, TOP_K), lambda i: (i, 0)),
            pl.BlockSpec(memory_space=pl.ANY),
            pl.BlockSpec((tc, half), lambda i: (i, 0)),
            pl.BlockSpec((tc, d), lambda i: (i, 0)),
            pl.BlockSpec((1, 6, d), lambda i: (i // per_batch, 0, 0)),
            pl.BlockSpec((d, de), const), pl.BlockSpec((d, de), const), pl.BlockSpec((de, d), const),
        ],
        out_specs=pl.BlockSpec((tc, d), lambda i: (i, 0)),
        scratch_shapes=[pltpu.SMEM((tc * TOP_K,), I32), pltpu.VMEM((TOP_K, tc, half), U32),
                        pltpu.SemaphoreType.DMA, pltpu.SemaphoreType.DMA],
        compiler_params=_cparams(("arbitrary",)),
        name="combine",
    )(dest_flat, w, ys, h2p, x1, mod, wgs, wus, wds)


def _rope_tables(n):
    rows = n // GRID_W
    row = jnp.broadcast_to(jnp.arange(rows)[:, None], (rows, GRID_W)).reshape(-1)
    col = jnp.broadcast_to(jnp.arange(GRID_W)[None, :], (rows, GRID_W)).reshape(-1)
    freqs = ROPE_BASE ** (-jnp.arange(ROPE_FREQS, dtype=F32) / ROPE_FREQS)
    pos = jnp.stack([row, col], axis=-1).astype(F32)
    ang = pos[:, :, None] * freqs
    cos, sin = jnp.cos(ang), jnp.sin(ang)
    cos64 = jnp.concatenate([cos[:, 0], cos[:, 0], cos[:, 1], cos[:, 1]], axis=-1)
    sin64 = jnp.concatenate([-sin[:, 0], sin[:, 0], -sin[:, 1], sin[:, 1]], axis=-1)
    reps = MXU_N // HEAD_DIM
    return jnp.tile(cos64, (1, reps)), jnp.tile(sin64, (1, reps))


def _gain_rows(qn_a, kn_a, qn_b, kn_b):
    reps = MXU_N // HEAD_DIM
    ones = jnp.ones((LANES,), F32)
    kb = jnp.tile(kn_b, reps)
    qa = jnp.tile(qn_a, reps) * ATTN_SCALE
    qb = jnp.tile(qn_b, reps) * (ATTN_SCALE * LOG2E)
    mixed = jnp.concatenate([jnp.tile(kn_a, LANES // HEAD_DIM), ones])
    lat_g = jnp.stack([kb] * 4 + [qa] * 4 + [qb] * 4 + [mixed])[:, None, :]
    ctx_g = jnp.stack([kb] * 4 + [mixed])[:, None, :]
    return lat_g, ctx_g


def kernel(x, c, ctx, c_ctx, w_ada, b_ada, norm1_g, norm2_g, w_in, qnorm_a, knorm_a, sink_a, qnorm_b, knorm_b,
           lam_q1, lam_k1, lam_q2, lam_k2, subln_g, w_pa, w_pb, w_o, w_router, router_bias, w_gate_e, w_up_e,
           w_down_e, w_gate_s, w_up_s, w_down_s):
    b, s, d = x.shape
    nctx = ctx.shape[1]
    assert w_ada.shape[0] == 1 and d == 1024 and s % BLOCK == 0 and s % GRID_W == 0
    t = b * s

    cc = jnp.concatenate([c, c_ctx[None, :], jnp.zeros((16 - b - 1, d), F32)], axis=0)
    mod_all = _ada(cc, w_ada[0], b_ada[0])
    mod = mod_all[:b].reshape(b, 6, d)
    mod_c = jnp.broadcast_to(mod_all[b].reshape(1, 6, d), (b, 6, d))

    w = w_in[0]
    ka_w, va_w = w[:, 0:128], w[:, 128:256]
    kb_w, vb_w = w[:, 256:1280], w[:, 1280:2304]
    qa_w, qb_w = w[:, 2304:3328], w[:, 3328:4352]
    gate_w = w[:, 4352:6400].astype(BF16)
    w_lat = jnp.concatenate([kb_w, qa_w, qb_w, ka_w, va_w, vb_w], axis=1).astype(BF16)
    w_ctx = jnp.concatenate([kb_w, ka_w, va_w, vb_w], axis=1).astype(BF16)

    lat_g, ctx_g = _gain_rows(qnorm_a[0], knorm_a[0], qnorm_b[0], knorm_b[0])
    hid = jnp.arange(MXU_N) // HEAD_DIM
    seg = jnp.where(hid[:, None] == hid[None, :], 1.0 / HEAD_DIM, 0.0).astype(BF16)
    cos_t, sin_t = _rope_tables(s)
    n1 = norm1_g[0].reshape(1, d)
    n2 = norm2_g[0].reshape(1, d)

    lat, vt = _proj(x, mod, n1, w_lat, lat_g, seg, cos_t, sin_t, nqk=LAT_NQK, use_rope=True)
    ctxp, vtx = _proj(ctx, mod_c, n1, w_ctx, ctx_g, seg, cos_t[:nctx], sin_t[:nctx], nqk=CTX_NQK, use_rope=False)

    oa = _attn_a(sink_a[0], lat, ctxp)
    lamv = jnp.stack([lam_q1[0], lam_k1[0], lam_q2[0], lam_k2[0]])
    bound = (HEAD_DIM * ATTN_SCALE * LOG2E * 1.01) * jnp.max(jnp.abs(qnorm_b[0])) * jnp.max(jnp.abs(knorm_b[0]))
    ob = _attn_b(bound.reshape(1), lamv, subln_g[0].reshape(LANES, 1), lat, vt, ctxp, vtx)

    x1, h2p, logits = _merge(x, mod, n1, n2, oa, ob, gate_w, w_pa[0].astype(BF16), w_pb[0].astype(BF16),
                             w_o[0].astype(BF16), w_router[0].astype(BF16))
    x1 = x1.reshape(t, d)
    h2p = h2p.reshape(t, d // 2)
    logits = logits.reshape(t, N_EXPERTS)

    idx, wts, rank, counts = _router(logits, router_bias[0].reshape(1, N_EXPERTS))

    cnt = counts[0].astype(I32)
    padded = (cnt + MOE_BLOCK - 1) // MOE_BLOCK * MOE_BLOCK
    pend = jnp.cumsum(padded)
    pstart = pend - padded
    nblk = -(-(t * TOP_K) // MOE_BLOCK) + N_EXPERTS
    blk_row0 = jnp.arange(nblk, dtype=I32) * MOE_BLOCK
    blk_expert = jnp.minimum(jnp.sum((pend[None, :] <= blk_row0[:, None]).astype(I32), axis=1), N_EXPERTS - 1)
    blk_valid = jnp.clip(pstart[blk_expert] + cnt[blk_expert] - blk_row0, 0, MOE_BLOCK).astype(I32)

    dest = _dest(idx, rank, pstart.astype(F32).reshape(1, N_EXPERTS)).reshape(t * TOP_K)
    xs = _dispatch(dest, h2p, nblk * MOE_BLOCK)
    ys = _experts(blk_expert, blk_valid, xs, w_gate_e[0], w_up_e[0], w_down_e[0])
    out = _combine(dest, wts, ys, h2p, x1, mod, w_gate_s[0].astype(BF16), w_up_s[0].astype(BF16),
                   w_down_s[0].astype(BF16), s)
    return out.reshape(b, s, d)
```

```python
import functools
import math

import jax
import jax.numpy as jnp
from jax import lax
from jax.experimental import pallas as pl
from jax.experimental.pallas import tpu as pltpu

F32 = jnp.float32
BF16 = jnp.bfloat16
I32 = jnp.int32
U32 = jnp.uint32

HEAD_DIM = 64
GRID_W = 64
ROPE_FREQS = HEAD_DIM // 4
ROPE_BASE = 10000.0
EPS = 1e-6
ATTN_SCALE = HEAD_DIM ** -0.5
BLOCK = 128
A_Q_HEADS = 16
A_KV_HEADS = 2
A_GROUP = A_Q_HEADS // A_KV_HEADS
B_HEADS = 8
N_EXPERTS = 256
TOP_K = 8
N_GROUPS = 8
TOPK_GROUPS = 4
GROUP_SIZE = N_EXPERTS // N_GROUPS
ROUTED_SCALE = 2.5
MOE_BLOCK = 512
LAM_INIT = 0.8 - 0.6 * math.exp(-0.3 * 0)

LANES = 128
MXU_N = 256
VMEM_LIMIT = 56 * 1024 * 1024
NEG_BIG = -1e30
LOG2E = math.log2(math.e)
MAX_FIXED_SHIFT = 60.0

LAT_KB, LAT_QA, LAT_QB, LAT_KA, LAT_VA = 0, 8, 16, 24, 25
LAT_NQK = 13
CTX_KB, CTX_KA, CTX_VA = 0, 8, 9
CTX_NQK = 5


def _tile(n, pref):
    return pref if n % pref == 0 else n


def _cparams(sem):
    return pltpu.CompilerParams(dimension_semantics=sem, vmem_limit_bytes=VMEM_LIMIT)


def _nt_dot(a, b):
    return lax.dot_general(a, b, (((1,), (1,)), ((), ())), preferred_element_type=F32)


def _dot(a, b):
    return jnp.dot(a, b, preferred_element_type=F32)


def _silu(x):
    return x * (1.0 / (1.0 + jnp.exp(-x)))


def _sigmoid(x):
    return 1.0 / (1.0 + jnp.exp(-x))


def _modulated_norm(xf, g_row, shift_row, scale_row):
    ms = jnp.mean(xf * xf, axis=-1, keepdims=True)
    y = xf * lax.rsqrt(ms + EPS) * g_row
    return y * (1.0 + scale_row) + shift_row


def _ada_kernel(c_ref, w_ref, b_ref, o_ref):
    o_ref[...] = _dot(_silu(c_ref[...]).astype(BF16), w_ref[...].astype(BF16)) + b_ref[...]


def _ada(cc, w_ada, b_ada):
    rows, d = cc.shape
    n = w_ada.shape[1]
    tn = _tile(n, 512)
    return pl.pallas_call(
        _ada_kernel,
        out_shape=jax.ShapeDtypeStruct((rows, n), F32),
        grid=(n // tn,),
        in_specs=[pl.BlockSpec((rows, d), lambda j: (0, 0)),
                  pl.BlockSpec((d, tn), lambda j: (0, j)),
                  pl.BlockSpec((1, tn), lambda j: (0, j))],
        out_specs=pl.BlockSpec((rows, tn), lambda j: (0, j)),
        compiler_params=_cparams(("parallel",)),
        name="ada",
    )(cc, w_ada, b_ada.reshape(1, n))


def _proj_kernel(x_ref, mod_ref, n1_ref, w_ref, gain_ref, seg_ref, cos_ref, sin_ref, o_ref, vt_ref,
                 *, nqk, use_rope):
    nj = w_ref.shape[1] // MXU_N
    h = _modulated_norm(x_ref[0], n1_ref[...], mod_ref[0, 0:1, :], mod_ref[0, 1:2, :]).astype(BF16)
    lane = lax.broadcasted_iota(I32, (1, MXU_N), 1)
    first = (lane % (2 * ROPE_FREQS)) < ROPE_FREQS

    def matmul(j):
        return _dot(h, w_ref[:, j * MXU_N:(j + 1) * MXU_N])

    def finish(j, acc):
        if j >= nqk:
            vt_ref[0, (j - nqk) * MXU_N:(j - nqk + 1) * MXU_N, :] = acc.T.astype(BF16)
            return
        ms = _dot((acc * acc).astype(BF16), seg_ref[...])
        y = acc * lax.rsqrt(ms + EPS) * gain_ref[j]
        if use_rope:
            rot = jnp.where(first, pltpu.roll(y, MXU_N - ROPE_FREQS, 1), pltpu.roll(y, ROPE_FREQS, 1))
            y = y * cos_ref[...] + rot * sin_ref[...]
        if j == nqk - 1:
            y = jnp.where(lane < LANES, y, acc)
        o_ref[0, :, j * MXU_N:(j + 1) * MXU_N] = y.astype(BF16)

    acc = matmul(0)
    for j in range(nj):
        nxt = matmul(j + 1) if j + 1 < nj else None
        finish(j, acc)
        acc = nxt


def _proj(x, mod, n1, w, gains, seg, cos_t, sin_t, *, nqk, use_rope):
    b, s, d = x.shape
    ncols = w.shape[1]
    tm = _tile(s, 512)
    nj = ncols // MXU_N
    const2 = lambda bi, si: (0, 0)
    return pl.pallas_call(
        functools.partial(_proj_kernel, nqk=nqk, use_rope=use_rope),
        out_shape=(jax.ShapeDtypeStruct((b, s, nqk * MXU_N), BF16),
                   jax.ShapeDtypeStruct((b, (nj - nqk) * MXU_N, s), BF16)),
        grid=(b, s // tm),
        in_specs=[
            pl.BlockSpec((1, tm, d), lambda bi, si: (bi, si, 0)),
            pl.BlockSpec((1, 6, d), lambda bi, si: (bi, 0, 0)),
            pl.BlockSpec((1, d), const2),
            pl.BlockSpec((d, ncols), const2),
            pl.BlockSpec((nqk, 1, MXU_N), lambda bi, si: (0, 0, 0)),
            pl.BlockSpec((MXU_N, MXU_N), const2),
            pl.BlockSpec((tm, MXU_N), lambda bi, si: (si, 0)),
            pl.BlockSpec((tm, MXU_N), lambda bi, si: (si, 0)),
        ],
        out_specs=(pl.BlockSpec((1, tm, nqk * MXU_N), lambda bi, si: (bi, si, 0)),
                   pl.BlockSpec((1, (nj - nqk) * MXU_N, tm), lambda bi, si: (bi, 0, si))),
        compiler_params=_cparams(("parallel", "parallel")),
        name="proj_rope" if use_rope else "proj_ctx",
    )(x, mod, n1, w, gains, seg, cos_t, sin_t)


def _attn_a_kernel(sink_ref, q_ref, kp_ref, kc_ref, kn_ref, vp_ref, vc_ref, vn_ref, kx_ref, vx_ref, o_ref):
    i = pl.program_id(1)
    nb = pl.num_programs(1)
    nctx = kx_ref.shape[1]
    span = 3 * BLOCK + nctx
    kall = jnp.concatenate([kp_ref[0], kc_ref[0], kn_ref[0], kx_ref[0]], axis=0)
    vall = jnp.concatenate([vp_ref[0], vc_ref[0], vn_ref[0], vx_ref[0]], axis=0)

    r = lax.broadcasted_iota(I32, (BLOCK, span), 0)
    c = lax.broadcasted_iota(I32, (BLOCK, span), 1)
    prev_ok = (c < BLOCK) & (c >= r) & (i > 0)
    cur_ok = (c >= BLOCK) & (c < 2 * BLOCK)
    next_ok = (c >= 2 * BLOCK) & (c < 3 * BLOCK) & (c - 2 * BLOCK <= r) & (i < nb - 1)
    valid = prev_ok | cur_ok | next_ok | (c >= 3 * BLOCK)
    bias = jnp.where(valid, 0.0, NEG_BIG)

    lane = lax.broadcasted_iota(I32, (BLOCK, LANES), 1)
    for kv in range(A_KV_HEADS):
        in_half = (lane >= kv * HEAD_DIM) & (lane < (kv + 1) * HEAD_DIM)
        qs = []
        for g in range(A_GROUP):
            hh = kv * A_GROUP + g
            blk = q_ref[0, :, (hh // 2) * LANES:(hh // 2 + 1) * LANES].astype(F32)
            if hh % 2 != kv:
                blk = pltpu.roll(blk, HEAD_DIM, 1)
            qs.append(jnp.where(in_half, blk, 0.0).astype(BF16))
        s_all = _nt_dot(jnp.concatenate(qs, axis=0), kall)
        ps, ls = [], []
        for g in range(A_GROUP):
            sink = sink_ref[kv, g]
            s = s_all[g * BLOCK:(g + 1) * BLOCK] + bias
            m = jnp.maximum(jnp.max(s, axis=-1, keepdims=True), sink)
            p = jnp.exp(s - m)
            ls.append(jnp.sum(p, axis=-1, keepdims=True) + jnp.exp(sink - m))
            ps.append(p.astype(BF16))
        o_all = _dot(jnp.concatenate(ps, axis=0), vall)
        os_ = [o_all[g * BLOCK:(g + 1) * BLOCK] / ls[g] for g in range(A_GROUP)]
        for pair in range(A_GROUP // 2):
            o0, o1 = os_[2 * pair], os_[2 * pair + 1]
            if kv == 0:
                o1 = pltpu.roll(o1, HEAD_DIM, 1)
            else:
                o0 = pltpu.roll(o0, HEAD_DIM, 1)
            col = (kv * (A_GROUP // 2) + pair) * LANES
            o_ref[0, :, col:col + LANES] = jnp.where(lane < HEAD_DIM, o0, o1).astype(BF16)


def _attn_a(sink, lat, ctxp):
    b, s, _ = lat.shape
    nctx = ctxp.shape[1]
    nb = s // BLOCK
    width = A_Q_HEADS * HEAD_DIM

    def kspec(col, shift):
        return pl.BlockSpec((1, BLOCK, LANES),
                            lambda bi, i: (bi, jnp.clip(i + shift, 0, nb - 1), col))

    return pl.pallas_call(
        _attn_a_kernel,
        out_shape=jax.ShapeDtypeStruct((b, s, width), BF16),
        grid=(b, nb),
        in_specs=[
            pl.BlockSpec(memory_space=pltpu.SMEM),
            pl.BlockSpec((1, BLOCK, width), lambda bi, i: (bi, i, LAT_QA * LANES // width)),
            kspec(LAT_KA, -1), kspec(LAT_KA, 0), kspec(LAT_KA, 1),
            kspec(LAT_VA, -1), kspec(LAT_VA, 0), kspec(LAT_VA, 1),
            pl.BlockSpec((1, nctx, LANES), lambda bi, i: (bi, 0, CTX_KA)),
            pl.BlockSpec((1, nctx, LANES), lambda bi, i: (bi, 0, CTX_VA)),
        ],
        out_specs=pl.BlockSpec((1, BLOCK, width), lambda bi, i: (bi, i, 0)),
        compiler_params=_cparams(("parallel", "parallel")),
        name="attn_a",
    )(sink, lat, lat, lat, lat, lat, lat, lat, ctxp, ctxp)


def _attn_b_kernel(bound_ref, lamv_ref, sub_ref, q_ref, k_ref, vt_ref, kx_ref, vtx_ref, o_ref, acc_ref, l_ref,
                   s0_ref, s1_ref, sx_ref, *, tk):
    tq = q_ref.shape[1]
    s_len = k_ref.shape[1]
    qt = q_ref[0].astype(F32).T
    row = lax.broadcasted_iota(I32, (LANES, tq), 0)
    qts = [jnp.where(row < HEAD_DIM, qt, 0.0).astype(BF16), jnp.where(row >= HEAD_DIM, qt, 0.0).astype(BF16)]
    acc_ref[...] = jnp.zeros(acc_ref.shape, F32)
    n = s_len // tk
    bound = bound_ref[0]

    def kchunk(i):
        return k_ref[0, pl.ds(pl.multiple_of(i * tk, tk), tk), :]

    def vchunk(i):
        return vt_ref[0, :, pl.ds(pl.multiple_of(i * tk, tk), tk)]

    def run(fixed):
        def scores(j, kc, s_ref):
            s = _dot(kc, qts[j])
            s_ref[j] = s
            return jnp.zeros((1, tq), F32) if fixed else jnp.max(s, axis=0, keepdims=True)

        def accumulate(j, s_ref, cm, vtc, stat):
            m_old, l_old = stat
            if fixed:
                p = jnp.exp2(s_ref[j] - bound)
                acc_ref[j] += _dot(vtc, p.astype(BF16))
                return m_old, l_old + jnp.sum(p, axis=0, keepdims=True)
            m_new = jnp.maximum(m_old, cm)
            alpha = jnp.exp2(m_old - m_new)
            p = jnp.exp2(s_ref[j] - m_new)
            acc_ref[j] = alpha * acc_ref[j] + _dot(vtc, p.astype(BF16))
            return m_new, alpha * l_old + jnp.sum(p, axis=0, keepdims=True)

        def half_step(k_next, s_next, s_cur, cms, vtc, stats):
            new_cms, new_stats = [], []
            for j in range(2):
                new_cms.append(scores(j, k_next, s_next))
                new_stats.append(accumulate(j, s_cur, cms[j], vtc, stats[j]))
            return tuple(new_cms), tuple(new_stats)

        m0 = jnp.zeros((1, tq), F32) if fixed else jnp.full((1, tq), NEG_BIG, F32)
        stats = ((m0, jnp.zeros((1, tq), F32)),) * 2
        cmx = tuple(scores(j, kx_ref[0], sx_ref) for j in range(2))
        cms, stats = half_step(kchunk(0), s0_ref, sx_ref, cmx, vtx_ref[0], stats)

        def body(pair, carry):
            cms, stats = carry
            i = 2 * pair
            cms, stats = half_step(kchunk(i + 1), s1_ref, s0_ref, cms, vchunk(i), stats)
            return half_step(kchunk(i + 2), s0_ref, s1_ref, cms, vchunk(i + 1), stats)

        cms, stats = lax.fori_loop(0, n // 2 - 1, body, (cms, stats))
        cms, stats = half_step(kchunk(n - 1), s1_ref, s0_ref, cms, vchunk(n - 2), stats)
        for j in range(2):
            l_ref[j] = accumulate(j, s1_ref, cms[j], vchunk(n - 1), stats[j])[1]

    pl.when(bound <= MAX_FIXED_SHIFT)(lambda: run(True))
    pl.when(bound > MAX_FIXED_SHIFT)(lambda: run(False))


    lv = lamv_ref[...]
    lam = (jnp.exp(jnp.sum(lv[0:1] * lv[1:2], axis=-1, keepdims=True))
           - jnp.exp(jnp.sum(lv[2:3] * lv[3:4], axis=-1, keepdims=True)) + LAM_INIT)
    o = acc_ref[0] * (1.0 / l_ref[0]) - acc_ref[1] * (lam / l_ref[1])
    ms = jnp.mean(o * o, axis=0, keepdims=True)
    o = o * lax.rsqrt(ms + EPS) * (sub_ref[...] * (1.0 - LAM_INIT))
    o_ref[0] = o.T.astype(BF16)


def _attn_b(bound, lamv, subln_col, lat, vt, ctxp, vtx):
    b, s, _ = lat.shape
    nctx = ctxp.shape[1]
    tq = _tile(s, 512)
    tk = min(512, s // 2)
    assert s % (2 * tk) == 0
    return pl.pallas_call(
        functools.partial(_attn_b_kernel, tk=tk),
        out_shape=jax.ShapeDtypeStruct((b, s, B_HEADS * LANES), BF16),
        grid=(b, B_HEADS, s // tq),
        in_specs=[
            pl.BlockSpec(memory_space=pltpu.SMEM),
            pl.BlockSpec((4, HEAD_DIM), lambda bi, h, qi: (0, 0)),
            pl.BlockSpec((LANES, 1), lambda bi, h, qi: (0, 0)),
            pl.BlockSpec((1, tq, LANES), lambda bi, h, qi: (bi, qi, LAT_QB + h)),
            pl.BlockSpec((1, s, LANES), lambda bi, h, qi: (bi, 0, LAT_KB + h)),
            pl.BlockSpec((1, LANES, s), lambda bi, h, qi: (bi, h, 0)),
            pl.BlockSpec((1, nctx, LANES), lambda bi, h, qi: (bi, 0, CTX_KB + h)),
            pl.BlockSpec((1, LANES, nctx), lambda bi, h, qi: (bi, h, 0)),
        ],
        out_specs=pl.BlockSpec((1, tq, LANES), lambda bi, h, qi: (bi, qi, h)),
        scratch_shapes=[pltpu.VMEM((2, LANES, tq), F32), pltpu.VMEM((2, 1, tq), F32),
                        pltpu.VMEM((2, tk, tq), F32), pltpu.VMEM((2, tk, tq), F32),
                        pltpu.VMEM((2, nctx, tq), F32)],
        compiler_params=_cparams(("parallel", "parallel", "arbitrary")),
        name="attn_b",
    )(bound, lamv, subln_col, lat, lat, vt, ctxp, vtx)


def _pack_rows(h):
    n = h.shape[1] // 2
    bits = pltpu.bitcast(h.astype(BF16).astype(F32), U32)
    return (bits[:, :n] >> 16) | (bits[:, n:] & jnp.uint32(0xFFFF0000))


def _unpack_rows(w):
    lo = pltpu.bitcast(w << 16, F32)
    hi = pltpu.bitcast(w & jnp.uint32(0xFFFF0000), F32)
    return jnp.concatenate([lo, hi], axis=1).astype(BF16)


def _merge_kernel(x_ref, mod_ref, n1_ref, n2_ref, oa_ref, ob_ref, wg_ref, wpa_ref, wpb_ref, wo_ref, wr_ref,
                  x1_ref, h2_ref, lg_ref):
    d = x_ref.shape[2]
    xf = x_ref[0]
    h = _modulated_norm(xf, n1_ref[...], mod_ref[0, 0:1, :], mod_ref[0, 1:2, :]).astype(BF16)
    gates = _sigmoid(_dot(h, wg_ref[...]))
    ya = _dot(oa_ref[0], wpa_ref[...])
    yb = _dot(ob_ref[0], wpb_ref[...])
    z = gates[:, :d] * ya + gates[:, d:] * yb
    x1 = xf + mod_ref[0, 2:3, :] * _dot(z.astype(BF16), wo_ref[...])
    x1_ref[0] = x1
    h2 = _modulated_norm(x1, n2_ref[...], mod_ref[0, 3:4, :], mod_ref[0, 4:5, :])
    lg_ref[0] = _dot(h2.astype(BF16), wr_ref[...])
    h2_ref[0] = _pack_rows(h2)


def _merge(x, mod, n1, n2, oa, ob, wg, wpa, wpb, wo, wr):
    b, s, d = x.shape
    tm = _tile(s, 256)
    ne = wr.shape[1]
    const = lambda bi, si: (0, 0)
    row = lambda bi, si: (bi, si, 0)
    return pl.pallas_call(
        _merge_kernel,
        out_shape=(jax.ShapeDtypeStruct((b, s, d), F32),
                   jax.ShapeDtypeStruct((b, s, d // 2), U32),
                   jax.ShapeDtypeStruct((b, s, ne), F32)),
        grid=(b, s // tm),
        in_specs=[
            pl.BlockSpec((1, tm, d), row),
            pl.BlockSpec((1, 6, d), lambda bi, si: (bi, 0, 0)),
            pl.BlockSpec((1, d), const), pl.BlockSpec((1, d), const),
            pl.BlockSpec((1, tm, d), row), pl.BlockSpec((1, tm, d), row),
            pl.BlockSpec((d, 2 * d), const), pl.BlockSpec((d, d), const),
            pl.BlockSpec((d, d), const), pl.BlockSpec((d, d), const),
            pl.BlockSpec((d, ne), const),
        ],
        out_specs=(pl.BlockSpec((1, tm, d), row), pl.BlockSpec((1, tm, d // 2), row),
                   pl.BlockSpec((1, tm, ne), row)),
        compiler_params=_cparams(("parallel", "parallel")),
        name="merge",
    )(x, mod, n1, n2, oa, ob, wg, wpa, wpb, wo, wr)


def _router_kernel(lg_ref, bias_ref, idx_ref, w_ref, rank_ref, cnt_ref, run_ref):
    i = pl.program_id(0)
    tr = lg_ref.shape[0]

    @pl.when(i == 0)
    def _():
        run_ref[...] = jnp.zeros(run_ref.shape, F32)

    scores = _sigmoid(lg_ref[...].T)
    biased = scores + bias_ref[...]
    row = lax.broadcasted_iota(I32, scores.shape, 0)
    neg_inf = -jnp.inf

    def first_argmax(vals, rows):
        m = jnp.max(vals, axis=0, keepdims=True)
        idx = jnp.min(jnp.where(vals == m, rows, N_EXPERTS), axis=0, keepdims=True)
        return m, idx

    def group(a, g):
        return a[g * GROUP_SIZE:(g + 1) * GROUP_SIZE]

    gscore = []
    for g in range(N_GROUPS):
        vals = group(biased, g)
        rows = lax.broadcasted_iota(I32, vals.shape, 0) + g * GROUP_SIZE
        m1, i1 = first_argmax(vals, rows)
        m2 = jnp.max(jnp.where(rows == i1, neg_inf, vals), axis=0, keepdims=True)
        gscore.append(m1 + m2)
    parts = []
    for g in range(N_GROUPS):
        beaten = jnp.zeros((1, tr), F32)
        for g2 in range(N_GROUPS):
            if g2 == g:
                continue
            wins = (gscore[g2] > gscore[g]) | ((gscore[g2] == gscore[g]) & (g2 < g))
            beaten = beaten + jnp.where(wins, 1.0, 0.0)
        parts.append(jnp.where(beaten < TOPK_GROUPS, group(biased, g), neg_inf))
    masked = jnp.concatenate(parts, axis=0)

    sel = jnp.zeros(scores.shape, F32)
    idxs, ws = [], []
    for _ in range(TOP_K):
        _, ik = first_argmax(masked, row)
        hit = row == ik
        idxs.append(ik)
        ws.append(jnp.sum(jnp.where(hit, scores, 0.0), axis=0, keepdims=True))
        sel = jnp.where(hit, 1.0, sel)
        masked = jnp.where(hit, neg_inf, masked)
    wsum = ws[0]
    for k in range(1, TOP_K):
        wsum = wsum + ws[k]

    rr = lax.broadcasted_iota(I32, (tr, tr), 0)
    cc = lax.broadcasted_iota(I32, (tr, tr), 1)
    earlier = jnp.where(rr < cc, 1.0, 0.0).astype(BF16)
    run = run_ref[...]
    selb = sel.astype(BF16)
    before = _dot(selb, earlier) + run
    ranks = [jnp.sum(jnp.where(row == idxs[k], before, 0.0), axis=0, keepdims=True) for k in range(TOP_K)]
    run_ref[...] = run + _dot(selb, jnp.ones((tr, tr), BF16))
    cnt_ref[...] = run_ref[...]

    def stack(rows_):
        slot = lax.broadcasted_iota(I32, (TOP_K, tr), 0)
        out = jnp.broadcast_to(rows_[0], (TOP_K, tr))
        for k in range(1, TOP_K):
            out = jnp.where(slot == k, rows_[k], out)
        return out

    idx_ref[...] = stack(idxs)
    w_ref[...] = stack([w / wsum * ROUTED_SCALE for w in ws])
    rank_ref[...] = stack(ranks).astype(I32)


def _router(logits, bias):
    t, ne = logits.shape
    tr = _tile(t, 512)
    bias_b = jnp.broadcast_to(bias[:, None], (ne, tr))
    return pl.pallas_call(
        _router_kernel,
        out_shape=(jax.ShapeDtypeStruct((TOP_K, t), I32), jax.ShapeDtypeStruct((TOP_K, t), F32),
                   jax.ShapeDtypeStruct((TOP_K, t), I32), jax.ShapeDtypeStruct((ne, tr), F32)),
        grid=(t // tr,),
        in_specs=[pl.BlockSpec((tr, ne), lambda i: (i, 0)), pl.BlockSpec((ne, tr), lambda i: (0, 0))],
        out_specs=(pl.BlockSpec((TOP_K, tr), lambda i: (0, i)), pl.BlockSpec((TOP_K, tr), lambda i: (0, i)),
                   pl.BlockSpec((TOP_K, tr), lambda i: (0, i)), pl.BlockSpec((ne, tr), lambda i: (0, 0))),
        scratch_shapes=[pltpu.VMEM((ne, tr), F32)],
        compiler_params=_cparams(("arbitrary",)),
        name="router",
    )(logits, bias_b)


def _dest_kernel(idx_ref, rank_ref, pstart_ref, dest_ref):
    idx = idx_ref[...]
    lane = lax.broadcasted_iota(I32, (idx.shape[0], N_EXPERTS), 1)
    cols = []
    for k in range(TOP_K):
        start = jnp.sum(jnp.where(lane == idx[:, k:k + 1], pstart_ref[...], 0.0), axis=-1, keepdims=True)
        cols.append(start.astype(I32) + rank_ref[:, k:k + 1])
    dest_ref[...] = jnp.concatenate(cols, axis=1)


def _dest(idx, rank, pstart):
    t = idx.shape[0]
    tr = _tile(t, 512)
    return pl.pallas_call(
        _dest_kernel,
        out_shape=jax.ShapeDtypeStruct((t, TOP_K), I32),
        grid=(t // tr,),
        in_specs=[pl.BlockSpec((tr, TOP_K), lambda i: (i, 0)), pl.BlockSpec((tr, TOP_K), lambda i: (i, 0)),
                  pl.BlockSpec((1, N_EXPERTS), lambda i: (0, 0))],
        out_specs=pl.BlockSpec((tr, TOP_K), lambda i: (i, 0)),
        compiler_params=_cparams(("parallel",)),
        name="dest",
    )(idx, rank, pstart)


def _dispatch_kernel(dest_hbm, h2_ref, xs_out, dest_smem, sem_idx, sem_rows):
    i = pl.program_id(0)
    td = h2_ref.shape[0]
    n = td * TOP_K
    cp = pltpu.make_async_copy(dest_hbm.at[pl.ds(i * n, n)], dest_smem, sem_idx)
    cp.start()
    cp.wait()

    def row_copy(t, d):
        return pltpu.make_async_copy(h2_ref.at[pl.ds(t, 1), :], xs_out.at[pl.ds(d, 1), :], sem_rows)

    def issue(t, carry):
        for k in range(TOP_K):
            row_copy(t, dest_smem[t * TOP_K + k]).start(priority=k % 2)
        return carry

    lax.fori_loop(0, td, issue, 0)

    def drain(t, carry):
        for k in range(TOP_K):
            row_copy(0, 0).wait()
        return carry

    lax.fori_loop(0, td, drain, 0)


def _dispatch(dest_flat, h2p, nrows):
    t, half = h2p.shape
    td = _tile(t, 1024)
    return pl.pallas_call(
        _dispatch_kernel,
        out_shape=jax.ShapeDtypeStruct((nrows, half), U32),
        grid=(t // td,),
        in_specs=[pl.BlockSpec(memory_space=pl.ANY),
                  pl.BlockSpec((td, half), lambda i: (i, 0))],
        out_specs=pl.BlockSpec(memory_space=pl.ANY),
        scratch_shapes=[pltpu.SMEM((td * TOP_K,), I32), pltpu.SemaphoreType.DMA, pltpu.SemaphoreType.DMA],
        compiler_params=_cparams(("arbitrary",)),
        name="dispatch",
    )(dest_flat, h2p)


def _experts_kernel(be_ref, valid_ref, xs_ref, wg_ref, wu_ref, wd_ref, ys_ref, wgb_ref, wub_ref, wdb_ref):
    blk = pl.program_id(0)
    valid = valid_ref[blk]
    used = valid > 0

    @pl.when(used & ((blk == 0) | (be_ref[blk] != be_ref[jnp.maximum(blk - 1, 0)])))
    def _():
        wgb_ref[...] = wg_ref[0].astype(BF16)
        wub_ref[...] = wu_ref[0].astype(BF16)
        wdb_ref[...] = wd_ref[0].astype(BF16)

    @pl.when(used)
    def _():
        rows = lax.broadcasted_iota(I32, (xs_ref.shape[0], 1), 0)
        x = _unpack_rows(jnp.where(rows < valid, xs_ref[...], jnp.uint32(0)))
        g = _dot(x, wgb_ref[...])
        u = _dot(x, wub_ref[...])
        a = (_silu(g) * u).astype(BF16)
        ys_ref[...] = _pack_rows(_dot(a, wdb_ref[...]))

    @pl.when(jnp.logical_not(used))
    def _():
        ys_ref[...] = jnp.zeros(ys_ref.shape, U32)


def _experts(blk_expert, blk_valid, xs, wg, wu, wd):
    p, half = xs.shape
    d = 2 * half
    de = wg.shape[2]
    nblk = p // MOE_BLOCK
    grid_spec = pltpu.PrefetchScalarGridSpec(
        num_scalar_prefetch=2,
        grid=(nblk,),
        in_specs=[
            pl.BlockSpec((MOE_BLOCK, half), lambda i, be, nu: (i, 0)),
            pl.BlockSpec((1, d, de), lambda i, be, nu: (be[i], 0, 0)),
            pl.BlockSpec((1, d, de), lambda i, be, nu: (be[i], 0, 0)),
            pl.BlockSpec((1, de, d), lambda i, be, nu: (be[i], 0, 0)),
        ],
        out_specs=pl.BlockSpec((MOE_BLOCK, half), lambda i, be, nu: (i, 0)),
        scratch_shapes=[pltpu.VMEM((d, de), BF16), pltpu.VMEM((d, de), BF16), pltpu.VMEM((de, d), BF16)],
    )
    return pl.pallas_call(
        _experts_kernel,
        out_shape=jax.ShapeDtypeStruct((p, half), U32),
        grid_spec=grid_spec,
        compiler_params=_cparams(("arbitrary",)),
        name="experts",
    )(blk_expert, blk_valid, xs, wg, wu, wd)


def _combine_kernel(dest_hbm, w_ref, ys_hbm, h2_ref, x1_ref, mod_ref, wgs_ref, wus_ref, wds_ref, o_ref,
                    dest_smem, buf, sem_idx, sem_rows):
    i = pl.program_id(0)
    tc = h2_ref.shape[0]
    n = tc * TOP_K
    cp = pltpu.make_async_copy(dest_hbm.at[pl.ds(i * n, n)], dest_smem, sem_idx)
    cp.start()
    cp.wait()

    def row_copy(t, k, d):
        return pltpu.make_async_copy(ys_hbm.at[pl.ds(d, 1), :], buf.at[k, pl.ds(t, 1), :], sem_rows)

    def issue(t, carry):
        for k in range(TOP_K):
            row_copy(t, k, dest_smem[t * TOP_K + k]).start(priority=k % 2)
        return carry

    lax.fori_loop(0, tc, issue, 0)

    x = _unpack_rows(h2_ref[...])
    a = (_silu(_dot(x, wgs_ref[...])) * _dot(x, wus_ref[...])).astype(BF16)
    y = _dot(a, wds_ref[...])

    def drain(t, carry):
        for k in range(TOP_K):
            row_copy(0, k, 0).wait()
        return carry

    lax.fori_loop(0, tc, drain, 0)

    half = buf.shape[2]
    y_lo, y_hi = y[:, :half], y[:, half:]
    for k in range(TOP_K):
        wk = w_ref[:, k:k + 1]
        words = buf[k]
        y_lo = y_lo + pltpu.bitcast(words << 16, F32) * wk
        y_hi = y_hi + pltpu.bitcast(words & jnp.uint32(0xFFFF0000), F32) * wk
    o_ref[...] = x1_ref[...] + mod_ref[0, 5:6, :] * jnp.concatenate([y_lo, y_hi], axis=1)


def _combine(dest_flat, w, ys, h2p, x1, mod, wgs, wus, wds, seq):
    t, d = x1.shape
    half = d // 2
    tc = _tile(seq, 512)
    de = wgs.shape[1]
    per_batch = seq // tc
    const = lambda i: (0, 0)
    return pl.pallas_call(
        _combine_kernel,
        out_shape=jax.ShapeDtypeStruct((t, d), F32),
        grid=(t // tc,),
        in_specs=[
            pl.BlockSpec(memory_space=pl.ANY),
            pl.BlockSpec((tc, TOP_K), lambda i: (i, 0)),
            pl.BlockSpec(memory_space=pl.ANY),
            pl.BlockSpec((tc, half), lambda i: (i, 0)),
            pl.BlockSpec((tc, d), lambda i: (i, 0)),
            pl.BlockSpec((1, 6, d), lambda i: (i // per_batch, 0, 0)),
            pl.BlockSpec((d, de), const), pl.BlockSpec((d, de), const), pl.BlockSpec((de, d), const),
        ],
        out_specs=pl.BlockSpec((tc, d), lambda i: (i, 0)),
        scratch_shapes=[pltpu.SMEM((tc * TOP_K,), I32), pltpu.VMEM((TOP_K, tc, half), U32),
                        pltpu.SemaphoreType.DMA, pltpu.SemaphoreType.DMA],
        compiler_params=_cparams(("arbitrary",)),
        name="combine",
    )(dest_flat, w, ys, h2p, x1, mod, wgs, wus, wds)


def _rope_tables(n):
    rows = n // GRID_W
    row = jnp.broadcast_to(jnp.arange(rows)[:, None], (rows, GRID_W)).reshape(-1)
    col = jnp.broadcast_to(jnp.arange(GRID_W)[None, :], (rows, GRID_W)).reshape(-1)
    freqs = ROPE_BASE ** (-jnp.arange(ROPE_FREQS, dtype=F32) / ROPE_FREQS)
    pos = jnp.stack([row, col], axis=-1).astype(F32)
    ang = pos[:, :, None] * freqs
    cos, sin = jnp.cos(ang), jnp.sin(ang)
    cos64 = jnp.concatenate([cos[:, 0], cos[:, 0], cos[:, 1], cos[:, 1]], axis=-1)
    sin64 = jnp.concatenate([-sin[:, 0], sin[:, 0], -sin[:, 1], sin[:, 1]], axis=-1)
    reps = MXU_N // HEAD_DIM
    return jnp.tile(cos64, (1, reps)), jnp.tile(sin64, (1, reps))


def _gain_rows(qn_a, kn_a, qn_b, kn_b):
    reps = MXU_N // HEAD_DIM
    ones = jnp.ones((LANES,), F32)
    kb = jnp.tile(kn_b, reps)
    qa = jnp.tile(qn_a, reps) * ATTN_SCALE
    qb = jnp.tile(qn_b, reps) * (ATTN_SCALE * LOG2E)
    mixed = jnp.concatenate([jnp.tile(kn_a, LANES // HEAD_DIM), ones])
    lat_g = jnp.stack([kb] * 4 + [qa] * 4 + [qb] * 4 + [mixed])[:, None, :]
    ctx_g = jnp.stack([kb] * 4 + [mixed])[:, None, :]
    return lat_g, ctx_g


def kernel(x, c, ctx, c_ctx, w_ada, b_ada, norm1_g, norm2_g, w_in, qnorm_a, knorm_a, sink_a, qnorm_b, knorm_b,
           lam_q1, lam_k1, lam_q2, lam_k2, subln_g, w_pa, w_pb, w_o, w_router, router_bias, w_gate_e, w_up_e,
           w_down_e, w_gate_s, w_up_s, w_down_s):
    b, s, d = x.shape
    nctx = ctx.shape[1]
    assert w_ada.shape[0] == 1 and d == 1024 and s % BLOCK == 0 and s % GRID_W == 0
    t = b * s

    cc = jnp.concatenate([c, c_ctx[None, :], jnp.zeros((16 - b - 1, d), F32)], axis=0)
    mod_all = _ada(cc, w_ada[0], b_ada[0])
    mod = mod_all[:b].reshape(b, 6, d)
    mod_c = jnp.broadcast_to(mod_all[b].reshape(1, 6, d), (b, 6, d))

    w = w_in[0]
    ka_w, va_w = w[:, 0:128], w[:, 128:256]
    kb_w, vb_w = w[:, 256:1280], w[:, 1280:2304]
    qa_w, qb_w = w[:, 2304:3328], w[:, 3328:4352]
    gate_w = w[:, 4352:6400].astype(BF16)
    w_lat = jnp.concatenate([kb_w, qa_w, qb_w, ka_w, va_w, vb_w], axis=1).astype(BF16)
    w_ctx = jnp.concatenate([kb_w, ka_w, va_w, vb_w], axis=1).astype(BF16)

    lat_g, ctx_g = _gain_rows(qnorm_a[0], knorm_a[0], qnorm_b[0], knorm_b[0])
    hid = jnp.arange(MXU_N) // HEAD_DIM
    seg = jnp.where(hid[:, None] == hid[None, :], 1.0 / HEAD_DIM, 0.0).astype(BF16)
    cos_t, sin_t = _rope_tables(s)
    n1 = norm1_g[0].reshape(1, d)
    n2 = norm2_g[0].reshape(1, d)

    lat, vt = _proj(x, mod, n1, w_lat, lat_g, seg, cos_t, sin_t, nqk=LAT_NQK, use_rope=True)
    ctxp, vtx = _proj(ctx, mod_c, n1, w_ctx, ctx_g, seg, cos_t[:nctx], sin_t[:nctx], nqk=CTX_NQK, use_rope=False)

    oa = _attn_a(sink_a[0], lat, ctxp)
    lamv = jnp.stack([lam_q1[0], lam_k1[0], lam_q2[0], lam_k2[0]])
    bound = (HEAD_DIM * ATTN_SCALE * LOG2E * 1.01) * jnp.max(jnp.abs(qnorm_b[0])) * jnp.max(jnp.abs(knorm_b[0]))
    ob = _attn_b(bound.reshape(1), lamv, subln_g[0].reshape(LANES, 1), lat, vt, ctxp, vtx)

    x1, h2p, logits = _merge(x, mod, n1, n2, oa, ob, gate_w, w_pa[0].astype(BF16), w_pb[0].astype(BF16),
                             w_o[0].astype(BF16), w_router[0].astype(BF16))
    x1 = x1.reshape(t, d)
    h2p = h2p.reshape(t, d // 2)
    logits = logits.reshape(t, N_EXPERTS)

    idx_t, wts_t, rank_t, counts = _router(logits, router_bias[0])
    idx, wts, rank = idx_t.T, wts_t.T, rank_t.T

    cnt = counts[:, 0].astype(I32)
    padded = (cnt + MOE_BLOCK - 1) // MOE_BLOCK * MOE_BLOCK
    pend = jnp.cumsum(padded)
    pstart = pend - padded
    nblk = -(-(t * TOP_K) // MOE_BLOCK) + N_EXPERTS
    blk_row0 = jnp.arange(nblk, dtype=I32) * MOE_BLOCK
    blk_expert = jnp.minimum(jnp.sum((pend[None, :] <= blk_row0[:, None]).astype(I32), axis=1), N_EXPERTS - 1)
    blk_valid = jnp.clip(pstart[blk_expert] + cnt[blk_expert] - blk_row0, 0, MOE_BLOCK).astype(I32)

    dest = _dest(idx, rank, pstart.astype(F32).reshape(1, N_EXPERTS)).reshape(t * TOP_K)
    xs = _dispatch(dest, h2p, nblk * MOE_BLOCK)
    ys = _experts(blk_expert, blk_valid, xs, w_gate_e[0], w_up_e[0], w_down_e[0])
    out = _combine(dest, wts, ys, h2p, x1, mod, w_gate_s[0].astype(BF16), w_up_s[0].astype(BF16),
                   w_down_s[0].astype(BF16), s)
    return out.reshape(b, s, d)
```

```python
import functools
import math

import jax
import jax.numpy as jnp
from jax import lax
from jax.experimental import pallas as pl
from jax.experimental.pallas import tpu as pltpu

F32 = jnp.float32
BF16 = jnp.bfloat16
I32 = jnp.int32
U32 = jnp.uint32

HEAD_DIM = 64
GRID_W = 64
ROPE_FREQS = HEAD_DIM // 4
ROPE_BASE = 10000.0
EPS = 1e-6
ATTN_SCALE = HEAD_DIM ** -0.5
BLOCK = 128
A_Q_HEADS = 16
A_KV_HEADS = 2
A_GROUP = A_Q_HEADS // A_KV_HEADS
B_HEADS = 8
N_EXPERTS = 256
TOP_K = 8
N_GROUPS = 8
TOPK_GROUPS = 4
GROUP_SIZE = N_EXPERTS // N_GROUPS
ROUTED_SCALE = 2.5
MOE_BLOCK = 512
LAM_INIT = 0.8 - 0.6 * math.exp(-0.3 * 0)

LANES = 128
SUBLANES = 8
MXU_N = 256
VMEM_LIMIT = 56 * 1024 * 1024
NEG_BIG = -1e30
LOG2E = math.log2(math.e)
MAX_FIXED_SHIFT = 60.0

LAT_KB, LAT_QA, LAT_QB, LAT_KA, LAT_VA = 0, 8, 16, 24, 25
LAT_NQK = 13
CTX_KB, CTX_KA, CTX_VA = 0, 8, 9
CTX_NQK = 5


def _tile(n, pref):
    return pref if n % pref == 0 else n


def _cparams(sem):
    return pltpu.CompilerParams(dimension_semantics=sem, vmem_limit_bytes=VMEM_LIMIT)


def _nt_dot(a, b):
    return lax.dot_general(a, b, (((1,), (1,)), ((), ())), preferred_element_type=F32)


def _dot(a, b):
    return jnp.dot(a, b, preferred_element_type=F32)


def _silu(x):
    return x * (1.0 / (1.0 + jnp.exp(-x)))


def _sigmoid(x):
    return 1.0 / (1.0 + jnp.exp(-x))


def _modulated_norm(xf, g_row, shift_row, scale_row):
    ms = jnp.mean(xf * xf, axis=-1, keepdims=True)
    y = xf * lax.rsqrt(ms + EPS) * g_row
    return y * (1.0 + scale_row) + shift_row


def _ada_kernel(c_ref, w_ref, b_ref, o_ref):
    o_ref[...] = _dot(_silu(c_ref[...]).astype(BF16), w_ref[...].astype(BF16)) + b_ref[...]


def _ada(cc, w_ada, b_ada):
    rows, d = cc.shape
    n = w_ada.shape[1]
    tn = _tile(n, 512)
    return pl.pallas_call(
        _ada_kernel,
        out_shape=jax.ShapeDtypeStruct((rows, n), F32),
        grid=(n // tn,),
        in_specs=[pl.BlockSpec((rows, d), lambda j: (0, 0)),
                  pl.BlockSpec((d, tn), lambda j: (0, j)),
                  pl.BlockSpec((1, tn), lambda j: (0, j))],
        out_specs=pl.BlockSpec((rows, tn), lambda j: (0, j)),
        compiler_params=_cparams(("parallel",)),
        name="ada",
    )(cc, w_ada, b_ada.reshape(1, n))


def _proj_kernel(x_ref, mod_ref, n1_ref, w_ref, gain_ref, seg_ref, cos_ref, sin_ref, o_ref, vt_ref,
                 *, nqk, use_rope):
    nj = w_ref.shape[1] // MXU_N
    h = _modulated_norm(x_ref[0], n1_ref[...], mod_ref[0, 0:1, :], mod_ref[0, 1:2, :]).astype(BF16)
    lane = lax.broadcasted_iota(I32, (1, MXU_N), 1)
    first = (lane % (2 * ROPE_FREQS)) < ROPE_FREQS

    def matmul(j):
        return _dot(h, w_ref[:, j * MXU_N:(j + 1) * MXU_N])

    def finish(j, acc):
        if j >= nqk:
            vt_ref[0, (j - nqk) * MXU_N:(j - nqk + 1) * MXU_N, :] = acc.T.astype(BF16)
            return
        ms = _dot((acc * acc).astype(BF16), seg_ref[...])
        y = acc * lax.rsqrt(ms + EPS) * gain_ref[j]
        if use_rope:
            rot = jnp.where(first, pltpu.roll(y, MXU_N - ROPE_FREQS, 1), pltpu.roll(y, ROPE_FREQS, 1))
            y = y * cos_ref[...] + rot * sin_ref[...]
        if j == nqk - 1:
            y = jnp.where(lane < LANES, y, acc)
        o_ref[0, :, j * MXU_N:(j + 1) * MXU_N] = y.astype(BF16)

    acc = matmul(0)
    for j in range(nj):
        nxt = matmul(j + 1) if j + 1 < nj else None
        finish(j, acc)
        acc = nxt


def _proj(x, mod, n1, w, gains, seg, cos_t, sin_t, *, nqk, use_rope):
    b, s, d = x.shape
    ncols = w.shape[1]
    tm = _tile(s, 512)
    nj = ncols // MXU_N
    const2 = lambda bi, si: (0, 0)
    return pl.pallas_call(
        functools.partial(_proj_kernel, nqk=nqk, use_rope=use_rope),
        out_shape=(jax.ShapeDtypeStruct((b, s, nqk * MXU_N), BF16),
                   jax.ShapeDtypeStruct((b, (nj - nqk) * MXU_N, s), BF16)),
        grid=(b, s // tm),
        in_specs=[
            pl.BlockSpec((1, tm, d), lambda bi, si: (bi, si, 0)),
            pl.BlockSpec((1, 6, d), lambda bi, si: (bi, 0, 0)),
            pl.BlockSpec((1, d), const2),
            pl.BlockSpec((d, ncols), const2),
            pl.BlockSpec((nqk, 1, MXU_N), lambda bi, si: (0, 0, 0)),
            pl.BlockSpec((MXU_N, MXU_N), const2),
            pl.BlockSpec((tm, MXU_N), lambda bi, si: (si, 0)),
            pl.BlockSpec((tm, MXU_N), lambda bi, si: (si, 0)),
        ],
        out_specs=(pl.BlockSpec((1, tm, nqk * MXU_N), lambda bi, si: (bi, si, 0)),
                   pl.BlockSpec((1, (nj - nqk) * MXU_N, tm), lambda bi, si: (bi, 0, si))),
        compiler_params=_cparams(("parallel", "parallel")),
        name="proj_rope" if use_rope else "proj_ctx",
    )(x, mod, n1, w, gains, seg, cos_t, sin_t)


def _attn_a_kernel(sink_ref, q_ref, kp_ref, kc_ref, kn_ref, vp_ref, vc_ref, vn_ref, kx_ref, vx_ref, o_ref):
    i = pl.program_id(1)
    nb = pl.num_programs(1)
    nctx = kx_ref.shape[1]
    span = 3 * BLOCK + nctx
    kall = jnp.concatenate([kp_ref[0], kc_ref[0], kn_ref[0], kx_ref[0]], axis=0)
    vall = jnp.concatenate([vp_ref[0], vc_ref[0], vn_ref[0], vx_ref[0]], axis=0)

    r = lax.broadcasted_iota(I32, (BLOCK, span), 0)
    c = lax.broadcasted_iota(I32, (BLOCK, span), 1)
    prev_ok = (c < BLOCK) & (c >= r) & (i > 0)
    cur_ok = (c >= BLOCK) & (c < 2 * BLOCK)
    next_ok = (c >= 2 * BLOCK) & (c < 3 * BLOCK) & (c - 2 * BLOCK <= r) & (i < nb - 1)
    valid = prev_ok | cur_ok | next_ok | (c >= 3 * BLOCK)
    bias = jnp.where(valid, 0.0, NEG_BIG)

    lane = lax.broadcasted_iota(I32, (BLOCK, LANES), 1)
    for kv in range(A_KV_HEADS):
        in_half = (lane >= kv * HEAD_DIM) & (lane < (kv + 1) * HEAD_DIM)
        qs = []
        for g in range(A_GROUP):
            hh = kv * A_GROUP + g
            blk = q_ref[0, :, (hh // 2) * LANES:(hh // 2 + 1) * LANES].astype(F32)
            if hh % 2 != kv:
                blk = pltpu.roll(blk, HEAD_DIM, 1)
            qs.append(jnp.where(in_half, blk, 0.0).astype(BF16))
        s_all = _nt_dot(jnp.concatenate(qs, axis=0), kall)
        ps, ls = [], []
        for g in range(A_GROUP):
            sink = sink_ref[kv, g]
            s = s_all[g * BLOCK:(g + 1) * BLOCK] + bias
            m = jnp.maximum(jnp.max(s, axis=-1, keepdims=True), sink)
            p = jnp.exp(s - m)
            ls.append(jnp.sum(p, axis=-1, keepdims=True) + jnp.exp(sink - m))
            ps.append(p.astype(BF16))
        o_all = _dot(jnp.concatenate(ps, axis=0), vall)
        os_ = [o_all[g * BLOCK:(g + 1) * BLOCK] / ls[g] for g in range(A_GROUP)]
        for pair in range(A_GROUP // 2):
            o0, o1 = os_[2 * pair], os_[2 * pair + 1]
            if kv == 0:
                o1 = pltpu.roll(o1, HEAD_DIM, 1)
            else:
                o0 = pltpu.roll(o0, HEAD_DIM, 1)
            col = (kv * (A_GROUP // 2) + pair) * LANES
            o_ref[0, :, col:col + LANES] = jnp.where(lane < HEAD_DIM, o0, o1).astype(BF16)


def _attn_a(sink, lat, ctxp):
    b, s, _ = lat.shape
    nctx = ctxp.shape[1]
    nb = s // BLOCK
    width = A_Q_HEADS * HEAD_DIM

    def kspec(col, shift):
        return pl.BlockSpec((1, BLOCK, LANES),
                            lambda bi, i: (bi, jnp.clip(i + shift, 0, nb - 1), col))

    return pl.pallas_call(
        _attn_a_kernel,
        out_shape=jax.ShapeDtypeStruct((b, s, width), BF16),
        grid=(b, nb),
        in_specs=[
            pl.BlockSpec(memory_space=pltpu.SMEM),
            pl.BlockSpec((1, BLOCK, width), lambda bi, i: (bi, i, LAT_QA * LANES // width)),
            kspec(LAT_KA, -1), kspec(LAT_KA, 0), kspec(LAT_KA, 1),
            kspec(LAT_VA, -1), kspec(LAT_VA, 0), kspec(LAT_VA, 1),
            pl.BlockSpec((1, nctx, LANES), lambda bi, i: (bi, 0, CTX_KA)),
            pl.BlockSpec((1, nctx, LANES), lambda bi, i: (bi, 0, CTX_VA)),
        ],
        out_specs=pl.BlockSpec((1, BLOCK, width), lambda bi, i: (bi, i, 0)),
        compiler_params=_cparams(("parallel", "parallel")),
        name="attn_a",
    )(sink, lat, lat, lat, lat, lat, lat, lat, ctxp, ctxp)


def _attn_b_kernel(bound_ref, lamv_ref, sub_ref, q_ref, k_ref, vt_ref, kx_ref, vtx_ref, o_ref, acc_ref, l_ref,
                   s0_ref, s1_ref, sx_ref, *, tk):
    tq = q_ref.shape[1]
    s_len = k_ref.shape[1]
    qt = q_ref[0].astype(F32).T
    row = lax.broadcasted_iota(I32, (LANES, tq), 0)
    qts = [jnp.where(row < HEAD_DIM, qt, 0.0).astype(BF16), jnp.where(row >= HEAD_DIM, qt, 0.0).astype(BF16)]
    acc_ref[...] = jnp.zeros(acc_ref.shape, F32)
    n = s_len // tk
    bound = bound_ref[0]

    def kchunk(i):
        return k_ref[0, pl.ds(pl.multiple_of(i * tk, tk), tk), :]

    def vchunk(i):
        return vt_ref[0, :, pl.ds(pl.multiple_of(i * tk, tk), tk)]

    def run(fixed):
        def scores(j, kc, s_ref):
            s = _dot(kc, qts[j])
            s_ref[j] = s
            return jnp.zeros((1, tq), F32) if fixed else jnp.max(s, axis=0, keepdims=True)

        def accumulate(j, s_ref, cm, vtc, stat):
            m_old, l_old = stat
            if fixed:
                p = jnp.exp2(s_ref[j] - bound)
                acc_ref[j] += _dot(vtc, p.astype(BF16))
                return m_old, l_old + jnp.sum(p, axis=0, keepdims=True)
            m_new = jnp.maximum(m_old, cm)
            alpha = jnp.exp2(m_old - m_new)
            p = jnp.exp2(s_ref[j] - m_new)
            acc_ref[j] = alpha * acc_ref[j] + _dot(vtc, p.astype(BF16))
            return m_new, alpha * l_old + jnp.sum(p, axis=0, keepdims=True)

        def half_step(k_next, s_next, s_cur, cms, vtc, stats):
            new_cms, new_stats = [], []
            for j in range(2):
                new_cms.append(scores(j, k_next, s_next))
                new_stats.append(accumulate(j, s_cur, cms[j], vtc, stats[j]))
            return tuple(new_cms), tuple(new_stats)

        m0 = jnp.zeros((1, tq), F32) if fixed else jnp.full((1, tq), NEG_BIG, F32)
        stats = ((m0, jnp.zeros((1, tq), F32)),) * 2
        cmx = tuple(scores(j, kx_ref[0], sx_ref) for j in range(2))
        cms, stats = half_step(kchunk(0), s0_ref, sx_ref, cmx, vtx_ref[0], stats)

        def body(pair, carry):
            cms, stats = carry
            i = 2 * pair
            cms, stats = half_step(kchunk(i + 1), s1_ref, s0_ref, cms, vchunk(i), stats)
            return half_step(kchunk(i + 2), s0_ref, s1_ref, cms, vchunk(i + 1), stats)

        cms, stats = lax.fori_loop(0, n // 2 - 1, body, (cms, stats))
        cms, stats = half_step(kchunk(n - 1), s1_ref, s0_ref, cms, vchunk(n - 2), stats)
        for j in range(2):
            l_ref[j] = accumulate(j, s1_ref, cms[j], vchunk(n - 1), stats[j])[1]

    pl.when(bound <= MAX_FIXED_SHIFT)(lambda: run(True))
    pl.when(bound > MAX_FIXED_SHIFT)(lambda: run(False))


    lv = lamv_ref[...]
    lam = (jnp.exp(jnp.sum(lv[0:1] * lv[1:2], axis=-1, keepdims=True))
           - jnp.exp(jnp.sum(lv[2:3] * lv[3:4], axis=-1, keepdims=True)) + LAM_INIT)
    o = acc_ref[0] * (1.0 / l_ref[0]) - acc_ref[1] * (lam / l_ref[1])
    ms = jnp.mean(o * o, axis=0, keepdims=True)
    o = o * lax.rsqrt(ms + EPS) * (sub_ref[...] * (1.0 - LAM_INIT))
    o_ref[0] = o.T.astype(BF16)


def _attn_b(bound, lamv, subln_col, lat, vt, ctxp, vtx):
    b, s, _ = lat.shape
    nctx = ctxp.shape[1]
    tq = _tile(s, 512)
    tk = min(512, s // 2)
    assert s % (2 * tk) == 0
    return pl.pallas_call(
        functools.partial(_attn_b_kernel, tk=tk),
        out_shape=jax.ShapeDtypeStruct((b, s, B_HEADS * LANES), BF16),
        grid=(b, B_HEADS, s // tq),
        in_specs=[
            pl.BlockSpec(memory_space=pltpu.SMEM),
            pl.BlockSpec((4, HEAD_DIM), lambda bi, h, qi: (0, 0)),
            pl.BlockSpec((LANES, 1), lambda bi, h, qi: (0, 0)),
            pl.BlockSpec((1, tq, LANES), lambda bi, h, qi: (bi, qi, LAT_QB + h)),
            pl.BlockSpec((1, s, LANES), lambda bi, h, qi: (bi, 0, LAT_KB + h)),
            pl.BlockSpec((1, LANES, s), lambda bi, h, qi: (bi, h, 0)),
            pl.BlockSpec((1, nctx, LANES), lambda bi, h, qi: (bi, 0, CTX_KB + h)),
            pl.BlockSpec((1, LANES, nctx), lambda bi, h, qi: (bi, h, 0)),
        ],
        out_specs=pl.BlockSpec((1, tq, LANES), lambda bi, h, qi: (bi, qi, h)),
        scratch_shapes=[pltpu.VMEM((2, LANES, tq), F32), pltpu.VMEM((2, 1, tq), F32),
                        pltpu.VMEM((2, tk, tq), F32), pltpu.VMEM((2, tk, tq), F32),
                        pltpu.VMEM((2, nctx, tq), F32)],
        compiler_params=_cparams(("parallel", "parallel", "arbitrary")),
        name="attn_b",
    )(bound, lamv, subln_col, lat, lat, vt, ctxp, vtx)


def _pack_rows(h):
    n = h.shape[1] // 2
    bits = pltpu.bitcast(h.astype(BF16).astype(F32), U32)
    return (bits[:, :n] >> 16) | (bits[:, n:] & jnp.uint32(0xFFFF0000))


def _unpack_rows(w):
    lo = pltpu.bitcast(w << 16, F32)
    hi = pltpu.bitcast(w & jnp.uint32(0xFFFF0000), F32)
    return jnp.concatenate([lo, hi], axis=1).astype(BF16)


def _merge_kernel(x_ref, mod_ref, n1_ref, n2_ref, oa_ref, ob_ref, wg_ref, wpa_ref, wpb_ref, wo_ref, wr_ref,
                  x1_ref, h2_ref, lg_ref):
    d = x_ref.shape[2]
    xf = x_ref[0]
    h = _modulated_norm(xf, n1_ref[...], mod_ref[0, 0:1, :], mod_ref[0, 1:2, :]).astype(BF16)
    gates = _sigmoid(_dot(h, wg_ref[...]))
    ya = _dot(oa_ref[0], wpa_ref[...])
    yb = _dot(ob_ref[0], wpb_ref[...])
    z = gates[:, :d] * ya + gates[:, d:] * yb
    x1 = xf + mod_ref[0, 2:3, :] * _dot(z.astype(BF16), wo_ref[...])
    x1_ref[0] = x1
    h2 = _modulated_norm(x1, n2_ref[...], mod_ref[0, 3:4, :], mod_ref[0, 4:5, :])
    lg_ref[0] = _dot(h2.astype(BF16), wr_ref[...])
    h2_ref[0] = _pack_rows(h2)


def _merge(x, mod, n1, n2, oa, ob, wg, wpa, wpb, wo, wr):
    b, s, d = x.shape
    tm = _tile(s, 256)
    ne = wr.shape[1]
    const = lambda bi, si: (0, 0)
    row = lambda bi, si: (bi, si, 0)
    return pl.pallas_call(
        _merge_kernel,
        out_shape=(jax.ShapeDtypeStruct((b, s, d), F32),
                   jax.ShapeDtypeStruct((b, s, d // 2), U32),
                   jax.ShapeDtypeStruct((b, s, ne), F32)),
        grid=(b, s // tm),
        in_specs=[
            pl.BlockSpec((1, tm, d), row),
            pl.BlockSpec((1, 6, d), lambda bi, si: (bi, 0, 0)),
            pl.BlockSpec((1, d), const), pl.BlockSpec((1, d), const),
            pl.BlockSpec((1, tm, d), row), pl.BlockSpec((1, tm, d), row),
            pl.BlockSpec((d, 2 * d), const), pl.BlockSpec((d, d), const),
            pl.BlockSpec((d, d), const), pl.BlockSpec((d, d), const),
            pl.BlockSpec((d, ne), const),
        ],
        out_specs=(pl.BlockSpec((1, tm, d), row), pl.BlockSpec((1, tm, d // 2), row),
                   pl.BlockSpec((1, tm, ne), row)),
        compiler_params=_cparams(("parallel", "parallel")),
        name="merge",
    )(x, mod, n1, n2, oa, ob, wg, wpa, wpb, wo, wr)


def _router_kernel(lg_ref, bias_ref, idx_ref, w_ref, rank_ref, cnt_ref, run_ref):
    i = pl.program_id(0)
    tr = lg_ref.shape[0]

    @pl.when(i == 0)
    def _():
        run_ref[...] = jnp.zeros(run_ref.shape, F32)

    scores = _sigmoid(lg_ref[...].T)
    biased = scores + bias_ref[...]
    row = lax.broadcasted_iota(I32, scores.shape, 0)
    neg_inf = -jnp.inf

    def first_argmax(vals, rows):
        m = jnp.max(vals, axis=0, keepdims=True)
        idx = jnp.min(jnp.where(vals == m, rows, N_EXPERTS), axis=0, keepdims=True)
        return m, idx

    def group(a, g):
        return a[g * GROUP_SIZE:(g + 1) * GROUP_SIZE]

    gscore = []
    for g in range(N_GROUPS):
        vals = group(biased, g)
        rows = lax.broadcasted_iota(I32, vals.shape, 0) + g * GROUP_SIZE
        m1, i1 = first_argmax(vals, rows)
        m2 = jnp.max(jnp.where(rows == i1, neg_inf, vals), axis=0, keepdims=True)
        gscore.append(m1 + m2)
    parts = []
    for g in range(N_GROUPS):
        beaten = jnp.zeros((1, tr), F32)
        for g2 in range(N_GROUPS):
            if g2 == g:
                continue
            wins = (gscore[g2] > gscore[g]) | ((gscore[g2] == gscore[g]) & (g2 < g))
            beaten = beaten + jnp.where(wins, 1.0, 0.0)
        parts.append(jnp.where(beaten < TOPK_GROUPS, group(biased, g), neg_inf))
    masked = jnp.concatenate(parts, axis=0)

    sel = jnp.zeros(scores.shape, F32)
    idxs, ws = [], []
    for _ in range(TOP_K):
        _, ik = first_argmax(masked, row)
        hit = row == ik
        idxs.append(ik)
        ws.append(jnp.sum(jnp.where(hit, scores, 0.0), axis=0, keepdims=True))
        sel = jnp.where(hit, 1.0, sel)
        masked = jnp.where(hit, neg_inf, masked)
    wsum = ws[0]
    for k in range(1, TOP_K):
        wsum = wsum + ws[k]

    rr = lax.broadcasted_iota(I32, (tr, tr), 0)
    cc = lax.broadcasted_iota(I32, (tr, tr), 1)
    earlier = jnp.where(rr < cc, 1.0, 0.0).astype(BF16)
    run = run_ref[...]
    selb = sel.astype(BF16)
    before = _dot(selb, earlier) + run
    ranks = [jnp.sum(jnp.where(row == idxs[k], before, 0.0), axis=0, keepdims=True) for k in range(TOP_K)]
    run_ref[...] = run + _dot(selb, jnp.ones((tr, tr), BF16))
    cnt_ref[...] = run_ref[...]

    def stack(rows_):
        slot = lax.broadcasted_iota(I32, (TOP_K, tr), 0)
        out = jnp.broadcast_to(rows_[0], (TOP_K, tr))
        for k in range(1, TOP_K):
            out = jnp.where(slot == k, rows_[k], out)
        return out

    idx_ref[...] = stack(idxs)
    w_ref[...] = stack([w / wsum * ROUTED_SCALE for w in ws])
    rank_ref[...] = stack(ranks).astype(I32)


def _router(logits, bias):
    t, ne = logits.shape
    tr = _tile(t, 512)
    bias_b = jnp.broadcast_to(bias[:, None], (ne, tr))
    return pl.pallas_call(
        _router_kernel,
        out_shape=(jax.ShapeDtypeStruct((TOP_K, t), I32), jax.ShapeDtypeStruct((TOP_K, t), F32),
                   jax.ShapeDtypeStruct((TOP_K, t), I32), jax.ShapeDtypeStruct((ne, tr), F32)),
        grid=(t // tr,),
        in_specs=[pl.BlockSpec((tr, ne), lambda i: (i, 0)), pl.BlockSpec((ne, tr), lambda i: (0, 0))],
        out_specs=(pl.BlockSpec((TOP_K, tr), lambda i: (0, i)), pl.BlockSpec((TOP_K, tr), lambda i: (0, i)),
                   pl.BlockSpec((TOP_K, tr), lambda i: (0, i)), pl.BlockSpec((ne, tr), lambda i: (0, 0))),
        scratch_shapes=[pltpu.VMEM((ne, tr), F32)],
        compiler_params=_cparams(("arbitrary",)),
        name="router",
    )(logits, bias_b)


def _dest_kernel(idx_ref, rank_ref, pstart_ref, dest_ref):
    idx = idx_ref[...]
    lane = lax.broadcasted_iota(I32, (idx.shape[0], N_EXPERTS), 1)
    cols = []
    for k in range(TOP_K):
        start = jnp.sum(jnp.where(lane == idx[:, k:k + 1], pstart_ref[...], 0.0), axis=-1, keepdims=True)
        cols.append(start.astype(I32) + rank_ref[:, k:k + 1])
    dest_ref[...] = jnp.concatenate(cols, axis=1)


def _dest(idx, rank, pstart):
    t = idx.shape[0]
    tr = _tile(t, 512)
    return pl.pallas_call(
        _dest_kernel,
        out_shape=jax.ShapeDtypeStruct((t, TOP_K), I32),
        grid=(t // tr,),
        in_specs=[pl.BlockSpec((tr, TOP_K), lambda i: (i, 0)), pl.BlockSpec((tr, TOP_K), lambda i: (i, 0)),
                  pl.BlockSpec((1, N_EXPERTS), lambda i: (0, 0))],
        out_specs=pl.BlockSpec((tr, TOP_K), lambda i: (i, 0)),
        compiler_params=_cparams(("parallel",)),
        name="dest",
    )(idx, rank, pstart)


def _dispatch_kernel(dest_hbm, h2_ref, xs_out, dest_smem, sem_idx, sem_rows):
    i = pl.program_id(0)
    td = h2_ref.shape[0]
    n = td * TOP_K
    cp = pltpu.make_async_copy(dest_hbm.at[pl.ds(i * n, n)], dest_smem, sem_idx)
    cp.start()
    cp.wait()

    def row_copy(t, d):
        return pltpu.make_async_copy(h2_ref.at[pl.ds(t, 1), :], xs_out.at[pl.ds(d, 1), :], sem_rows)

    def issue(g, carry):
        base = pl.multiple_of(g * SUBLANES, SUBLANES)
        for r in range(SUBLANES):
            for k in range(TOP_K):
                row_copy(base + r, dest_smem[(g * SUBLANES + r) * TOP_K + k]).start(priority=k % 2)
        return carry

    lax.fori_loop(0, td // SUBLANES, issue, 0)

    def drain(t, carry):
        for k in range(TOP_K):
            row_copy(0, 0).wait()
        return carry

    lax.fori_loop(0, td, drain, 0)


def _dispatch(dest_flat, h2p, nrows):
    t, half = h2p.shape
    td = _tile(t, 1024)
    return pl.pallas_call(
        _dispatch_kernel,
        out_shape=jax.ShapeDtypeStruct((nrows, half), U32),
        grid=(t // td,),
        in_specs=[pl.BlockSpec(memory_space=pl.ANY),
                  pl.BlockSpec((td, half), lambda i: (i, 0))],
        out_specs=pl.BlockSpec(memory_space=pl.ANY),
        scratch_shapes=[pltpu.SMEM((td * TOP_K,), I32), pltpu.SemaphoreType.DMA, pltpu.SemaphoreType.DMA],
        compiler_params=_cparams(("arbitrary",)),
        name="dispatch",
    )(dest_flat, h2p)


def _experts_kernel(be_ref, valid_ref, xs_ref, wg_ref, wu_ref, wd_ref, ys_ref, wgb_ref, wub_ref, wdb_ref):
    blk = pl.program_id(0)
    valid = valid_ref[blk]
    used = valid > 0

    @pl.when(used & ((blk == 0) | (be_ref[blk] != be_ref[jnp.maximum(blk - 1, 0)])))
    def _():
        wgb_ref[...] = wg_ref[0].astype(BF16)
        wub_ref[...] = wu_ref[0].astype(BF16)
        wdb_ref[...] = wd_ref[0].astype(BF16)

    @pl.when(used)
    def _():
        rows = lax.broadcasted_iota(I32, (xs_ref.shape[0], 1), 0)
        x = _unpack_rows(jnp.where(rows < valid, xs_ref[...], jnp.uint32(0)))
        g = _dot(x, wgb_ref[...])
        u = _dot(x, wub_ref[...])
        a = (_silu(g) * u).astype(BF16)
        ys_ref[...] = _pack_rows(_dot(a, wdb_ref[...]))

    @pl.when(jnp.logical_not(used))
    def _():
        ys_ref[...] = jnp.zeros(ys_ref.shape, U32)


def _experts(blk_expert, blk_valid, xs, wg, wu, wd):
    p, half = xs.shape
    d = 2 * half
    de = wg.shape[2]
    nblk = p // MOE_BLOCK
    grid_spec = pltpu.PrefetchScalarGridSpec(
        num_scalar_prefetch=2,
        grid=(nblk,),
        in_specs=[
            pl.BlockSpec((MOE_BLOCK, half), lambda i, be, nu: (i, 0)),
            pl.BlockSpec((1, d, de), lambda i, be, nu: (be[i], 0, 0)),
            pl.BlockSpec((1, d, de), lambda i, be, nu: (be[i], 0, 0)),
            pl.BlockSpec((1, de, d), lambda i, be, nu: (be[i], 0, 0)),
        ],
        out_specs=pl.BlockSpec((MOE_BLOCK, half), lambda i, be, nu: (i, 0)),
        scratch_shapes=[pltpu.VMEM((d, de), BF16), pltpu.VMEM((d, de), BF16), pltpu.VMEM((de, d), BF16)],
    )
    return pl.pallas_call(
        _experts_kernel,
        out_shape=jax.ShapeDtypeStruct((p, half), U32),
        grid_spec=grid_spec,
        compiler_params=_cparams(("arbitrary",)),
        name="experts",
    )(blk_expert, blk_valid, xs, wg, wu, wd)


def _combine_kernel(dest_hbm, w_ref, ys_hbm, h2_ref, x1_ref, mod_ref, wgs_ref, wus_ref, wds_ref, o_ref,
                    dest_smem, buf, sem_idx, sem_rows):
    i = pl.program_id(0)
    nsteps = pl.num_programs(0)
    tc = h2_ref.shape[0]
    n = tc * TOP_K
    slot = i % 2
    nxt = 1 - slot

    def idx_copy(step, s):
        return pltpu.make_async_copy(dest_hbm.at[pl.ds(step * n, n)], dest_smem.at[pl.ds(s * n, n)], sem_idx.at[s])

    def row_copy(s, t, k, d):
        return pltpu.make_async_copy(ys_hbm.at[pl.ds(d, 1), :], buf.at[s, k, pl.ds(t, 1), :], sem_rows.at[s])

    def issue_rows(s):
        def issue(g, carry):
            base = pl.multiple_of(g * SUBLANES, SUBLANES)
            for r in range(SUBLANES):
                for k in range(TOP_K):
                    d = dest_smem[s * n + (g * SUBLANES + r) * TOP_K + k]
                    row_copy(s, base + r, k, d).start(priority=k % 2)
            return carry

        lax.fori_loop(0, tc // SUBLANES, issue, 0)

    @pl.when(i == 0)
    def _():
        idx_copy(0, 0).start()
        idx_copy(0, 0).wait()
        issue_rows(0)

    @pl.when(i + 1 < nsteps)
    def _():
        idx_copy(i + 1, nxt).start()

    x = _unpack_rows(h2_ref[...])
    a = (_silu(_dot(x, wgs_ref[...])) * _dot(x, wus_ref[...])).astype(BF16)
    y = _dot(a, wds_ref[...])

    @pl.when(i + 1 < nsteps)
    def _():
        idx_copy(i + 1, nxt).wait()
        issue_rows(nxt)

    def drain(t, carry):
        for k in range(TOP_K):
            row_copy(slot, 0, k, 0).wait()
        return carry

    lax.fori_loop(0, tc, drain, 0)

    half = buf.shape[3]
    y_lo, y_hi = y[:, :half], y[:, half:]
    for k in range(TOP_K):
        wk = w_ref[:, k:k + 1]
        words = buf[slot, k]
        y_lo = y_lo + pltpu.bitcast(words << 16, F32) * wk
        y_hi = y_hi + pltpu.bitcast(words & jnp.uint32(0xFFFF0000), F32) * wk
    o_ref[...] = x1_ref[...] + mod_ref[0, 5:6, :] * jnp.concatenate([y_lo, y_hi], axis=1)


def _combine(dest_flat, w, ys, h2p, x1, mod, wgs, wus, wds, seq):
    t, d = x1.shape
    half = d // 2
    tc = _tile(seq, 512)
    de = wgs.shape[1]
    per_batch = seq // tc
    const = lambda i: (0, 0)
    return pl.pallas_call(
        _combine_kernel,
        out_shape=jax.ShapeDtypeStruct((t, d), F32),
        grid=(t // tc,),
        in_specs=[
            pl.BlockSpec(memory_space=pl.ANY),
            pl.BlockSpec((tc, TOP_K), lambda i: (i, 0)),
            pl.BlockSpec(memory_space=pl.ANY),
            pl.BlockSpec((tc, half), lambda i: (i, 0)),
            pl.BlockSpec((tc, d), lambda i: (i, 0)),
            pl.BlockSpec((1, 6, d), lambda i: (i // per_batch, 0, 0)),
            pl.BlockSpec((d, de), const), pl.BlockSpec((d, de), const), pl.BlockSpec((de, d), const),
        ],
        out_specs=pl.BlockSpec((tc, d), lambda i: (i, 0)),
        scratch_shapes=[pltpu.SMEM((2 * tc * TOP_K,), I32), pltpu.VMEM((2, TOP_K, tc, half), U32),
                        pltpu.SemaphoreType.DMA((2,)), pltpu.SemaphoreType.DMA((2,))],
        compiler_params=_cparams(("arbitrary",)),
        name="combine",
    )(dest_flat, w, ys, h2p, x1, mod, wgs, wus, wds)


def _rope_tables(n):
    rows = n // GRID_W
    row = jnp.broadcast_to(jnp.arange(rows)[:, None], (rows, GRID_W)).reshape(-1)
    col = jnp.broadcast_to(jnp.arange(GRID_W)[None, :], (rows, GRID_W)).reshape(-1)
    freqs = ROPE_BASE ** (-jnp.arange(ROPE_FREQS, dtype=F32) / ROPE_FREQS)
    pos = jnp.stack([row, col], axis=-1).astype(F32)
    ang = pos[:, :, None] * freqs
    cos, sin = jnp.cos(ang), jnp.sin(ang)
    cos64 = jnp.concatenate([cos[:, 0], cos[:, 0], cos[:, 1], cos[:, 1]], axis=-1)
    sin64 = jnp.concatenate([-sin[:, 0], sin[:, 0], -sin[:, 1], sin[:, 1]], axis=-1)
    reps = MXU_N // HEAD_DIM
    return jnp.tile(cos64, (1, reps)), jnp.tile(sin64, (1, reps))


def _gain_rows(qn_a, kn_a, qn_b, kn_b):
    reps = MXU_N // HEAD_DIM
    ones = jnp.ones((LANES,), F32)
    kb = jnp.tile(kn_b, reps)
    qa = jnp.tile(qn_a, reps) * ATTN_SCALE
    qb = jnp.tile(qn_b, reps) * (ATTN_SCALE * LOG2E)
    mixed = jnp.concatenate([jnp.tile(kn_a, LANES // HEAD_DIM), ones])
    lat_g = jnp.stack([kb] * 4 + [qa] * 4 + [qb] * 4 + [mixed])[:, None, :]
    ctx_g = jnp.stack([kb] * 4 + [mixed])[:, None, :]
    return lat_g, ctx_g


def kernel(x, c, ctx, c_ctx, w_ada, b_ada, norm1_g, norm2_g, w_in, qnorm_a, knorm_a, sink_a, qnorm_b, knorm_b,
           lam_q1, lam_k1, lam_q2, lam_k2, subln_g, w_pa, w_pb, w_o, w_router, router_bias, w_gate_e, w_up_e,
           w_down_e, w_gate_s, w_up_s, w_down_s):
    b, s, d = x.shape
    nctx = ctx.shape[1]
    assert w_ada.shape[0] == 1 and d == 1024 and s % BLOCK == 0 and s % GRID_W == 0
    t = b * s

    cc = jnp.concatenate([c, c_ctx[None, :], jnp.zeros((16 - b - 1, d), F32)], axis=0)
    mod_all = _ada(cc, w_ada[0], b_ada[0])
    mod = mod_all[:b].reshape(b, 6, d)
    mod_c = jnp.broadcast_to(mod_all[b].reshape(1, 6, d), (b, 6, d))

    w = w_in[0]
    ka_w, va_w = w[:, 0:128], w[:, 128:256]
    kb_w, vb_w = w[:, 256:1280], w[:, 1280:2304]
    qa_w, qb_w = w[:, 2304:3328], w[:, 3328:4352]
    gate_w = w[:, 4352:6400].astype(BF16)
    w_lat = jnp.concatenate([kb_w, qa_w, qb_w, ka_w, va_w, vb_w], axis=1).astype(BF16)
    w_ctx = jnp.concatenate([kb_w, ka_w, va_w, vb_w], axis=1).astype(BF16)

    lat_g, ctx_g = _gain_rows(qnorm_a[0], knorm_a[0], qnorm_b[0], knorm_b[0])
    hid = jnp.arange(MXU_N) // HEAD_DIM
    seg = jnp.where(hid[:, None] == hid[None, :], 1.0 / HEAD_DIM, 0.0).astype(BF16)
    cos_t, sin_t = _rope_tables(s)
    n1 = norm1_g[0].reshape(1, d)
    n2 = norm2_g[0].reshape(1, d)

    lat, vt = _proj(x, mod, n1, w_lat, lat_g, seg, cos_t, sin_t, nqk=LAT_NQK, use_rope=True)
    ctxp, vtx = _proj(ctx, mod_c, n1, w_ctx, ctx_g, seg, cos_t[:nctx], sin_t[:nctx], nqk=CTX_NQK, use_rope=False)

    oa = _attn_a(sink_a[0], lat, ctxp)
    lamv = jnp.stack([lam_q1[0], lam_k1[0], lam_q2[0], lam_k2[0]])
    bound = (HEAD_DIM * ATTN_SCALE * LOG2E * 1.01) * jnp.max(jnp.abs(qnorm_b[0])) * jnp.max(jnp.abs(knorm_b[0]))
    ob = _attn_b(bound.reshape(1), lamv, subln_g[0].reshape(LANES, 1), lat, vt, ctxp, vtx)

    x1, h2p, logits = _merge(x, mod, n1, n2, oa, ob, gate_w, w_pa[0].astype(BF16), w_pb[0].astype(BF16),
                             w_o[0].astype(BF16), w_router[0].astype(BF16))
    x1 = x1.reshape(t, d)
    h2p = h2p.reshape(t, d // 2)
    logits = logits.reshape(t, N_EXPERTS)

    idx_t, wts_t, rank_t, counts = _router(logits, router_bias[0])
    idx, wts, rank = idx_t.T, wts_t.T, rank_t.T

    cnt = counts[:, 0].astype(I32)
    padded = (cnt + MOE_BLOCK - 1) // MOE_BLOCK * MOE_BLOCK
    pend = jnp.cumsum(padded)
    pstart = pend - padded
    nblk = -(-(t * TOP_K) // MOE_BLOCK) + N_EXPERTS
    blk_row0 = jnp.arange(nblk, dtype=I32) * MOE_BLOCK
    blk_expert = jnp.minimum(jnp.sum((pend[None, :] <= blk_row0[:, None]).astype(I32), axis=1), N_EXPERTS - 1)
    blk_valid = jnp.clip(pstart[blk_expert] + cnt[blk_expert] - blk_row0, 0, MOE_BLOCK).astype(I32)

    dest = _dest(idx, rank, pstart.astype(F32).reshape(1, N_EXPERTS)).reshape(t * TOP_K)
    xs = _dispatch(dest, h2p, nblk * MOE_BLOCK)
    ys = _experts(blk_expert, blk_valid, xs, w_gate_e[0], w_up_e[0], w_down_e[0])
    out = _combine(dest, wts, ys, h2p, x1, mod, w_gate_s[0].astype(BF16), w_up_s[0].astype(BF16),
                   w_down_s[0].astype(BF16), s)
    return out.reshape(b, s, d)
```

```python
import functools
import math

import jax
import jax.numpy as jnp
from jax import lax
from jax.experimental import pallas as pl
from jax.experimental.pallas import tpu as pltpu

F32 = jnp.float32
BF16 = jnp.bfloat16
I32 = jnp.int32
U32 = jnp.uint32

HEAD_DIM = 64
GRID_W = 64
ROPE_FREQS = HEAD_DIM // 4
ROPE_BASE = 10000.0
EPS = 1e-6
ATTN_SCALE = HEAD_DIM ** -0.5
BLOCK = 128
A_Q_HEADS = 16
A_KV_HEADS = 2
A_GROUP = A_Q_HEADS // A_KV_HEADS
B_HEADS = 8
N_EXPERTS = 256
TOP_K = 8
N_GROUPS = 8
TOPK_GROUPS = 4
GROUP_SIZE = N_EXPERTS // N_GROUPS
ROUTED_SCALE = 2.5
MOE_BLOCK = 512
LAM_INIT = 0.8 - 0.6 * math.exp(-0.3 * 0)

LANES = 128
SUBLANES = 8
MXU_N = 256
VMEM_LIMIT = 56 * 1024 * 1024
NEG_BIG = -1e30
LOG2E = math.log2(math.e)
MAX_FIXED_SHIFT = 60.0

LAT_KB, LAT_QA, LAT_QB, LAT_KA, LAT_VA = 0, 8, 16, 24, 25
LAT_NQK = 13
CTX_KB, CTX_KA, CTX_VA = 0, 8, 9
CTX_NQK = 5


def _tile(n, pref):
    return pref if n % pref == 0 else n


def _cparams(sem):
    return pltpu.CompilerParams(dimension_semantics=sem, vmem_limit_bytes=VMEM_LIMIT)


def _nt_dot(a, b):
    return lax.dot_general(a, b, (((1,), (1,)), ((), ())), preferred_element_type=F32)


def _dot(a, b):
    return jnp.dot(a, b, preferred_element_type=F32)


def _silu(x):
    return x * (1.0 / (1.0 + jnp.exp(-x)))


def _sigmoid(x):
    return 1.0 / (1.0 + jnp.exp(-x))


def _modulated_norm(xf, g_row, shift_row, scale_row):
    ms = jnp.mean(xf * xf, axis=-1, keepdims=True)
    y = xf * lax.rsqrt(ms + EPS) * g_row
    return y * (1.0 + scale_row) + shift_row


def _ada_kernel(c_ref, w_ref, b_ref, o_ref):
    o_ref[...] = _dot(_silu(c_ref[...]).astype(BF16), w_ref[...].astype(BF16)) + b_ref[...]


def _ada(cc, w_ada, b_ada):
    rows, d = cc.shape
    n = w_ada.shape[1]
    tn = _tile(n, 512)
    return pl.pallas_call(
        _ada_kernel,
        out_shape=jax.ShapeDtypeStruct((rows, n), F32),
        grid=(n // tn,),
        in_specs=[pl.BlockSpec((rows, d), lambda j: (0, 0)),
                  pl.BlockSpec((d, tn), lambda j: (0, j)),
                  pl.BlockSpec((1, tn), lambda j: (0, j))],
        out_specs=pl.BlockSpec((rows, tn), lambda j: (0, j)),
        compiler_params=_cparams(("parallel",)),
        name="ada",
    )(cc, w_ada, b_ada.reshape(1, n))


def _proj_kernel(x_ref, mod_ref, n1_ref, w_ref, gain_ref, seg_ref, cos_ref, sin_ref, o_ref, vt_ref,
                 *, nqk, use_rope):
    nj = w_ref.shape[1] // MXU_N
    h = _modulated_norm(x_ref[0], n1_ref[...], mod_ref[0, 0:1, :], mod_ref[0, 1:2, :]).astype(BF16)
    lane = lax.broadcasted_iota(I32, (1, MXU_N), 1)
    first = (lane % (2 * ROPE_FREQS)) < ROPE_FREQS

    def matmul(j):
        return _dot(h, w_ref[:, j * MXU_N:(j + 1) * MXU_N])

    def finish(j, acc):
        if j >= nqk:
            vt_ref[0, (j - nqk) * MXU_N:(j - nqk + 1) * MXU_N, :] = acc.T.astype(BF16)
            return
        ms = _dot((acc * acc).astype(BF16), seg_ref[...])
        y = acc * lax.rsqrt(ms + EPS) * gain_ref[j]
        if use_rope:
            rot = jnp.where(first, pltpu.roll(y, MXU_N - ROPE_FREQS, 1), pltpu.roll(y, ROPE_FREQS, 1))
            y = y * cos_ref[...] + rot * sin_ref[...]
        if j == nqk - 1:
            y = jnp.where(lane < LANES, y, acc)
        o_ref[0, :, j * MXU_N:(j + 1) * MXU_N] = y.astype(BF16)

    acc = matmul(0)
    for j in range(nj):
        nxt = matmul(j + 1) if j + 1 < nj else None
        finish(j, acc)
        acc = nxt


def _proj(x, mod, n1, w, gains, seg, cos_t, sin_t, *, nqk, use_rope):
    b, s, d = x.shape
    ncols = w.shape[1]
    tm = _tile(s, 512)
    nj = ncols // MXU_N
    const2 = lambda bi, si: (0, 0)
    return pl.pallas_call(
        functools.partial(_proj_kernel, nqk=nqk, use_rope=use_rope),
        out_shape=(jax.ShapeDtypeStruct((b, s, nqk * MXU_N), BF16),
                   jax.ShapeDtypeStruct((b, (nj - nqk) * MXU_N, s), BF16)),
        grid=(b, s // tm),
        in_specs=[
            pl.BlockSpec((1, tm, d), lambda bi, si: (bi, si, 0)),
            pl.BlockSpec((1, 6, d), lambda bi, si: (bi, 0, 0)),
            pl.BlockSpec((1, d), const2),
            pl.BlockSpec((d, ncols), const2),
            pl.BlockSpec((nqk, 1, MXU_N), lambda bi, si: (0, 0, 0)),
            pl.BlockSpec((MXU_N, MXU_N), const2),
            pl.BlockSpec((tm, MXU_N), lambda bi, si: (si, 0)),
            pl.BlockSpec((tm, MXU_N), lambda bi, si: (si, 0)),
        ],
        out_specs=(pl.BlockSpec((1, tm, nqk * MXU_N), lambda bi, si: (bi, si, 0)),
                   pl.BlockSpec((1, (nj - nqk) * MXU_N, tm), lambda bi, si: (bi, 0, si))),
        compiler_params=_cparams(("parallel", "parallel")),
        name="proj_rope" if use_rope else "proj_ctx",
    )(x, mod, n1, w, gains, seg, cos_t, sin_t)


def _attn_a_kernel(sink_ref, q_ref, kp_ref, kc_ref, kn_ref, vp_ref, vc_ref, vn_ref, kx_ref, vx_ref, o_ref):
    i = pl.program_id(1)
    nb = pl.num_programs(1)
    nctx = kx_ref.shape[1]
    span = 3 * BLOCK + nctx
    kall = jnp.concatenate([kp_ref[0], kc_ref[0], kn_ref[0], kx_ref[0]], axis=0)
    vall = jnp.concatenate([vp_ref[0], vc_ref[0], vn_ref[0], vx_ref[0]], axis=0)

    c = lax.broadcasted_iota(I32, (span, BLOCK), 0)
    r = lax.broadcasted_iota(I32, (span, BLOCK), 1)
    prev_ok = (c < BLOCK) & (c >= r) & (i > 0)
    cur_ok = (c >= BLOCK) & (c < 2 * BLOCK)
    next_ok = (c >= 2 * BLOCK) & (c < 3 * BLOCK) & (c - 2 * BLOCK <= r) & (i < nb - 1)
    valid = prev_ok | cur_ok | next_ok | (c >= 3 * BLOCK)
    bias = jnp.where(valid, 0.0, NEG_BIG)

    vt_all = vall.astype(F32).T
    zeros = jnp.zeros((HEAD_DIM, BLOCK), F32)
    pair_t = [q_ref[0, :, j * LANES:(j + 1) * LANES].astype(F32).T for j in range(A_Q_HEADS // 2)]
    for kv in range(A_KV_HEADS):
        qts = []
        for g in range(A_GROUP):
            hh = kv * A_GROUP + g
            mine = pair_t[hh // 2][(hh % 2) * HEAD_DIM:(hh % 2 + 1) * HEAD_DIM]
            qts.append(jnp.concatenate([mine, zeros] if kv == 0 else [zeros, mine], axis=0).astype(BF16))
        s_all = _dot(kall, jnp.concatenate(qts, axis=1))
        ps, inv_ls = [], []
        for g in range(A_GROUP):
            sink = sink_ref[kv, g]
            s = s_all[:, g * BLOCK:(g + 1) * BLOCK] + bias
            m = jnp.maximum(jnp.max(s, axis=0, keepdims=True), sink)
            p = jnp.exp(s - m)
            inv_ls.append(1.0 / (jnp.sum(p, axis=0, keepdims=True) + jnp.exp(sink - m)))
            ps.append(p.astype(BF16))
        vt = vt_all[kv * HEAD_DIM:(kv + 1) * HEAD_DIM].astype(BF16)
        o_all = _dot(vt, jnp.concatenate(ps, axis=1))
        os_ = [o_all[:, g * BLOCK:(g + 1) * BLOCK] * inv_ls[g] for g in range(A_GROUP)]
        for pair in range(A_GROUP // 2):
            both = jnp.concatenate([os_[2 * pair], os_[2 * pair + 1]], axis=0)
            col = (kv * (A_GROUP // 2) + pair) * LANES
            o_ref[0, :, col:col + LANES] = both.T.astype(BF16)


def _attn_a(sink, lat, ctxp):
    b, s, _ = lat.shape
    nctx = ctxp.shape[1]
    nb = s // BLOCK
    width = A_Q_HEADS * HEAD_DIM

    def kspec(col, shift):
        return pl.BlockSpec((1, BLOCK, LANES),
                            lambda bi, i: (bi, jnp.clip(i + shift, 0, nb - 1), col))

    return pl.pallas_call(
        _attn_a_kernel,
        out_shape=jax.ShapeDtypeStruct((b, s, width), BF16),
        grid=(b, nb),
        in_specs=[
            pl.BlockSpec(memory_space=pltpu.SMEM),
            pl.BlockSpec((1, BLOCK, width), lambda bi, i: (bi, i, LAT_QA * LANES // width)),
            kspec(LAT_KA, -1), kspec(LAT_KA, 0), kspec(LAT_KA, 1),
            kspec(LAT_VA, -1), kspec(LAT_VA, 0), kspec(LAT_VA, 1),
            pl.BlockSpec((1, nctx, LANES), lambda bi, i: (bi, 0, CTX_KA)),
            pl.BlockSpec((1, nctx, LANES), lambda bi, i: (bi, 0, CTX_VA)),
        ],
        out_specs=pl.BlockSpec((1, BLOCK, width), lambda bi, i: (bi, i, 0)),
        compiler_params=_cparams(("parallel", "parallel")),
        name="attn_a",
    )(sink, lat, lat, lat, lat, lat, lat, lat, ctxp, ctxp)


def _attn_b_kernel(bound_ref, lamv_ref, sub_ref, q_ref, k_ref, vt_ref, kx_ref, vtx_ref, o_ref, acc_ref, l_ref,
                   s0_ref, s1_ref, sx_ref, *, tk):
    tq = q_ref.shape[1]
    s_len = k_ref.shape[1]
    qt = q_ref[0].astype(F32).T
    row = lax.broadcasted_iota(I32, (LANES, tq), 0)
    qts = [jnp.where(row < HEAD_DIM, qt, 0.0).astype(BF16), jnp.where(row >= HEAD_DIM, qt, 0.0).astype(BF16)]
    acc_ref[...] = jnp.zeros(acc_ref.shape, F32)
    n = s_len // tk
    bound = bound_ref[0]

    def kchunk(i):
        return k_ref[0, pl.ds(pl.multiple_of(i * tk, tk), tk), :]

    def vchunk(i):
        return vt_ref[0, :, pl.ds(pl.multiple_of(i * tk, tk), tk)]

    def run(fixed):
        def scores(j, kc, s_ref):
            s = _dot(kc, qts[j])
            s_ref[j] = s
            return jnp.zeros((1, tq), F32) if fixed else jnp.max(s, axis=0, keepdims=True)

        def accumulate(j, s_ref, cm, vtc, stat):
            m_old, l_old = stat
            if fixed:
                p = jnp.exp2(s_ref[j] - bound)
                acc_ref[j] += _dot(vtc, p.astype(BF16))
                return m_old, l_old + jnp.sum(p, axis=0, keepdims=True)
            m_new = jnp.maximum(m_old, cm)
            alpha = jnp.exp2(m_old - m_new)
            p = jnp.exp2(s_ref[j] - m_new)
            acc_ref[j] = alpha * acc_ref[j] + _dot(vtc, p.astype(BF16))
            return m_new, alpha * l_old + jnp.sum(p, axis=0, keepdims=True)

        def half_step(k_next, s_next, s_cur, cms, vtc, stats):
            new_cms, new_stats = [], []
            for j in range(2):
                new_cms.append(scores(j, k_next, s_next))
                new_stats.append(accumulate(j, s_cur, cms[j], vtc, stats[j]))
            return tuple(new_cms), tuple(new_stats)

        m0 = jnp.zeros((1, tq), F32) if fixed else jnp.full((1, tq), NEG_BIG, F32)
        stats = ((m0, jnp.zeros((1, tq), F32)),) * 2
        cmx = tuple(scores(j, kx_ref[0], sx_ref) for j in range(2))
        cms, stats = half_step(kchunk(0), s0_ref, sx_ref, cmx, vtx_ref[0], stats)

        def body(pair, carry):
            cms, stats = carry
            i = 2 * pair
            cms, stats = half_step(kchunk(i + 1), s1_ref, s0_ref, cms, vchunk(i), stats)
            return half_step(kchunk(i + 2), s0_ref, s1_ref, cms, vchunk(i + 1), stats)

        cms, stats = lax.fori_loop(0, n // 2 - 1, body, (cms, stats))
        cms, stats = half_step(kchunk(n - 1), s1_ref, s0_ref, cms, vchunk(n - 2), stats)
        for j in range(2):
            l_ref[j] = accumulate(j, s1_ref, cms[j], vchunk(n - 1), stats[j])[1]

    pl.when(bound <= MAX_FIXED_SHIFT)(lambda: run(True))
    pl.when(bound > MAX_FIXED_SHIFT)(lambda: run(False))


    lv = lamv_ref[...]
    lam = (jnp.exp(jnp.sum(lv[0:1] * lv[1:2], axis=-1, keepdims=True))
           - jnp.exp(jnp.sum(lv[2:3] * lv[3:4], axis=-1, keepdims=True)) + LAM_INIT)
    o = acc_ref[0] * (1.0 / l_ref[0]) - acc_ref[1] * (lam / l_ref[1])
    ms = jnp.mean(o * o, axis=0, keepdims=True)
    o = o * lax.rsqrt(ms + EPS) * (sub_ref[...] * (1.0 - LAM_INIT))
    o_ref[0] = o.T.astype(BF16)


def _attn_b(bound, lamv, subln_col, lat, vt, ctxp, vtx):
    b, s, _ = lat.shape
    nctx = ctxp.shape[1]
    tq = _tile(s, 512)
    tk = min(512, s // 2)
    assert s % (2 * tk) == 0
    return pl.pallas_call(
        functools.partial(_attn_b_kernel, tk=tk),
        out_shape=jax.ShapeDtypeStruct((b, s, B_HEADS * LANES), BF16),
        grid=(b, B_HEADS, s // tq),
        in_specs=[
            pl.BlockSpec(memory_space=pltpu.SMEM),
            pl.BlockSpec((4, HEAD_DIM), lambda bi, h, qi: (0, 0)),
            pl.BlockSpec((LANES, 1), lambda bi, h, qi: (0, 0)),
            pl.BlockSpec((1, tq, LANES), lambda bi, h, qi: (bi, qi, LAT_QB + h)),
            pl.BlockSpec((1, s, LANES), lambda bi, h, qi: (bi, 0, LAT_KB + h)),
            pl.BlockSpec((1, LANES, s), lambda bi, h, qi: (bi, h, 0)),
            pl.BlockSpec((1, nctx, LANES), lambda bi, h, qi: (bi, 0, CTX_KB + h)),
            pl.BlockSpec((1, LANES, nctx), lambda bi, h, qi: (bi, h, 0)),
        ],
        out_specs=pl.BlockSpec((1, tq, LANES), lambda bi, h, qi: (bi, qi, h)),
        scratch_shapes=[pltpu.VMEM((2, LANES, tq), F32), pltpu.VMEM((2, 1, tq), F32),
                        pltpu.VMEM((2, tk, tq), F32), pltpu.VMEM((2, tk, tq), F32),
                        pltpu.VMEM((2, nctx, tq), F32)],
        compiler_params=_cparams(("parallel", "parallel", "arbitrary")),
        name="attn_b",
    )(bound, lamv, subln_col, lat, lat, vt, ctxp, vtx)


def _pack_rows(h):
    n = h.shape[1] // 2
    bits = pltpu.bitcast(h.astype(BF16).astype(F32), U32)
    return (bits[:, :n] >> 16) | (bits[:, n:] & jnp.uint32(0xFFFF0000))


def _unpack_rows(w):
    lo = pltpu.bitcast(w << 16, F32)
    hi = pltpu.bitcast(w & jnp.uint32(0xFFFF0000), F32)
    return jnp.concatenate([lo, hi], axis=1).astype(BF16)


def _merge_kernel(x_ref, mod_ref, n1_ref, n2_ref, oa_ref, ob_ref, wg_ref, wpa_ref, wpb_ref, wo_ref, wr_ref,
                  x1_ref, h2_ref, lg_ref):
    d = x_ref.shape[2]
    xf = x_ref[0]
    h = _modulated_norm(xf, n1_ref[...], mod_ref[0, 0:1, :], mod_ref[0, 1:2, :]).astype(BF16)
    gates = _sigmoid(_dot(h, wg_ref[...]))
    ya = _dot(oa_ref[0], wpa_ref[...])
    yb = _dot(ob_ref[0], wpb_ref[...])
    z = gates[:, :d] * ya + gates[:, d:] * yb
    x1 = xf + mod_ref[0, 2:3, :] * _dot(z.astype(BF16), wo_ref[...])
    x1_ref[0] = x1
    h2 = _modulated_norm(x1, n2_ref[...], mod_ref[0, 3:4, :], mod_ref[0, 4:5, :])
    lg_ref[0] = _dot(h2.astype(BF16), wr_ref[...])
    h2_ref[0] = _pack_rows(h2)


def _merge(x, mod, n1, n2, oa, ob, wg, wpa, wpb, wo, wr):
    b, s, d = x.shape
    tm = _tile(s, 256)
    ne = wr.shape[1]
    const = lambda bi, si: (0, 0)
    row = lambda bi, si: (bi, si, 0)
    return pl.pallas_call(
        _merge_kernel,
        out_shape=(jax.ShapeDtypeStruct((b, s, d), F32),
                   jax.ShapeDtypeStruct((b, s, d // 2), U32),
                   jax.ShapeDtypeStruct((b, s, ne), F32)),
        grid=(b, s // tm),
        in_specs=[
            pl.BlockSpec((1, tm, d), row),
            pl.BlockSpec((1, 6, d), lambda bi, si: (bi, 0, 0)),
            pl.BlockSpec((1, d), const), pl.BlockSpec((1, d), const),
            pl.BlockSpec((1, tm, d), row), pl.BlockSpec((1, tm, d), row),
            pl.BlockSpec((d, 2 * d), const), pl.BlockSpec((d, d), const),
            pl.BlockSpec((d, d), const), pl.BlockSpec((d, d), const),
            pl.BlockSpec((d, ne), const),
        ],
        out_specs=(pl.BlockSpec((1, tm, d), row), pl.BlockSpec((1, tm, d // 2), row),
                   pl.BlockSpec((1, tm, ne), row)),
        compiler_params=_cparams(("parallel", "parallel")),
        name="merge",
    )(x, mod, n1, n2, oa, ob, wg, wpa, wpb, wo, wr)


def _router_kernel(lg_ref, bias_ref, idx_ref, w_ref, rank_ref, cnt_ref, run_ref):
    i = pl.program_id(0)
    tr = lg_ref.shape[0]

    @pl.when(i == 0)
    def _():
        run_ref[...] = jnp.zeros(run_ref.shape, F32)

    scores = _sigmoid(lg_ref[...].T)
    biased = scores + bias_ref[...]
    row = lax.broadcasted_iota(I32, scores.shape, 0)
    neg_inf = -jnp.inf

    def first_argmax(vals, rows):
        m = jnp.max(vals, axis=0, keepdims=True)
        idx = jnp.min(jnp.where(vals == m, rows, N_EXPERTS), axis=0, keepdims=True)
        return m, idx

    def group(a, g):
        return a[g * GROUP_SIZE:(g + 1) * GROUP_SIZE]

    gscore = []
    for g in range(N_GROUPS):
        vals = group(biased, g)
        rows = lax.broadcasted_iota(I32, vals.shape, 0) + g * GROUP_SIZE
        m1, i1 = first_argmax(vals, rows)
        m2 = jnp.max(jnp.where(rows == i1, neg_inf, vals), axis=0, keepdims=True)
        gscore.append(m1 + m2)
    parts = []
    for g in range(N_GROUPS):
        beaten = jnp.zeros((1, tr), F32)
        for g2 in range(N_GROUPS):
            if g2 == g:
                continue
            wins = (gscore[g2] > gscore[g]) | ((gscore[g2] == gscore[g]) & (g2 < g))
            beaten = beaten + jnp.where(wins, 1.0, 0.0)
        parts.append(jnp.where(beaten < TOPK_GROUPS, group(biased, g), neg_inf))
    masked = jnp.concatenate(parts, axis=0)

    sel = jnp.zeros(scores.shape, F32)
    idxs, ws = [], []
    for _ in range(TOP_K):
        _, ik = first_argmax(masked, row)
        hit = row == ik
        idxs.append(ik)
        ws.append(jnp.sum(jnp.where(hit, scores, 0.0), axis=0, keepdims=True))
        sel = jnp.where(hit, 1.0, sel)
        masked = jnp.where(hit, neg_inf, masked)
    wsum = ws[0]
    for k in range(1, TOP_K):
        wsum = wsum + ws[k]

    rr = lax.broadcasted_iota(I32, (tr, tr), 0)
    cc = lax.broadcasted_iota(I32, (tr, tr), 1)
    earlier = jnp.where(rr < cc, 1.0, 0.0).astype(BF16)
    run = run_ref[...]
    selb = sel.astype(BF16)
    before = _dot(selb, earlier) + run
    ranks = [jnp.sum(jnp.where(row == idxs[k], before, 0.0), axis=0, keepdims=True) for k in range(TOP_K)]
    run_ref[...] = run + _dot(selb, jnp.ones((tr, tr), BF16))
    cnt_ref[...] = run_ref[...]

    def stack(rows_):
        slot = lax.broadcasted_iota(I32, (TOP_K, tr), 0)
        out = jnp.broadcast_to(rows_[0], (TOP_K, tr))
        for k in range(1, TOP_K):
            out = jnp.where(slot == k, rows_[k], out)
        return out

    idx_ref[...] = stack(idxs)
    w_ref[...] = stack([w / wsum * ROUTED_SCALE for w in ws])
    rank_ref[...] = stack(ranks).astype(I32)


def _router(logits, bias):
    t, ne = logits.shape
    tr = _tile(t, 512)
    bias_b = jnp.broadcast_to(bias[:, None], (ne, tr))
    return pl.pallas_call(
        _router_kernel,
        out_shape=(jax.ShapeDtypeStruct((TOP_K, t), I32), jax.ShapeDtypeStruct((TOP_K, t), F32),
                   jax.ShapeDtypeStruct((TOP_K, t), I32), jax.ShapeDtypeStruct((ne, tr), F32)),
        grid=(t // tr,),
        in_specs=[pl.BlockSpec((tr, ne), lambda i: (i, 0)), pl.BlockSpec((ne, tr), lambda i: (0, 0))],
        out_specs=(pl.BlockSpec((TOP_K, tr), lambda i: (0, i)), pl.BlockSpec((TOP_K, tr), lambda i: (0, i)),
                   pl.BlockSpec((TOP_K, tr), lambda i: (0, i)), pl.BlockSpec((ne, tr), lambda i: (0, 0))),
        scratch_shapes=[pltpu.VMEM((ne, tr), F32)],
        compiler_params=_cparams(("arbitrary",)),
        name="router",
    )(logits, bias_b)


def _dest_kernel(idx_ref, rank_ref, pstart_ref, dest_ref):
    idx = idx_ref[...]
    lane = lax.broadcasted_iota(I32, (idx.shape[0], N_EXPERTS), 1)
    cols = []
    for k in range(TOP_K):
        start = jnp.sum(jnp.where(lane == idx[:, k:k + 1], pstart_ref[...], 0.0), axis=-1, keepdims=True)
        cols.append(start.astype(I32) + rank_ref[:, k:k + 1])
    dest_ref[...] = jnp.concatenate(cols, axis=1)


def _dest(idx, rank, pstart):
    t = idx.shape[0]
    tr = _tile(t, 512)
    return pl.pallas_call(
        _dest_kernel,
        out_shape=jax.ShapeDtypeStruct((t, TOP_K), I32),
        grid=(t // tr,),
        in_specs=[pl.BlockSpec((tr, TOP_K), lambda i: (i, 0)), pl.BlockSpec((tr, TOP_K), lambda i: (i, 0)),
                  pl.BlockSpec((1, N_EXPERTS), lambda i: (0, 0))],
        out_specs=pl.BlockSpec((tr, TOP_K), lambda i: (i, 0)),
        compiler_params=_cparams(("parallel",)),
        name="dest",
    )(idx, rank, pstart)


def _dispatch_kernel(dest_hbm, h2_ref, xs_out, dest_smem, sem_idx, sem_rows):
    i = pl.program_id(0)
    td = h2_ref.shape[0]
    n = td * TOP_K
    cp = pltpu.make_async_copy(dest_hbm.at[pl.ds(i * n, n)], dest_smem, sem_idx)
    cp.start()
    cp.wait()

    def row_copy(t, d):
        return pltpu.make_async_copy(h2_ref.at[pl.ds(t, 1), :], xs_out.at[pl.ds(d, 1), :], sem_rows)

    def issue(g, carry):
        base = pl.multiple_of(g * SUBLANES, SUBLANES)
        for r in range(SUBLANES):
            for k in range(TOP_K):
                row_copy(base + r, dest_smem[(g * SUBLANES + r) * TOP_K + k]).start(priority=k % 2)
        return carry

    lax.fori_loop(0, td // SUBLANES, issue, 0)

    def drain(t, carry):
        for k in range(TOP_K):
            row_copy(0, 0).wait()
        return carry

    lax.fori_loop(0, td, drain, 0)


def _dispatch(dest_flat, h2p, nrows):
    t, half = h2p.shape
    td = _tile(t, 1024)
    return pl.pallas_call(
        _dispatch_kernel,
        out_shape=jax.ShapeDtypeStruct((nrows, half), U32),
        grid=(t // td,),
        in_specs=[pl.BlockSpec(memory_space=pl.ANY),
                  pl.BlockSpec((td, half), lambda i: (i, 0))],
        out_specs=pl.BlockSpec(memory_space=pl.ANY),
        scratch_shapes=[pltpu.SMEM((td * TOP_K,), I32), pltpu.SemaphoreType.DMA, pltpu.SemaphoreType.DMA],
        compiler_params=_cparams(("arbitrary",)),
        name="dispatch",
    )(dest_flat, h2p)


def _experts_kernel(be_ref, valid_ref, xs_ref, wg_ref, wu_ref, wd_ref, ys_ref, wgb_ref, wub_ref, wdb_ref):
    blk = pl.program_id(0)
    valid = valid_ref[blk]
    used = valid > 0

    @pl.when(used & ((blk == 0) | (be_ref[blk] != be_ref[jnp.maximum(blk - 1, 0)])))
    def _():
        wgb_ref[...] = wg_ref[0].astype(BF16)
        wub_ref[...] = wu_ref[0].astype(BF16)
        wdb_ref[...] = wd_ref[0].astype(BF16)

    @pl.when(used)
    def _():
        rows = lax.broadcasted_iota(I32, (xs_ref.shape[0], 1), 0)
        x = _unpack_rows(jnp.where(rows < valid, xs_ref[...], jnp.uint32(0)))
        g = _dot(x, wgb_ref[...])
        u = _dot(x, wub_ref[...])
        a = (_silu(g) * u).astype(BF16)
        ys_ref[...] = _pack_rows(_dot(a, wdb_ref[...]))

    @pl.when(jnp.logical_not(used))
    def _():
        ys_ref[...] = jnp.zeros(ys_ref.shape, U32)


def _experts(blk_expert, blk_valid, xs, wg, wu, wd):
    p, half = xs.shape
    d = 2 * half
    de = wg.shape[2]
    nblk = p // MOE_BLOCK
    grid_spec = pltpu.PrefetchScalarGridSpec(
        num_scalar_prefetch=2,
        grid=(nblk,),
        in_specs=[
            pl.BlockSpec((MOE_BLOCK, half), lambda i, be, nu: (i, 0)),
            pl.BlockSpec((1, d, de), lambda i, be, nu: (be[i], 0, 0)),
            pl.BlockSpec((1, d, de), lambda i, be, nu: (be[i], 0, 0)),
            pl.BlockSpec((1, de, d), lambda i, be, nu: (be[i], 0, 0)),
        ],
        out_specs=pl.BlockSpec((MOE_BLOCK, half), lambda i, be, nu: (i, 0)),
        scratch_shapes=[pltpu.VMEM((d, de), BF16), pltpu.VMEM((d, de), BF16), pltpu.VMEM((de, d), BF16)],
    )
    return pl.pallas_call(
        _experts_kernel,
        out_shape=jax.ShapeDtypeStruct((p, half), U32),
        grid_spec=grid_spec,
        compiler_params=_cparams(("arbitrary",)),
        name="experts",
    )(blk_expert, blk_valid, xs, wg, wu, wd)


def _combine_kernel(dest_hbm, w_ref, ys_hbm, h2_ref, x1_ref, mod_ref, wgs_ref, wus_ref, wds_ref, o_ref,
                    dest_smem, buf, sem_idx, sem_rows):
    i = pl.program_id(0)
    tc = h2_ref.shape[0]
    n = tc * TOP_K
    cp = pltpu.make_async_copy(dest_hbm.at[pl.ds(i * n, n)], dest_smem, sem_idx)
    cp.start()
    cp.wait()

    def row_copy(t, k, d):
        return pltpu.make_async_copy(ys_hbm.at[pl.ds(d, 1), :], buf.at[k, pl.ds(t, 1), :], sem_rows)

    def issue(g, carry):
        base = pl.multiple_of(g * SUBLANES, SUBLANES)
        for r in range(SUBLANES):
            for k in range(TOP_K):
                row_copy(base + r, k, dest_smem[(g * SUBLANES + r) * TOP_K + k]).start(priority=k % 2)
        return carry

    lax.fori_loop(0, tc // SUBLANES, issue, 0)

    x = _unpack_rows(h2_ref[...])
    a = (_silu(_dot(x, wgs_ref[...])) * _dot(x, wus_ref[...])).astype(BF16)
    y = _dot(a, wds_ref[...])

    def drain(t, carry):
        for k in range(TOP_K):
            row_copy(0, k, 0).wait()
        return carry

    lax.fori_loop(0, tc, drain, 0)

    half = buf.shape[2]
    y_lo, y_hi = y[:, :half], y[:, half:]
    for k in range(TOP_K):
        wk = w_ref[:, k:k + 1]
        words = buf[k]
        y_lo = y_lo + pltpu.bitcast(words << 16, F32) * wk
        y_hi = y_hi + pltpu.bitcast(words & jnp.uint32(0xFFFF0000), F32) * wk
    o_ref[...] = x1_ref[...] + mod_ref[0, 5:6, :] * jnp.concatenate([y_lo, y_hi], axis=1)


def _combine(dest_flat, w, ys, h2p, x1, mod, wgs, wus, wds, seq):
    t, d = x1.shape
    half = d // 2
    tc = _tile(seq, 512)
    de = wgs.shape[1]
    per_batch = seq // tc
    const = lambda i: (0, 0)
    return pl.pallas_call(
        _combine_kernel,
        out_shape=jax.ShapeDtypeStruct((t, d), F32),
        grid=(t // tc,),
        in_specs=[
            pl.BlockSpec(memory_space=pl.ANY),
            pl.BlockSpec((tc, TOP_K), lambda i: (i, 0)),
            pl.BlockSpec(memory_space=pl.ANY),
            pl.BlockSpec((tc, half), lambda i: (i, 0)),
            pl.BlockSpec((tc, d), lambda i: (i, 0)),
            pl.BlockSpec((1, 6, d), lambda i: (i // per_batch, 0, 0)),
            pl.BlockSpec((d, de), const), pl.BlockSpec((d, de), const), pl.BlockSpec((de, d), const),
        ],
        out_specs=pl.BlockSpec((tc, d), lambda i: (i, 0)),
        scratch_shapes=[pltpu.SMEM((tc * TOP_K,), I32), pltpu.VMEM((TOP_K, tc, half), U32),
                        pltpu.SemaphoreType.DMA, pltpu.SemaphoreType.DMA],
        compiler_params=_cparams(("arbitrary",)),
        name="combine",
    )(dest_flat, w, ys, h2p, x1, mod, wgs, wus, wds)


def _rope_tables(n):
    rows = n // GRID_W
    row = jnp.broadcast_to(jnp.arange(rows)[:, None], (rows, GRID_W)).reshape(-1)
    col = jnp.broadcast_to(jnp.arange(GRID_W)[None, :], (rows, GRID_W)).reshape(-1)
    freqs = ROPE_BASE ** (-jnp.arange(ROPE_FREQS, dtype=F32) / ROPE_FREQS)
    pos = jnp.stack([row, col], axis=-1).astype(F32)
    ang = pos[:, :, None] * freqs
    cos, sin = jnp.cos(ang), jnp.sin(ang)
    cos64 = jnp.concatenate([cos[:, 0], cos[:, 0], cos[:, 1], cos[:, 1]], axis=-1)
    sin64 = jnp.concatenate([-sin[:, 0], sin[:, 0], -sin[:, 1], sin[:, 1]], axis=-1)
    reps = MXU_N // HEAD_DIM
    return jnp.tile(cos64, (1, reps)), jnp.tile(sin64, (1, reps))


def _gain_rows(qn_a, kn_a, qn_b, kn_b):
    reps = MXU_N // HEAD_DIM
    ones = jnp.ones((LANES,), F32)
    kb = jnp.tile(kn_b, reps)
    qa = jnp.tile(qn_a, reps) * ATTN_SCALE
    qb = jnp.tile(qn_b, reps) * (ATTN_SCALE * LOG2E)
    mixed = jnp.concatenate([jnp.tile(kn_a, LANES // HEAD_DIM), ones])
    lat_g = jnp.stack([kb] * 4 + [qa] * 4 + [qb] * 4 + [mixed])[:, None, :]
    ctx_g = jnp.stack([kb] * 4 + [mixed])[:, None, :]
    return lat_g, ctx_g


def kernel(x, c, ctx, c_ctx, w_ada, b_ada, norm1_g, norm2_g, w_in, qnorm_a, knorm_a, sink_a, qnorm_b, knorm_b,
           lam_q1, lam_k1, lam_q2, lam_k2, subln_g, w_pa, w_pb, w_o, w_router, router_bias, w_gate_e, w_up_e,
           w_down_e, w_gate_s, w_up_s, w_down_s):
    b, s, d = x.shape
    nctx = ctx.shape[1]
    assert w_ada.shape[0] == 1 and d == 1024 and s % BLOCK == 0 and s % GRID_W == 0
    t = b * s

    cc = jnp.concatenate([c, c_ctx[None, :], jnp.zeros((16 - b - 1, d), F32)], axis=0)
    mod_all = _ada(cc, w_ada[0], b_ada[0])
    mod = mod_all[:b].reshape(b, 6, d)
    mod_c = jnp.broadcast_to(mod_all[b].reshape(1, 6, d), (b, 6, d))

    w = w_in[0]
    ka_w, va_w = w[:, 0:128], w[:, 128:256]
    kb_w, vb_w = w[:, 256:1280], w[:, 1280:2304]
    qa_w, qb_w = w[:, 2304:3328], w[:, 3328:4352]
    gate_w = w[:, 4352:6400].astype(BF16)
    w_lat = jnp.concatenate([kb_w, qa_w, qb_w, ka_w, va_w, vb_w], axis=1).astype(BF16)
    w_ctx = jnp.concatenate([kb_w, ka_w, va_w, vb_w], axis=1).astype(BF16)

    lat_g, ctx_g = _gain_rows(qnorm_a[0], knorm_a[0], qnorm_b[0], knorm_b[0])
    hid = jnp.arange(MXU_N) // HEAD_DIM
    seg = jnp.where(hid[:, None] == hid[None, :], 1.0 / HEAD_DIM, 0.0).astype(BF16)
    cos_t, sin_t = _rope_tables(s)
    n1 = norm1_g[0].reshape(1, d)
    n2 = norm2_g[0].reshape(1, d)

    lat, vt = _proj(x, mod, n1, w_lat, lat_g, seg, cos_t, sin_t, nqk=LAT_NQK, use_rope=True)
    ctxp, vtx = _proj(ctx, mod_c, n1, w_ctx, ctx_g, seg, cos_t[:nctx], sin_t[:nctx], nqk=CTX_NQK, use_rope=False)

    oa = _attn_a(sink_a[0], lat, ctxp)
    lamv = jnp.stack([lam_q1[0], lam_k1[0], lam_q2[0], lam_k2[0]])
    bound = (HEAD_DIM * ATTN_SCALE * LOG2E * 1.01) * jnp.max(jnp.abs(qnorm_b[0])) * jnp.max(jnp.abs(knorm_b[0]))
    ob = _attn_b(bound.reshape(1), lamv, subln_g[0].reshape(LANES, 1), lat, vt, ctxp, vtx)

    x1, h2p, logits = _merge(x, mod, n1, n2, oa, ob, gate_w, w_pa[0].astype(BF16), w_pb[0].astype(BF16),
                             w_o[0].astype(BF16), w_router[0].astype(BF16))
    x1 = x1.reshape(t, d)
    h2p = h2p.reshape(t, d // 2)
    logits = logits.reshape(t, N_EXPERTS)

    idx_t, wts_t, rank_t, counts = _router(logits, router_bias[0])
    idx, wts, rank = idx_t.T, wts_t.T, rank_t.T

    cnt = counts[:, 0].astype(I32)
    padded = (cnt + MOE_BLOCK - 1) // MOE_BLOCK * MOE_BLOCK
    pend = jnp.cumsum(padded)
    pstart = pend - padded
    nblk = -(-(t * TOP_K) // MOE_BLOCK) + N_EXPERTS
    blk_row0 = jnp.arange(nblk, dtype=I32) * MOE_BLOCK
    blk_expert = jnp.minimum(jnp.sum((pend[None, :] <= blk_row0[:, None]).astype(I32), axis=1), N_EXPERTS - 1)
    blk_valid = jnp.clip(pstart[blk_expert] + cnt[blk_expert] - blk_row0, 0, MOE_BLOCK).astype(I32)

    dest = _dest(idx, rank, pstart.astype(F32).reshape(1, N_EXPERTS)).reshape(t * TOP_K)
    xs = _dispatch(dest, h2p, nblk * MOE_BLOCK)
    ys = _experts(blk_expert, blk_valid, xs, w_gate_e[0], w_up_e[0], w_down_e[0])
    out = _combine(dest, wts, ys, h2p, x1, mod, w_gate_s[0].astype(BF16), w_up_s[0].astype(BF16),
                   w_down_s[0].astype(BF16), s)
    return out.reshape(b, s, d)
```

```python
import functools
import math

import jax
import jax.numpy as jnp
from jax import lax
from jax.experimental import pallas as pl
from jax.experimental.pallas import tpu as pltpu

F32 = jnp.float32
BF16 = jnp.bfloat16
I32 = jnp.int32
U32 = jnp.uint32

HEAD_DIM = 64
GRID_W = 64
ROPE_FREQS = HEAD_DIM // 4
ROPE_BASE = 10000.0
EPS = 1e-6
ATTN_SCALE = HEAD_DIM ** -0.5
BLOCK = 128
A_Q_HEADS = 16
A_KV_HEADS = 2
A_GROUP = A_Q_HEADS // A_KV_HEADS
B_HEADS = 8
N_EXPERTS = 256
TOP_K = 8
N_GROUPS = 8
TOPK_GROUPS = 4
GROUP_SIZE = N_EXPERTS // N_GROUPS
ROUTED_SCALE = 2.5
MOE_BLOCK = 512
LAM_INIT = 0.8 - 0.6 * math.exp(-0.3 * 0)

LANES = 128
SUBLANES = 8
MXU_N = 256
VMEM_LIMIT = 56 * 1024 * 1024
NEG_BIG = -1e30
LOG2E = math.log2(math.e)
MAX_FIXED_SHIFT = 60.0

LAT_KB, LAT_QA, LAT_QB, LAT_KA, LAT_VA = 0, 8, 16, 24, 25
LAT_NQK = 13
CTX_KB, CTX_KA, CTX_VA = 0, 8, 9
CTX_NQK = 5


def _tile(n, pref):
    return pref if n % pref == 0 else n


def _cparams(sem):
    return pltpu.CompilerParams(dimension_semantics=sem, vmem_limit_bytes=VMEM_LIMIT)


def _nt_dot(a, b):
    return lax.dot_general(a, b, (((1,), (1,)), ((), ())), preferred_element_type=F32)


def _dot(a, b):
    return jnp.dot(a, b, preferred_element_type=F32)


def _silu(x):
    return x * (1.0 / (1.0 + jnp.exp(-x)))


def _sigmoid(x):
    return 1.0 / (1.0 + jnp.exp(-x))


def _stack_rows(rows_):
    n, m = len(rows_), rows_[0].shape[1]
    slot = lax.broadcasted_iota(I32, (n, m), 0)
    out = jnp.broadcast_to(rows_[0], (n, m))
    for k in range(1, n):
        out = jnp.where(slot == k, rows_[k], out)
    return out


def _modulated_norm(xf, g_row, shift_row, scale_row):
    ms = jnp.mean(xf * xf, axis=-1, keepdims=True)
    y = xf * lax.rsqrt(ms + EPS) * g_row
    return y * (1.0 + scale_row) + shift_row


def _ada_kernel(c_ref, w_ref, b_ref, o_ref):
    o_ref[...] = _dot(_silu(c_ref[...]).astype(BF16), w_ref[...].astype(BF16)) + b_ref[...]


def _ada(cc, w_ada, b_ada):
    rows, d = cc.shape
    n = w_ada.shape[1]
    tn = _tile(n, 512)
    return pl.pallas_call(
        _ada_kernel,
        out_shape=jax.ShapeDtypeStruct((rows, n), F32),
        grid=(n // tn,),
        in_specs=[pl.BlockSpec((rows, d), lambda j: (0, 0)),
                  pl.BlockSpec((d, tn), lambda j: (0, j)),
                  pl.BlockSpec((1, tn), lambda j: (0, j))],
        out_specs=pl.BlockSpec((rows, tn), lambda j: (0, j)),
        compiler_params=_cparams(("parallel",)),
        name="ada",
    )(cc, w_ada, b_ada.reshape(1, n))


def _proj_kernel(x_ref, mod_ref, n1_ref, w_ref, gain_ref, seg_ref, cos_ref, sin_ref, o_ref, vt_ref,
                 *, nqk, use_rope):
    nj = w_ref.shape[1] // MXU_N
    h = _modulated_norm(x_ref[0], n1_ref[...], mod_ref[0, 0:1, :], mod_ref[0, 1:2, :]).astype(BF16)
    lane = lax.broadcasted_iota(I32, (1, MXU_N), 1)
    first = (lane % (2 * ROPE_FREQS)) < ROPE_FREQS

    def matmul(j):
        return _dot(h, w_ref[:, j * MXU_N:(j + 1) * MXU_N])

    def finish(j, acc):
        if j >= nqk:
            vt_ref[0, (j - nqk) * MXU_N:(j - nqk + 1) * MXU_N, :] = acc.T.astype(BF16)
            return
        ms = _dot((acc * acc).astype(BF16), seg_ref[...])
        y = acc * lax.rsqrt(ms + EPS) * gain_ref[j]
        if use_rope:
            rot = jnp.where(first, pltpu.roll(y, MXU_N - ROPE_FREQS, 1), pltpu.roll(y, ROPE_FREQS, 1))
            y = y * cos_ref[...] + rot * sin_ref[...]
        if j == nqk - 1:
            y = jnp.where(lane < LANES, y, acc)
        o_ref[0, :, j * MXU_N:(j + 1) * MXU_N] = y.astype(BF16)

    acc = matmul(0)
    for j in range(nj):
        nxt = matmul(j + 1) if j + 1 < nj else None
        finish(j, acc)
        acc = nxt


def _proj(x, mod, n1, w, gains, seg, cos_t, sin_t, *, nqk, use_rope):
    b, s, d = x.shape
    ncols = w.shape[1]
    tm = _tile(s, 512)
    nj = ncols // MXU_N
    const2 = lambda bi, si: (0, 0)
    return pl.pallas_call(
        functools.partial(_proj_kernel, nqk=nqk, use_rope=use_rope),
        out_shape=(jax.ShapeDtypeStruct((b, s, nqk * MXU_N), BF16),
                   jax.ShapeDtypeStruct((b, (nj - nqk) * MXU_N, s), BF16)),
        grid=(b, s // tm),
        in_specs=[
            pl.BlockSpec((1, tm, d), lambda bi, si: (bi, si, 0)),
            pl.BlockSpec((1, 6, d), lambda bi, si: (bi, 0, 0)),
            pl.BlockSpec((1, d), const2),
            pl.BlockSpec((d, ncols), const2),
            pl.BlockSpec((nqk, 1, MXU_N), lambda bi, si: (0, 0, 0)),
            pl.BlockSpec((MXU_N, MXU_N), const2),
            pl.BlockSpec((tm, MXU_N), lambda bi, si: (si, 0)),
            pl.BlockSpec((tm, MXU_N), lambda bi, si: (si, 0)),
        ],
        out_specs=(pl.BlockSpec((1, tm, nqk * MXU_N), lambda bi, si: (bi, si, 0)),
                   pl.BlockSpec((1, (nj - nqk) * MXU_N, tm), lambda bi, si: (bi, 0, si))),
        compiler_params=_cparams(("parallel", "parallel")),
        name="proj_rope" if use_rope else "proj_ctx",
    )(x, mod, n1, w, gains, seg, cos_t, sin_t)


def _attn_a_kernel(sink_ref, q_ref, kp_ref, kc_ref, kn_ref, vp_ref, vc_ref, vn_ref, kx_ref, vx_ref, o_ref):
    i = pl.program_id(1)
    nb = pl.num_programs(1)
    nctx = kx_ref.shape[1]
    span = 3 * BLOCK + nctx
    kall = jnp.concatenate([kp_ref[0], kc_ref[0], kn_ref[0], kx_ref[0]], axis=0)
    vall = jnp.concatenate([vp_ref[0], vc_ref[0], vn_ref[0], vx_ref[0]], axis=0)

    c = lax.broadcasted_iota(I32, (span, BLOCK), 0)
    r = lax.broadcasted_iota(I32, (span, BLOCK), 1)
    prev_ok = (c < BLOCK) & (c >= r) & (i > 0)
    cur_ok = (c >= BLOCK) & (c < 2 * BLOCK)
    next_ok = (c >= 2 * BLOCK) & (c < 3 * BLOCK) & (c - 2 * BLOCK <= r) & (i < nb - 1)
    valid = prev_ok | cur_ok | next_ok | (c >= 3 * BLOCK)
    bias = jnp.where(valid, 0.0, NEG_BIG)

    vt_all = vall.astype(F32).T
    zeros = jnp.zeros((HEAD_DIM, BLOCK), F32)
    pair_t = [q_ref[0, :, j * LANES:(j + 1) * LANES].astype(F32).T for j in range(A_Q_HEADS // 2)]
    for kv in range(A_KV_HEADS):
        qts = []
        for g in range(A_GROUP):
            hh = kv * A_GROUP + g
            mine = pair_t[hh // 2][(hh % 2) * HEAD_DIM:(hh % 2 + 1) * HEAD_DIM]
            qts.append(jnp.concatenate([mine, zeros] if kv == 0 else [zeros, mine], axis=0).astype(BF16))
        s_all = _dot(kall, jnp.concatenate(qts, axis=1))
        ps, inv_ls = [], []
        for g in range(A_GROUP):
            sink = sink_ref[kv, g]
            s = s_all[:, g * BLOCK:(g + 1) * BLOCK] + bias
            m = jnp.maximum(jnp.max(s, axis=0, keepdims=True), sink)
            p = jnp.exp(s - m)
            inv_ls.append(1.0 / (jnp.sum(p, axis=0, keepdims=True) + jnp.exp(sink - m)))
            ps.append(p.astype(BF16))
        vt = vt_all[kv * HEAD_DIM:(kv + 1) * HEAD_DIM].astype(BF16)
        o_all = _dot(vt, jnp.concatenate(ps, axis=1))
        os_ = [o_all[:, g * BLOCK:(g + 1) * BLOCK] * inv_ls[g] for g in range(A_GROUP)]
        for pair in range(A_GROUP // 2):
            both = jnp.concatenate([os_[2 * pair], os_[2 * pair + 1]], axis=0)
            col = (kv * (A_GROUP // 2) + pair) * LANES
            o_ref[0, :, col:col + LANES] = both.T.astype(BF16)


def _attn_a(sink, lat, ctxp):
    b, s, _ = lat.shape
    nctx = ctxp.shape[1]
    nb = s // BLOCK
    width = A_Q_HEADS * HEAD_DIM

    def kspec(col, shift):
        return pl.BlockSpec((1, BLOCK, LANES),
                            lambda bi, i: (bi, jnp.clip(i + shift, 0, nb - 1), col))

    return pl.pallas_call(
        _attn_a_kernel,
        out_shape=jax.ShapeDtypeStruct((b, s, width), BF16),
        grid=(b, nb),
        in_specs=[
            pl.BlockSpec(memory_space=pltpu.SMEM),
            pl.BlockSpec((1, BLOCK, width), lambda bi, i: (bi, i, LAT_QA * LANES // width)),
            kspec(LAT_KA, -1), kspec(LAT_KA, 0), kspec(LAT_KA, 1),
            kspec(LAT_VA, -1), kspec(LAT_VA, 0), kspec(LAT_VA, 1),
            pl.BlockSpec((1, nctx, LANES), lambda bi, i: (bi, 0, CTX_KA)),
            pl.BlockSpec((1, nctx, LANES), lambda bi, i: (bi, 0, CTX_VA)),
        ],
        out_specs=pl.BlockSpec((1, BLOCK, width), lambda bi, i: (bi, i, 0)),
        compiler_params=_cparams(("parallel", "parallel")),
        name="attn_a",
    )(sink, lat, lat, lat, lat, lat, lat, lat, ctxp, ctxp)


def _attn_b_kernel(bound_ref, lamv_ref, sub_ref, q_ref, k_ref, vt_ref, kx_ref, vtx_ref, o_ref, acc_ref, l_ref,
                   s0_ref, s1_ref, sx_ref, *, tk):
    tq = q_ref.shape[1]
    s_len = k_ref.shape[1]
    qt = q_ref[0].astype(F32).T
    row = lax.broadcasted_iota(I32, (LANES, tq), 0)
    qts = [jnp.where(row < HEAD_DIM, qt, 0.0).astype(BF16), jnp.where(row >= HEAD_DIM, qt, 0.0).astype(BF16)]
    acc_ref[...] = jnp.zeros(acc_ref.shape, F32)
    n = s_len // tk
    bound = bound_ref[0]

    def kchunk(i):
        return k_ref[0, pl.ds(pl.multiple_of(i * tk, tk), tk), :]

    def vchunk(i):
        return vt_ref[0, :, pl.ds(pl.multiple_of(i * tk, tk), tk)]

    def run(fixed):
        def scores(j, kc, s_ref):
            s = _dot(kc, qts[j])
            s_ref[j] = s
            return jnp.zeros((1, tq), F32) if fixed else jnp.max(s, axis=0, keepdims=True)

        def accumulate(j, s_ref, cm, vtc, stat):
            m_old, l_old = stat
            if fixed:
                p = jnp.exp2(s_ref[j] - bound)
                acc_ref[j] += _dot(vtc, p.astype(BF16))
                return m_old, l_old + jnp.sum(p, axis=0, keepdims=True)
            m_new = jnp.maximum(m_old, cm)
            alpha = jnp.exp2(m_old - m_new)
            p = jnp.exp2(s_ref[j] - m_new)
            acc_ref[j] = alpha * acc_ref[j] + _dot(vtc, p.astype(BF16))
            return m_new, alpha * l_old + jnp.sum(p, axis=0, keepdims=True)

        def half_step(k_next, s_next, s_cur, cms, vtc, stats):
            new_cms, new_stats = [], []
            for j in range(2):
                new_cms.append(scores(j, k_next, s_next))
                new_stats.append(accumulate(j, s_cur, cms[j], vtc, stats[j]))
            return tuple(new_cms), tuple(new_stats)

        m0 = jnp.zeros((1, tq), F32) if fixed else jnp.full((1, tq), NEG_BIG, F32)
        stats = ((m0, jnp.zeros((1, tq), F32)),) * 2
        cmx = tuple(scores(j, kx_ref[0], sx_ref) for j in range(2))
        cms, stats = half_step(kchunk(0), s0_ref, sx_ref, cmx, vtx_ref[0], stats)

        def body(pair, carry):
            cms, stats = carry
            i = 2 * pair
            cms, stats = half_step(kchunk(i + 1), s1_ref, s0_ref, cms, vchunk(i), stats)
            return half_step(kchunk(i + 2), s0_ref, s1_ref, cms, vchunk(i + 1), stats)

        cms, stats = lax.fori_loop(0, n // 2 - 1, body, (cms, stats))
        cms, stats = half_step(kchunk(n - 1), s1_ref, s0_ref, cms, vchunk(n - 2), stats)
        for j in range(2):
            l_ref[j] = accumulate(j, s1_ref, cms[j], vchunk(n - 1), stats[j])[1]

    pl.when(bound <= MAX_FIXED_SHIFT)(lambda: run(True))
    pl.when(bound > MAX_FIXED_SHIFT)(lambda: run(False))


    lv = lamv_ref[...]
    lam = (jnp.exp(jnp.sum(lv[0:1] * lv[1:2], axis=-1, keepdims=True))
           - jnp.exp(jnp.sum(lv[2:3] * lv[3:4], axis=-1, keepdims=True)) + LAM_INIT)
    o = acc_ref[0] * (1.0 / l_ref[0]) - acc_ref[1] * (lam / l_ref[1])
    ms = jnp.mean(o * o, axis=0, keepdims=True)
    o = o * lax.rsqrt(ms + EPS) * (sub_ref[...] * (1.0 - LAM_INIT))
    o_ref[0] = o.T.astype(BF16)


def _attn_b(bound, lamv, subln_col, lat, vt, ctxp, vtx):
    b, s, _ = lat.shape
    nctx = ctxp.shape[1]
    tq = _tile(s, 512)
    tk = min(512, s // 2)
    assert s % (2 * tk) == 0
    return pl.pallas_call(
        functools.partial(_attn_b_kernel, tk=tk),
        out_shape=jax.ShapeDtypeStruct((b, s, B_HEADS * LANES), BF16),
        grid=(b, B_HEADS, s // tq),
        in_specs=[
            pl.BlockSpec(memory_space=pltpu.SMEM),
            pl.BlockSpec((4, HEAD_DIM), lambda bi, h, qi: (0, 0)),
            pl.BlockSpec((LANES, 1), lambda bi, h, qi: (0, 0)),
            pl.BlockSpec((1, tq, LANES), lambda bi, h, qi: (bi, qi, LAT_QB + h)),
            pl.BlockSpec((1, s, LANES), lambda bi, h, qi: (bi, 0, LAT_KB + h)),
            pl.BlockSpec((1, LANES, s), lambda bi, h, qi: (bi, h, 0)),
            pl.BlockSpec((1, nctx, LANES), lambda bi, h, qi: (bi, 0, CTX_KB + h)),
            pl.BlockSpec((1, LANES, nctx), lambda bi, h, qi: (bi, h, 0)),
        ],
        out_specs=pl.BlockSpec((1, tq, LANES), lambda bi, h, qi: (bi, qi, h)),
        scratch_shapes=[pltpu.VMEM((2, LANES, tq), F32), pltpu.VMEM((2, 1, tq), F32),
                        pltpu.VMEM((2, tk, tq), F32), pltpu.VMEM((2, tk, tq), F32),
                        pltpu.VMEM((2, nctx, tq), F32)],
        compiler_params=_cparams(("parallel", "parallel", "arbitrary")),
        name="attn_b",
    )(bound, lamv, subln_col, lat, lat, vt, ctxp, vtx)


def _pack_rows(h):
    n = h.shape[1] // 2
    bits = pltpu.bitcast(h.astype(BF16).astype(F32), U32)
    return (bits[:, :n] >> 16) | (bits[:, n:] & jnp.uint32(0xFFFF0000))


def _unpack_rows(w):
    lo = pltpu.bitcast(w << 16, F32)
    hi = pltpu.bitcast(w & jnp.uint32(0xFFFF0000), F32)
    return jnp.concatenate([lo, hi], axis=1).astype(BF16)


def _merge_kernel(x_ref, mod_ref, n1_ref, n2_ref, oa_ref, ob_ref, wg_ref, wpa_ref, wpb_ref, wo_ref, wr_ref,
                  x1_ref, h2_ref, lg_ref):
    d = x_ref.shape[2]
    xf = x_ref[0]
    h = _modulated_norm(xf, n1_ref[...], mod_ref[0, 0:1, :], mod_ref[0, 1:2, :]).astype(BF16)
    gates = _sigmoid(_dot(h, wg_ref[...]))
    ya = _dot(oa_ref[0], wpa_ref[...])
    yb = _dot(ob_ref[0], wpb_ref[...])
    z = gates[:, :d] * ya + gates[:, d:] * yb
    x1 = xf + mod_ref[0, 2:3, :] * _dot(z.astype(BF16), wo_ref[...])
    x1_ref[0] = x1
    h2 = _modulated_norm(x1, n2_ref[...], mod_ref[0, 3:4, :], mod_ref[0, 4:5, :])
    lg_ref[0] = _dot(h2.astype(BF16), wr_ref[...])
    h2_ref[0] = _pack_rows(h2)


def _merge(x, mod, n1, n2, oa, ob, wg, wpa, wpb, wo, wr):
    b, s, d = x.shape
    tm = _tile(s, 256)
    ne = wr.shape[1]
    const = lambda bi, si: (0, 0)
    row = lambda bi, si: (bi, si, 0)
    return pl.pallas_call(
        _merge_kernel,
        out_shape=(jax.ShapeDtypeStruct((b, s, d), F32),
                   jax.ShapeDtypeStruct((b, s, d // 2), U32),
                   jax.ShapeDtypeStruct((b, s, ne), F32)),
        grid=(b, s // tm),
        in_specs=[
            pl.BlockSpec((1, tm, d), row),
            pl.BlockSpec((1, 6, d), lambda bi, si: (bi, 0, 0)),
            pl.BlockSpec((1, d), const), pl.BlockSpec((1, d), const),
            pl.BlockSpec((1, tm, d), row), pl.BlockSpec((1, tm, d), row),
            pl.BlockSpec((d, 2 * d), const), pl.BlockSpec((d, d), const),
            pl.BlockSpec((d, d), const), pl.BlockSpec((d, d), const),
            pl.BlockSpec((d, ne), const),
        ],
        out_specs=(pl.BlockSpec((1, tm, d), row), pl.BlockSpec((1, tm, d // 2), row),
                   pl.BlockSpec((1, tm, ne), row)),
        compiler_params=_cparams(("parallel", "parallel")),
        name="merge",
    )(x, mod, n1, n2, oa, ob, wg, wpa, wpb, wo, wr)


def _router_kernel(lg_ref, bias_ref, idx_ref, w_ref, rank_ref, cnt_ref, run_ref):
    i = pl.program_id(0)
    tr = lg_ref.shape[0]

    @pl.when(i == 0)
    def _():
        run_ref[...] = jnp.zeros(run_ref.shape, F32)

    scores = _sigmoid(lg_ref[...].T)
    biased = scores + bias_ref[...]
    row = lax.broadcasted_iota(I32, scores.shape, 0)
    neg_inf = -jnp.inf

    def first_argmax(vals, rows):
        m = jnp.max(vals, axis=0, keepdims=True)
        idx = jnp.min(jnp.where(vals == m, rows, N_EXPERTS), axis=0, keepdims=True)
        return m, idx

    def group(a, g):
        return a[g * GROUP_SIZE:(g + 1) * GROUP_SIZE]

    gscore = []
    for g in range(N_GROUPS):
        vals = group(biased, g)
        rows = lax.broadcasted_iota(I32, vals.shape, 0) + g * GROUP_SIZE
        m1, i1 = first_argmax(vals, rows)
        m2 = jnp.max(jnp.where(rows == i1, neg_inf, vals), axis=0, keepdims=True)
        gscore.append(m1 + m2)
    parts = []
    for g in range(N_GROUPS):
        beaten = jnp.zeros((1, tr), F32)
        for g2 in range(N_GROUPS):
            if g2 == g:
                continue
            wins = (gscore[g2] > gscore[g]) | ((gscore[g2] == gscore[g]) & (g2 < g))
            beaten = beaten + jnp.where(wins, 1.0, 0.0)
        parts.append(jnp.where(beaten < TOPK_GROUPS, group(biased, g), neg_inf))
    masked = jnp.concatenate(parts, axis=0)

    sel = jnp.zeros(scores.shape, F32)
    idxs, ws = [], []
    for _ in range(TOP_K):
        _, ik = first_argmax(masked, row)
        hit = row == ik
        idxs.append(ik)
        ws.append(jnp.sum(jnp.where(hit, scores, 0.0), axis=0, keepdims=True))
        sel = jnp.where(hit, 1.0, sel)
        masked = jnp.where(hit, neg_inf, masked)
    wsum = ws[0]
    for k in range(1, TOP_K):
        wsum = wsum + ws[k]

    rr = lax.broadcasted_iota(I32, (tr, tr), 0)
    cc = lax.broadcasted_iota(I32, (tr, tr), 1)
    earlier = jnp.where(rr < cc, 1.0, 0.0).astype(BF16)
    run = run_ref[...]
    selb = sel.astype(BF16)
    before = _dot(selb, earlier) + run
    ranks = [jnp.sum(jnp.where(row == idxs[k], before, 0.0), axis=0, keepdims=True) for k in range(TOP_K)]
    run_ref[...] = run + _dot(selb, jnp.ones((tr, tr), BF16))
    cnt_ref[...] = run_ref[...]

    idx_ref[...] = _stack_rows(idxs)
    w_ref[...] = _stack_rows([w / wsum * ROUTED_SCALE for w in ws])
    rank_ref[...] = _stack_rows(ranks).astype(I32)


def _router(logits, bias):
    t, ne = logits.shape
    tr = _tile(t, 512)
    bias_b = jnp.broadcast_to(bias[:, None], (ne, tr))
    return pl.pallas_call(
        _router_kernel,
        out_shape=(jax.ShapeDtypeStruct((TOP_K, t), I32), jax.ShapeDtypeStruct((TOP_K, t), F32),
                   jax.ShapeDtypeStruct((TOP_K, t), I32), jax.ShapeDtypeStruct((ne, tr), F32)),
        grid=(t // tr,),
        in_specs=[pl.BlockSpec((tr, ne), lambda i: (i, 0)), pl.BlockSpec((ne, tr), lambda i: (0, 0))],
        out_specs=(pl.BlockSpec((TOP_K, tr), lambda i: (0, i)), pl.BlockSpec((TOP_K, tr), lambda i: (0, i)),
                   pl.BlockSpec((TOP_K, tr), lambda i: (0, i)), pl.BlockSpec((ne, tr), lambda i: (0, 0))),
        scratch_shapes=[pltpu.VMEM((ne, tr), F32)],
        compiler_params=_cparams(("arbitrary",)),
        name="router",
    )(logits, bias_b)


def _dest_kernel(idx_ref, rank_ref, pstart_ref, dest_ref):
    tr = idx_ref.shape[1]
    row = lax.broadcasted_iota(I32, (N_EXPERTS, tr), 0)
    rows_ = []
    for k in range(TOP_K):
        start = jnp.sum(jnp.where(row == idx_ref[k:k + 1, :], pstart_ref[...], 0.0), axis=0, keepdims=True)
        rows_.append(start.astype(I32) + rank_ref[k:k + 1, :])
    dest_ref[...] = _stack_rows(rows_)


def _dest(idx_t, rank_t, pstart):
    t = idx_t.shape[1]
    tr = _tile(t, 512)
    pstart_b = jnp.broadcast_to(pstart[:, None], (N_EXPERTS, tr))
    return pl.pallas_call(
        _dest_kernel,
        out_shape=jax.ShapeDtypeStruct((TOP_K, t), I32),
        grid=(t // tr,),
        in_specs=[pl.BlockSpec((TOP_K, tr), lambda i: (0, i)), pl.BlockSpec((TOP_K, tr), lambda i: (0, i)),
                  pl.BlockSpec((N_EXPERTS, tr), lambda i: (0, 0))],
        out_specs=pl.BlockSpec((TOP_K, tr), lambda i: (0, i)),
        compiler_params=_cparams(("parallel",)),
        name="dest",
    )(idx_t, rank_t, pstart_b)


def _dispatch_kernel(dest_hbm, h2_ref, xs_out, dest_smem, sem_idx, sem_rows):
    i = pl.program_id(0)
    groups = h2_ref.shape[0]
    n = groups * SUBLANES * TOP_K
    cp = pltpu.make_async_copy(dest_hbm.at[pl.ds(i * n, n)], dest_smem, sem_idx)
    cp.start()
    cp.wait()

    def row_copy(g, r, d):
        return pltpu.make_async_copy(h2_ref.at[g, pl.ds(r, 1), :], xs_out.at[pl.ds(d, 1), :], sem_rows)

    def issue(g, carry):
        for r in range(SUBLANES):
            for k in range(TOP_K):
                row_copy(g, r, dest_smem[(g * SUBLANES + r) * TOP_K + k]).start(priority=k % 2)
        return carry

    lax.fori_loop(0, groups, issue, 0)

    def drain(g, carry):
        for _ in range(SUBLANES * TOP_K):
            row_copy(0, 0, 0).wait()
        return carry

    lax.fori_loop(0, groups, drain, 0)


def _dispatch(dest_flat, h2p, nrows):
    t, half = h2p.shape
    td = _tile(t, 1024)
    return pl.pallas_call(
        _dispatch_kernel,
        out_shape=jax.ShapeDtypeStruct((nrows, half), U32),
        grid=(t // td,),
        in_specs=[pl.BlockSpec(memory_space=pl.ANY),
                  pl.BlockSpec((td // SUBLANES, SUBLANES, half), lambda i: (i, 0, 0))],
        out_specs=pl.BlockSpec(memory_space=pl.ANY),
        scratch_shapes=[pltpu.SMEM((td * TOP_K,), I32), pltpu.SemaphoreType.DMA, pltpu.SemaphoreType.DMA],
        compiler_params=_cparams(("arbitrary",)),
        name="dispatch",
    )(dest_flat, h2p.reshape(t // SUBLANES, SUBLANES, half))


def _experts_kernel(be_ref, valid_ref, xs_ref, wg_ref, wu_ref, wd_ref, ys_ref, wgb_ref, wub_ref, wdb_ref):
    blk = pl.program_id(0)
    valid = valid_ref[blk]
    used = valid > 0

    @pl.when(used & ((blk == 0) | (be_ref[blk] != be_ref[jnp.maximum(blk - 1, 0)])))
    def _():
        wgb_ref[...] = wg_ref[0].astype(BF16)
        wub_ref[...] = wu_ref[0].astype(BF16)
        wdb_ref[...] = wd_ref[0].astype(BF16)

    @pl.when(used)
    def _():
        rows = lax.broadcasted_iota(I32, (xs_ref.shape[0], 1), 0)
        x = _unpack_rows(jnp.where(rows < valid, xs_ref[...], jnp.uint32(0)))
        g = _dot(x, wgb_ref[...])
        u = _dot(x, wub_ref[...])
        a = (_silu(g) * u).astype(BF16)
        ys_ref[...] = _pack_rows(_dot(a, wdb_ref[...]))

    @pl.when(jnp.logical_not(used))
    def _():
        ys_ref[...] = jnp.zeros(ys_ref.shape, U32)


def _experts(blk_expert, blk_valid, xs, wg, wu, wd):
    p, half = xs.shape
    d = 2 * half
    de = wg.shape[2]
    nblk = p // MOE_BLOCK
    grid_spec = pltpu.PrefetchScalarGridSpec(
        num_scalar_prefetch=2,
        grid=(nblk,),
        in_specs=[
            pl.BlockSpec((MOE_BLOCK, half), lambda i, be, nu: (i, 0)),
            pl.BlockSpec((1, d, de), lambda i, be, nu: (be[i], 0, 0)),
            pl.BlockSpec((1, d, de), lambda i, be, nu: (be[i], 0, 0)),
            pl.BlockSpec((1, de, d), lambda i, be, nu: (be[i], 0, 0)),
        ],
        out_specs=pl.BlockSpec((MOE_BLOCK, half), lambda i, be, nu: (i, 0)),
        scratch_shapes=[pltpu.VMEM((d, de), BF16), pltpu.VMEM((d, de), BF16), pltpu.VMEM((de, d), BF16)],
    )
    return pl.pallas_call(
        _experts_kernel,
        out_shape=jax.ShapeDtypeStruct((p, half), U32),
        grid_spec=grid_spec,
        compiler_params=_cparams(("arbitrary",)),
        name="experts",
    )(blk_expert, blk_valid, xs, wg, wu, wd)


def _combine_kernel(dest_hbm, w_ref, ys_hbm, h2_ref, x1_ref, mod_ref, wgs_ref, wus_ref, wds_ref, o_ref,
                    dest_smem, buf, sem_idx, sem_rows):
    i = pl.program_id(0)
    tc = h2_ref.shape[0]
    n = tc * TOP_K
    cp = pltpu.make_async_copy(dest_hbm.at[pl.ds(i * n, n)], dest_smem, sem_idx)
    cp.start()
    cp.wait()

    def row_copy(g, r, k, d):
        return pltpu.make_async_copy(ys_hbm.at[pl.ds(d, 1), :], buf.at[k, g, pl.ds(r, 1), :], sem_rows)

    def issue(g, carry):
        for r in range(SUBLANES):
            for k in range(TOP_K):
                row_copy(g, r, k, dest_smem[(g * SUBLANES + r) * TOP_K + k]).start(priority=k % 2)
        return carry

    lax.fori_loop(0, tc // SUBLANES, issue, 0)

    x = _unpack_rows(h2_ref[...])
    a = (_silu(_dot(x, wgs_ref[...])) * _dot(x, wus_ref[...])).astype(BF16)
    y = _dot(a, wds_ref[...])

    def drain(t, carry):
        for k in range(TOP_K):
            row_copy(0, 0, k, 0).wait()
        return carry

    lax.fori_loop(0, tc, drain, 0)

    half = buf.shape[3]
    y_lo, y_hi = y[:, :half], y[:, half:]
    for k in range(TOP_K):
        wk = w_ref[:, k:k + 1]
        words = buf[k].reshape(tc, half)
        y_lo = y_lo + pltpu.bitcast(words << 16, F32) * wk
        y_hi = y_hi + pltpu.bitcast(words & jnp.uint32(0xFFFF0000), F32) * wk
    o_ref[...] = x1_ref[...] + mod_ref[0, 5:6, :] * jnp.concatenate([y_lo, y_hi], axis=1)


def _combine(dest_flat, w, ys, h2p, x1, mod, wgs, wus, wds, seq):
    t, d = x1.shape
    half = d // 2
    tc = _tile(seq, 512)
    de = wgs.shape[1]
    per_batch = seq // tc
    const = lambda i: (0, 0)
    return pl.pallas_call(
        _combine_kernel,
        out_shape=jax.ShapeDtypeStruct((t, d), F32),
        grid=(t // tc,),
        in_specs=[
            pl.BlockSpec(memory_space=pl.ANY),
            pl.BlockSpec((tc, TOP_K), lambda i: (i, 0)),
            pl.BlockSpec(memory_space=pl.ANY),
            pl.BlockSpec((tc, half), lambda i: (i, 0)),
            pl.BlockSpec((tc, d), lambda i: (i, 0)),
            pl.BlockSpec((1, 6, d), lambda i: (i // per_batch, 0, 0)),
            pl.BlockSpec((d, de), const), pl.BlockSpec((d, de), const), pl.BlockSpec((de, d), const),
        ],
        out_specs=pl.BlockSpec((tc, d), lambda i: (i, 0)),
        scratch_shapes=[pltpu.SMEM((tc * TOP_K,), I32), pltpu.VMEM((TOP_K, tc // SUBLANES, SUBLANES, half), U32),
                        pltpu.SemaphoreType.DMA, pltpu.SemaphoreType.DMA],
        compiler_params=_cparams(("arbitrary",)),
        name="combine",
    )(dest_flat, w, ys, h2p, x1, mod, wgs, wus, wds)


def _rope_tables(n):
    rows = n // GRID_W
    row = jnp.broadcast_to(jnp.arange(rows)[:, None], (rows, GRID_W)).reshape(-1)
    col = jnp.broadcast_to(jnp.arange(GRID_W)[None, :], (rows, GRID_W)).reshape(-1)
    freqs = ROPE_BASE ** (-jnp.arange(ROPE_FREQS, dtype=F32) / ROPE_FREQS)
    pos = jnp.stack([row, col], axis=-1).astype(F32)
    ang = pos[:, :, None] * freqs
    cos, sin = jnp.cos(ang), jnp.sin(ang)
    cos64 = jnp.concatenate([cos[:, 0], cos[:, 0], cos[:, 1], cos[:, 1]], axis=-1)
    sin64 = jnp.concatenate([-sin[:, 0], sin[:, 0], -sin[:, 1], sin[:, 1]], axis=-1)
    reps = MXU_N // HEAD_DIM
    return jnp.tile(cos64, (1, reps)), jnp.tile(sin64, (1, reps))


def _gain_rows(qn_a, kn_a, qn_b, kn_b):
    reps = MXU_N // HEAD_DIM
    ones = jnp.ones((LANES,), F32)
    kb = jnp.tile(kn_b, reps)
    qa = jnp.tile(qn_a, reps) * ATTN_SCALE
    qb = jnp.tile(qn_b, reps) * (ATTN_SCALE * LOG2E)
    mixed = jnp.concatenate([jnp.tile(kn_a, LANES // HEAD_DIM), ones])
    lat_g = jnp.stack([kb] * 4 + [qa] * 4 + [qb] * 4 + [mixed])[:, None, :]
    ctx_g = jnp.stack([kb] * 4 + [mixed])[:, None, :]
    return lat_g, ctx_g


def kernel(x, c, ctx, c_ctx, w_ada, b_ada, norm1_g, norm2_g, w_in, qnorm_a, knorm_a, sink_a, qnorm_b, knorm_b,
           lam_q1, lam_k1, lam_q2, lam_k2, subln_g, w_pa, w_pb, w_o, w_router, router_bias, w_gate_e, w_up_e,
           w_down_e, w_gate_s, w_up_s, w_down_s):
    b, s, d = x.shape
    nctx = ctx.shape[1]
    assert w_ada.shape[0] == 1 and d == 1024 and s % BLOCK == 0 and s % GRID_W == 0
    t = b * s

    cc = jnp.concatenate([c, c_ctx[None, :], jnp.zeros((16 - b - 1, d), F32)], axis=0)
    mod_all = _ada(cc, w_ada[0], b_ada[0])
    mod = mod_all[:b].reshape(b, 6, d)
    mod_c = jnp.broadcast_to(mod_all[b].reshape(1, 6, d), (b, 6, d))

    w = w_in[0]
    ka_w, va_w = w[:, 0:128], w[:, 128:256]
    kb_w, vb_w = w[:, 256:1280], w[:, 1280:2304]
    qa_w, qb_w = w[:, 2304:3328], w[:, 3328:4352]
    gate_w = w[:, 4352:6400].astype(BF16)
    w_lat = jnp.concatenate([kb_w, qa_w, qb_w, ka_w, va_w, vb_w], axis=1).astype(BF16)
    w_ctx = jnp.concatenate([kb_w, ka_w, va_w, vb_w], axis=1).astype(BF16)

    lat_g, ctx_g = _gain_rows(qnorm_a[0], knorm_a[0], qnorm_b[0], knorm_b[0])
    hid = jnp.arange(MXU_N) // HEAD_DIM
    seg = jnp.where(hid[:, None] == hid[None, :], 1.0 / HEAD_DIM, 0.0).astype(BF16)
    cos_t, sin_t = _rope_tables(s)
    n1 = norm1_g[0].reshape(1, d)
    n2 = norm2_g[0].reshape(1, d)

    lat, vt = _proj(x, mod, n1, w_lat, lat_g, seg, cos_t, sin_t, nqk=LAT_NQK, use_rope=True)
    ctxp, vtx = _proj(ctx, mod_c, n1, w_ctx, ctx_g, seg, cos_t[:nctx], sin_t[:nctx], nqk=CTX_NQK, use_rope=False)

    oa = _attn_a(sink_a[0], lat, ctxp)
    lamv = jnp.stack([lam_q1[0], lam_k1[0], lam_q2[0], lam_k2[0]])
    bound = (HEAD_DIM * ATTN_SCALE * LOG2E * 1.01) * jnp.max(jnp.abs(qnorm_b[0])) * jnp.max(jnp.abs(knorm_b[0]))
    ob = _attn_b(bound.reshape(1), lamv, subln_g[0].reshape(LANES, 1), lat, vt, ctxp, vtx)

    x1, h2p, logits = _merge(x, mod, n1, n2, oa, ob, gate_w, w_pa[0].astype(BF16), w_pb[0].astype(BF16),
                             w_o[0].astype(BF16), w_router[0].astype(BF16))
    x1 = x1.reshape(t, d)
    h2p = h2p.reshape(t, d // 2)
    logits = logits.reshape(t, N_EXPERTS)

    idx_t, wts_t, rank_t, counts = _router(logits, router_bias[0])
    wts = wts_t.T

    cnt = counts[:, 0].astype(I32)
    padded = (cnt + MOE_BLOCK - 1) // MOE_BLOCK * MOE_BLOCK
    pend = jnp.cumsum(padded)
    pstart = pend - padded
    nblk = -(-(t * TOP_K) // MOE_BLOCK) + N_EXPERTS
    blk_row0 = jnp.arange(nblk, dtype=I32) * MOE_BLOCK
    blk_expert = jnp.minimum(jnp.sum((pend[None, :] <= blk_row0[:, None]).astype(I32), axis=1), N_EXPERTS - 1)
    blk_valid = jnp.clip(pstart[blk_expert] + cnt[blk_expert] - blk_row0, 0, MOE_BLOCK).astype(I32)

    dest = _dest(idx_t, rank_t, pstart.astype(F32)).T.reshape(t * TOP_K)
    xs = _dispatch(dest, h2p, nblk * MOE_BLOCK)
    ys = _experts(blk_expert, blk_valid, xs, w_gate_e[0], w_up_e[0], w_down_e[0])
    out = _combine(dest, wts, ys, h2p, x1, mod, w_gate_s[0].astype(BF16), w_up_s[0].astype(BF16),
                   w_down_s[0].astype(BF16), s)
    return out.reshape(b, s, d)
```

```python
import functools
import math

import jax
import jax.numpy as jnp
from jax import lax
from jax.experimental import pallas as pl
from jax.experimental.pallas import tpu as pltpu

F32 = jnp.float32
BF16 = jnp.bfloat16
I32 = jnp.int32
U32 = jnp.uint32

HEAD_DIM = 64
GRID_W = 64
ROPE_FREQS = HEAD_DIM // 4
ROPE_BASE = 10000.0
EPS = 1e-6
ATTN_SCALE = HEAD_DIM ** -0.5
BLOCK = 128
A_Q_HEADS = 16
A_KV_HEADS = 2
A_GROUP = A_Q_HEADS // A_KV_HEADS
B_HEADS = 8
N_EXPERTS = 256
TOP_K = 8
N_GROUPS = 8
TOPK_GROUPS = 4
GROUP_SIZE = N_EXPERTS // N_GROUPS
ROUTED_SCALE = 2.5
MOE_BLOCK = 512
LAM_INIT = 0.8 - 0.6 * math.exp(-0.3 * 0)

LANES = 128
SUBLANES = 8
MXU_N = 256
VMEM_LIMIT = 56 * 1024 * 1024
NEG_BIG = -1e30
LOG2E = math.log2(math.e)
MAX_FIXED_SHIFT = 60.0

LAT_KB, LAT_QA, LAT_QB, LAT_KA, LAT_VA = 0, 8, 16, 24, 25
LAT_NQK = 13
CTX_KB, CTX_KA, CTX_VA = 0, 8, 9
CTX_NQK = 5


ADA_COLS = 512
PROJ_TOKENS = 512
ATTN_B_QUERIES = 512
ATTN_B_KEYS = 512
MERGE_TOKENS = 256
ROUTER_TOKENS = 512
DISPATCH_TOKENS = 2048
COMBINE_TOKENS = 1024


def _tile(n, pref):
    return pref if n % pref == 0 else n


def _cparams(sem):
    return pltpu.CompilerParams(dimension_semantics=sem, vmem_limit_bytes=VMEM_LIMIT)


def _dot(a, b):
    return jnp.dot(a, b, preferred_element_type=F32)


def _silu(x):
    return x * (1.0 / (1.0 + jnp.exp(-x)))


def _sigmoid(x):
    return 1.0 / (1.0 + jnp.exp(-x))


def _stack_rows(rows_):
    n, m = len(rows_), rows_[0].shape[1]
    slot = lax.broadcasted_iota(I32, (n, m), 0)
    out = jnp.broadcast_to(rows_[0], (n, m))
    for k in range(1, n):
        out = jnp.where(slot == k, rows_[k], out)
    return out


def _modulated_norm(xf, g_row, shift_row, scale_row):
    ms = jnp.mean(xf * xf, axis=-1, keepdims=True)
    y = xf * lax.rsqrt(ms + EPS) * g_row
    return y * (1.0 + scale_row) + shift_row


def _ada_kernel(c_ref, w_ref, b_ref, o_ref):
    o_ref[...] = _dot(_silu(c_ref[...]).astype(BF16), w_ref[...].astype(BF16)) + b_ref[...]


def _ada(cc, w_ada, b_ada):
    rows, d = cc.shape
    n = w_ada.shape[1]
    tn = _tile(n, ADA_COLS)
    return pl.pallas_call(
        _ada_kernel,
        out_shape=jax.ShapeDtypeStruct((rows, n), F32),
        grid=(n // tn,),
        in_specs=[pl.BlockSpec((rows, d), lambda j: (0, 0)),
                  pl.BlockSpec((d, tn), lambda j: (0, j)),
                  pl.BlockSpec((1, tn), lambda j: (0, j))],
        out_specs=pl.BlockSpec((rows, tn), lambda j: (0, j)),
        compiler_params=_cparams(("parallel",)),
        name="ada",
    )(cc, w_ada, b_ada.reshape(1, n))


def _proj_kernel(x_ref, mod_ref, n1_ref, w_ref, gain_ref, seg_ref, cos_ref, sin_ref, o_ref, vt_ref,
                 *, nqk, use_rope):
    nj = w_ref.shape[1] // MXU_N
    h = _modulated_norm(x_ref[0], n1_ref[...], mod_ref[0, 0:1, :], mod_ref[0, 1:2, :]).astype(BF16)
    lane = lax.broadcasted_iota(I32, (1, MXU_N), 1)
    first = (lane % (2 * ROPE_FREQS)) < ROPE_FREQS

    def matmul(j):
        return _dot(h, w_ref[:, j * MXU_N:(j + 1) * MXU_N])

    def finish(j, acc):
        if j >= nqk:
            vt_ref[0, (j - nqk) * MXU_N:(j - nqk + 1) * MXU_N, :] = acc.T.astype(BF16)
            return
        ms = _dot((acc * acc).astype(BF16), seg_ref[...])
        y = acc * lax.rsqrt(ms + EPS) * gain_ref[j]
        if use_rope:
            rot = jnp.where(first, pltpu.roll(y, MXU_N - ROPE_FREQS, 1), pltpu.roll(y, ROPE_FREQS, 1))
            y = y * cos_ref[...] + rot * sin_ref[...]
        if j == nqk - 1:
            y = jnp.where(lane < LANES, y, acc)
        o_ref[0, :, j * MXU_N:(j + 1) * MXU_N] = y.astype(BF16)

    acc = matmul(0)
    for j in range(nj):
        nxt = matmul(j + 1) if j + 1 < nj else None
        finish(j, acc)
        acc = nxt


def _proj(x, mod, n1, w, gains, seg, cos_t, sin_t, *, nqk, use_rope):
    b, s, d = x.shape
    ncols = w.shape[1]
    tm = _tile(s, PROJ_TOKENS)
    nj = ncols // MXU_N
    const2 = lambda bi, si: (0, 0)
    return pl.pallas_call(
        functools.partial(_proj_kernel, nqk=nqk, use_rope=use_rope),
        out_shape=(jax.ShapeDtypeStruct((b, s, nqk * MXU_N), BF16),
                   jax.ShapeDtypeStruct((b, (nj - nqk) * MXU_N, s), BF16)),
        grid=(b, s // tm),
        in_specs=[
            pl.BlockSpec((1, tm, d), lambda bi, si: (bi, si, 0)),
            pl.BlockSpec((1, 6, d), lambda bi, si: (bi, 0, 0)),
            pl.BlockSpec((1, d), const2),
            pl.BlockSpec((d, ncols), const2),
            pl.BlockSpec((nqk, 1, MXU_N), lambda bi, si: (0, 0, 0)),
            pl.BlockSpec((MXU_N, MXU_N), const2),
            pl.BlockSpec((tm, MXU_N), lambda bi, si: (si, 0)),
            pl.BlockSpec((tm, MXU_N), lambda bi, si: (si, 0)),
        ],
        out_specs=(pl.BlockSpec((1, tm, nqk * MXU_N), lambda bi, si: (bi, si, 0)),
                   pl.BlockSpec((1, (nj - nqk) * MXU_N, tm), lambda bi, si: (bi, 0, si))),
        compiler_params=_cparams(("parallel", "parallel")),
        name="proj_rope" if use_rope else "proj_ctx",
    )(x, mod, n1, w, gains, seg, cos_t, sin_t)


def _attn_a_kernel(sink_ref, q_ref, kp_ref, kc_ref, kn_ref, vp_ref, vc_ref, vn_ref, kx_ref, vx_ref, o_ref):
    i = pl.program_id(1)
    nb = pl.num_programs(1)
    nctx = kx_ref.shape[1]
    span = 3 * BLOCK + nctx
    kall = jnp.concatenate([kp_ref[0], kc_ref[0], kn_ref[0], kx_ref[0]], axis=0)
    vall = jnp.concatenate([vp_ref[0], vc_ref[0], vn_ref[0], vx_ref[0]], axis=0)

    c = lax.broadcasted_iota(I32, (span, BLOCK), 0)
    r = lax.broadcasted_iota(I32, (span, BLOCK), 1)
    prev_ok = (c < BLOCK) & (c >= r) & (i > 0)
    cur_ok = (c >= BLOCK) & (c < 2 * BLOCK)
    next_ok = (c >= 2 * BLOCK) & (c < 3 * BLOCK) & (c - 2 * BLOCK <= r) & (i < nb - 1)
    valid = prev_ok | cur_ok | next_ok | (c >= 3 * BLOCK)
    bias = jnp.where(valid, 0.0, NEG_BIG)

    vt_all = vall.astype(F32).T
    zeros = jnp.zeros((HEAD_DIM, BLOCK), F32)
    pair_t = [q_ref[0, :, j * LANES:(j + 1) * LANES].astype(F32).T for j in range(A_Q_HEADS // 2)]
    for kv in range(A_KV_HEADS):
        qts = []
        for g in range(A_GROUP):
            hh = kv * A_GROUP + g
            mine = pair_t[hh // 2][(hh % 2) * HEAD_DIM:(hh % 2 + 1) * HEAD_DIM]
            qts.append(jnp.concatenate([mine, zeros] if kv == 0 else [zeros, mine], axis=0).astype(BF16))
        s_all = _dot(kall, jnp.concatenate(qts, axis=1))
        ps, inv_ls = [], []
        for g in range(A_GROUP):
            sink = sink_ref[kv, g]
            s = s_all[:, g * BLOCK:(g + 1) * BLOCK] + bias
            m = jnp.maximum(jnp.max(s, axis=0, keepdims=True), sink)
            p = jnp.exp(s - m)
            inv_ls.append(1.0 / (jnp.sum(p, axis=0, keepdims=True) + jnp.exp(sink - m)))
            ps.append(p.astype(BF16))
        vt = vt_all[kv * HEAD_DIM:(kv + 1) * HEAD_DIM].astype(BF16)
        o_all = _dot(vt, jnp.concatenate(ps, axis=1))
        os_ = [o_all[:, g * BLOCK:(g + 1) * BLOCK] * inv_ls[g] for g in range(A_GROUP)]
        for pair in range(A_GROUP // 2):
            both = jnp.concatenate([os_[2 * pair], os_[2 * pair + 1]], axis=0)
            col = (kv * (A_GROUP // 2) + pair) * LANES
            o_ref[0, :, col:col + LANES] = both.T.astype(BF16)


def _attn_a(sink, lat, ctxp):
    b, s, _ = lat.shape
    nctx = ctxp.shape[1]
    nb = s // BLOCK
    width = A_Q_HEADS * HEAD_DIM

    def kspec(col, shift):
        return pl.BlockSpec((1, BLOCK, LANES),
                            lambda bi, i: (bi, jnp.clip(i + shift, 0, nb - 1), col))

    return pl.pallas_call(
        _attn_a_kernel,
        out_shape=jax.ShapeDtypeStruct((b, s, width), BF16),
        grid=(b, nb),
        in_specs=[
            pl.BlockSpec(memory_space=pltpu.SMEM),
            pl.BlockSpec((1, BLOCK, width), lambda bi, i: (bi, i, LAT_QA * LANES // width)),
            kspec(LAT_KA, -1), kspec(LAT_KA, 0), kspec(LAT_KA, 1),
            kspec(LAT_VA, -1), kspec(LAT_VA, 0), kspec(LAT_VA, 1),
            pl.BlockSpec((1, nctx, LANES), lambda bi, i: (bi, 0, CTX_KA)),
            pl.BlockSpec((1, nctx, LANES), lambda bi, i: (bi, 0, CTX_VA)),
        ],
        out_specs=pl.BlockSpec((1, BLOCK, width), lambda bi, i: (bi, i, 0)),
        compiler_params=_cparams(("parallel", "parallel")),
        name="attn_a",
    )(sink, lat, lat, lat, lat, lat, lat, lat, ctxp, ctxp)


def _attn_b_kernel(bound_ref, lamv_ref, sub_ref, q_ref, k_ref, vt_ref, kx_ref, vtx_ref, o_ref, acc_ref, l_ref,
                   s0_ref, s1_ref, sx_ref, *, tk):
    tq = q_ref.shape[1]
    s_len = k_ref.shape[1]
    qt = q_ref[0].astype(F32).T
    row = lax.broadcasted_iota(I32, (LANES, tq), 0)
    qts = [jnp.where(row < HEAD_DIM, qt, 0.0).astype(BF16), jnp.where(row >= HEAD_DIM, qt, 0.0).astype(BF16)]
    acc_ref[...] = jnp.zeros(acc_ref.shape, F32)
    n = s_len // tk
    bound = bound_ref[0]

    def kchunk(i):
        return k_ref[0, pl.ds(pl.multiple_of(i * tk, tk), tk), :]

    def vchunk(i):
        return vt_ref[0, :, pl.ds(pl.multiple_of(i * tk, tk), tk)]

    def run(fixed):
        def scores(j, kc, s_ref):
            s = _dot(kc, qts[j])
            s_ref[j] = s
            return jnp.zeros((1, tq), F32) if fixed else jnp.max(s, axis=0, keepdims=True)

        def accumulate(j, s_ref, cm, vtc, stat):
            m_old, l_old = stat
            if fixed:
                p = jnp.exp2(s_ref[j] - bound)
                acc_ref[j] += _dot(vtc, p.astype(BF16))
                return m_old, l_old + jnp.sum(p, axis=0, keepdims=True)
            m_new = jnp.maximum(m_old, cm)
            alpha = jnp.exp2(m_old - m_new)
            p = jnp.exp2(s_ref[j] - m_new)
            acc_ref[j] = alpha * acc_ref[j] + _dot(vtc, p.astype(BF16))
            return m_new, alpha * l_old + jnp.sum(p, axis=0, keepdims=True)

        def half_step(k_next, s_next, s_cur, cms, vtc, stats):
            new_cms, new_stats = [], []
            for j in range(2):
                new_cms.append(scores(j, k_next, s_next))
                new_stats.append(accumulate(j, s_cur, cms[j], vtc, stats[j]))
            return tuple(new_cms), tuple(new_stats)

        m0 = jnp.zeros((1, tq), F32) if fixed else jnp.full((1, tq), NEG_BIG, F32)
        stats = ((m0, jnp.zeros((1, tq), F32)),) * 2
        cmx = tuple(scores(j, kx_ref[0], sx_ref) for j in range(2))
        cms, stats = half_step(kchunk(0), s0_ref, sx_ref, cmx, vtx_ref[0], stats)

        def body(pair, carry):
            cms, stats = carry
            i = 2 * pair
            cms, stats = half_step(kchunk(i + 1), s1_ref, s0_ref, cms, vchunk(i), stats)
            return half_step(kchunk(i + 2), s0_ref, s1_ref, cms, vchunk(i + 1), stats)

        cms, stats = lax.fori_loop(0, n // 2 - 1, body, (cms, stats))
        cms, stats = half_step(kchunk(n - 1), s1_ref, s0_ref, cms, vchunk(n - 2), stats)
        for j in range(2):
            l_ref[j] = accumulate(j, s1_ref, cms[j], vchunk(n - 1), stats[j])[1]

    pl.when(bound <= MAX_FIXED_SHIFT)(lambda: run(True))
    pl.when(bound > MAX_FIXED_SHIFT)(lambda: run(False))


    lv = lamv_ref[...]
    lam = (jnp.exp(jnp.sum(lv[0:1] * lv[1:2], axis=-1, keepdims=True))
           - jnp.exp(jnp.sum(lv[2:3] * lv[3:4], axis=-1, keepdims=True)) + LAM_INIT)
    o = acc_ref[0] * (1.0 / l_ref[0]) - acc_ref[1] * (lam / l_ref[1])
    ms = jnp.mean(o * o, axis=0, keepdims=True)
    o = o * lax.rsqrt(ms + EPS) * (sub_ref[...] * (1.0 - LAM_INIT))
    o_ref[0] = o.T.astype(BF16)


def _attn_b(bound, lamv, subln_col, lat, vt, ctxp, vtx):
    b, s, _ = lat.shape
    nctx = ctxp.shape[1]
    tq = _tile(s, ATTN_B_QUERIES)
    tk = min(ATTN_B_KEYS, s // 2)
    assert s % (2 * tk) == 0
    return pl.pallas_call(
        functools.partial(_attn_b_kernel, tk=tk),
        out_shape=jax.ShapeDtypeStruct((b, s, B_HEADS * LANES), BF16),
        grid=(b, B_HEADS, s // tq),
        in_specs=[
            pl.BlockSpec(memory_space=pltpu.SMEM),
            pl.BlockSpec((4, HEAD_DIM), lambda bi, h, qi: (0, 0)),
            pl.BlockSpec((LANES, 1), lambda bi, h, qi: (0, 0)),
            pl.BlockSpec((1, tq, LANES), lambda bi, h, qi: (bi, qi, LAT_QB + h)),
            pl.BlockSpec((1, s, LANES), lambda bi, h, qi: (bi, 0, LAT_KB + h)),
            pl.BlockSpec((1, LANES, s), lambda bi, h, qi: (bi, h, 0)),
            pl.BlockSpec((1, nctx, LANES), lambda bi, h, qi: (bi, 0, CTX_KB + h)),
            pl.BlockSpec((1, LANES, nctx), lambda bi, h, qi: (bi, h, 0)),
        ],
        out_specs=pl.BlockSpec((1, tq, LANES), lambda bi, h, qi: (bi, qi, h)),
        scratch_shapes=[pltpu.VMEM((2, LANES, tq), F32), pltpu.VMEM((2, 1, tq), F32),
                        pltpu.VMEM((2, tk, tq), F32), pltpu.VMEM((2, tk, tq), F32),
                        pltpu.VMEM((2, nctx, tq), F32)],
        compiler_params=_cparams(("parallel", "parallel", "arbitrary")),
        name="attn_b",
    )(bound, lamv, subln_col, lat, lat, vt, ctxp, vtx)


def _pack_rows(h):
    n = h.shape[1] // 2
    bits = pltpu.bitcast(h.astype(BF16).astype(F32), U32)
    return (bits[:, :n] >> 16) | (bits[:, n:] & jnp.uint32(0xFFFF0000))


def _unpack_rows(w):
    lo = pltpu.bitcast(w << 16, F32)
    hi = pltpu.bitcast(w & jnp.uint32(0xFFFF0000), F32)
    return jnp.concatenate([lo, hi], axis=1).astype(BF16)


def _merge_kernel(x_ref, mod_ref, n1_ref, n2_ref, oa_ref, ob_ref, wg_ref, wpa_ref, wpb_ref, wo_ref, wr_ref,
                  x1_ref, h2_ref, lg_ref):
    d = x_ref.shape[2]
    xf = x_ref[0]
    h = _modulated_norm(xf, n1_ref[...], mod_ref[0, 0:1, :], mod_ref[0, 1:2, :]).astype(BF16)
    gates = _sigmoid(_dot(h, wg_ref[...]))
    ya = _dot(oa_ref[0], wpa_ref[...])
    yb = _dot(ob_ref[0], wpb_ref[...])
    z = gates[:, :d] * ya + gates[:, d:] * yb
    x1 = xf + mod_ref[0, 2:3, :] * _dot(z.astype(BF16), wo_ref[...])
    x1_ref[0] = x1
    h2 = _modulated_norm(x1, n2_ref[...], mod_ref[0, 3:4, :], mod_ref[0, 4:5, :])
    lg_ref[0] = _dot(h2.astype(BF16), wr_ref[...])
    h2_ref[0] = _pack_rows(h2)


def _merge(x, mod, n1, n2, oa, ob, wg, wpa, wpb, wo, wr):
    b, s, d = x.shape
    tm = _tile(s, MERGE_TOKENS)
    ne = wr.shape[1]
    const = lambda bi, si: (0, 0)
    row = lambda bi, si: (bi, si, 0)
    return pl.pallas_call(
        _merge_kernel,
        out_shape=(jax.ShapeDtypeStruct((b, s, d), F32),
                   jax.ShapeDtypeStruct((b, s, d // 2), U32),
                   jax.ShapeDtypeStruct((b, s, ne), F32)),
        grid=(b, s // tm),
        in_specs=[
            pl.BlockSpec((1, tm, d), row),
            pl.BlockSpec((1, 6, d), lambda bi, si: (bi, 0, 0)),
            pl.BlockSpec((1, d), const), pl.BlockSpec((1, d), const),
            pl.BlockSpec((1, tm, d), row), pl.BlockSpec((1, tm, d), row),
            pl.BlockSpec((d, 2 * d), const), pl.BlockSpec((d, d), const),
            pl.BlockSpec((d, d), const), pl.BlockSpec((d, d), const),
            pl.BlockSpec((d, ne), const),
        ],
        out_specs=(pl.BlockSpec((1, tm, d), row), pl.BlockSpec((1, tm, d // 2), row),
                   pl.BlockSpec((1, tm, ne), row)),
        compiler_params=_cparams(("parallel", "parallel")),
        name="merge",
    )(x, mod, n1, n2, oa, ob, wg, wpa, wpb, wo, wr)


def _router_kernel(lg_ref, bias_ref, idx_ref, w_ref, rank_ref, cnt_ref, run_ref):
    i = pl.program_id(0)
    tr = lg_ref.shape[0]

    @pl.when(i == 0)
    def _():
        run_ref[...] = jnp.zeros(run_ref.shape, F32)

    scores = _sigmoid(lg_ref[...].T)
    biased = scores + bias_ref[...]
    row = lax.broadcasted_iota(I32, scores.shape, 0)
    neg_inf = -jnp.inf

    def first_argmax(vals, rows):
        m = jnp.max(vals, axis=0, keepdims=True)
        idx = jnp.min(jnp.where(vals == m, rows, N_EXPERTS), axis=0, keepdims=True)
        return m, idx

    def group(a, g):
        return a[g * GROUP_SIZE:(g + 1) * GROUP_SIZE]

    gscore = []
    for g in range(N_GROUPS):
        vals = group(biased, g)
        rows = lax.broadcasted_iota(I32, vals.shape, 0) + g * GROUP_SIZE
        m1, i1 = first_argmax(vals, rows)
        m2 = jnp.max(jnp.where(rows == i1, neg_inf, vals), axis=0, keepdims=True)
        gscore.append(m1 + m2)
    parts = []
    for g in range(N_GROUPS):
        beaten = jnp.zeros((1, tr), F32)
        for g2 in range(N_GROUPS):
            if g2 == g:
                continue
            wins = (gscore[g2] > gscore[g]) | ((gscore[g2] == gscore[g]) & (g2 < g))
            beaten = beaten + jnp.where(wins, 1.0, 0.0)
        parts.append(jnp.where(beaten < TOPK_GROUPS, group(biased, g), neg_inf))
    masked = jnp.concatenate(parts, axis=0)

    sel = jnp.zeros(scores.shape, F32)
    idxs, ws = [], []
    for _ in range(TOP_K):
        _, ik = first_argmax(masked, row)
        hit = row == ik
        idxs.append(ik)
        ws.append(jnp.sum(jnp.where(hit, scores, 0.0), axis=0, keepdims=True))
        sel = jnp.where(hit, 1.0, sel)
        masked = jnp.where(hit, neg_inf, masked)
    wsum = ws[0]
    for k in range(1, TOP_K):
        wsum = wsum + ws[k]

    rr = lax.broadcasted_iota(I32, (tr, tr), 0)
    cc = lax.broadcasted_iota(I32, (tr, tr), 1)
    earlier = jnp.where(rr < cc, 1.0, 0.0).astype(BF16)
    run = run_ref[...]
    selb = sel.astype(BF16)
    before = _dot(selb, earlier) + run
    ranks = [jnp.sum(jnp.where(row == idxs[k], before, 0.0), axis=0, keepdims=True) for k in range(TOP_K)]
    run_ref[...] = run + _dot(selb, jnp.ones((tr, tr), BF16))
    cnt_ref[...] = run_ref[...]

    idx_ref[...] = _stack_rows(idxs)
    w_ref[...] = _stack_rows([w / wsum * ROUTED_SCALE for w in ws])
    rank_ref[...] = _stack_rows(ranks).astype(I32)


def _router(logits, bias):
    t, ne = logits.shape
    tr = _tile(t, ROUTER_TOKENS)
    bias_b = jnp.broadcast_to(bias[:, None], (ne, tr))
    return pl.pallas_call(
        _router_kernel,
        out_shape=(jax.ShapeDtypeStruct((TOP_K, t), I32), jax.ShapeDtypeStruct((TOP_K, t), F32),
                   jax.ShapeDtypeStruct((TOP_K, t), I32), jax.ShapeDtypeStruct((ne, tr), F32)),
        grid=(t // tr,),
        in_specs=[pl.BlockSpec((tr, ne), lambda i: (i, 0)), pl.BlockSpec((ne, tr), lambda i: (0, 0))],
        out_specs=(pl.BlockSpec((TOP_K, tr), lambda i: (0, i)), pl.BlockSpec((TOP_K, tr), lambda i: (0, i)),
                   pl.BlockSpec((TOP_K, tr), lambda i: (0, i)), pl.BlockSpec((ne, tr), lambda i: (0, 0))),
        scratch_shapes=[pltpu.VMEM((ne, tr), F32)],
        compiler_params=_cparams(("arbitrary",)),
        name="router",
    )(logits, bias_b)


def _dest_kernel(idx_ref, rank_ref, pstart_ref, dest_ref):
    tr = idx_ref.shape[1]
    row = lax.broadcasted_iota(I32, (N_EXPERTS, tr), 0)
    rows_ = []
    for k in range(TOP_K):
        start = jnp.sum(jnp.where(row == idx_ref[k:k + 1, :], pstart_ref[...], 0.0), axis=0, keepdims=True)
        rows_.append(start.astype(I32) + rank_ref[k:k + 1, :])
    dest_ref[...] = _stack_rows(rows_)


def _dest(idx_t, rank_t, pstart):
    t = idx_t.shape[1]
    tr = _tile(t, ROUTER_TOKENS)
    pstart_b = jnp.broadcast_to(pstart[:, None], (N_EXPERTS, tr))
    return pl.pallas_call(
        _dest_kernel,
        out_shape=jax.ShapeDtypeStruct((TOP_K, t), I32),
        grid=(t // tr,),
        in_specs=[pl.BlockSpec((TOP_K, tr), lambda i: (0, i)), pl.BlockSpec((TOP_K, tr), lambda i: (0, i)),
                  pl.BlockSpec((N_EXPERTS, tr), lambda i: (0, 0))],
        out_specs=pl.BlockSpec((TOP_K, tr), lambda i: (0, i)),
        compiler_params=_cparams(("parallel",)),
        name="dest",
    )(idx_t, rank_t, pstart_b)


def _dispatch_kernel(dest_hbm, h2_ref, xs_out, dest_smem, sem_idx, sem_rows):
    i = pl.program_id(0)
    groups = h2_ref.shape[0]
    n = groups * SUBLANES * TOP_K
    cp = pltpu.make_async_copy(dest_hbm.at[pl.ds(i * n, n)], dest_smem, sem_idx)
    cp.start()
    cp.wait()

    def row_copy(g, r, d):
        return pltpu.make_async_copy(h2_ref.at[g, pl.ds(r, 1), :], xs_out.at[pl.ds(d, 1), :], sem_rows)

    def issue(g, carry):
        for r in range(SUBLANES):
            for k in range(TOP_K):
                row_copy(g, r, dest_smem[(g * SUBLANES + r) * TOP_K + k]).start(priority=k % 2)
        return carry

    lax.fori_loop(0, groups, issue, 0)

    def drain(g, carry):
        for _ in range(SUBLANES * TOP_K):
            row_copy(0, 0, 0).wait()
        return carry

    lax.fori_loop(0, groups, drain, 0)


def _dispatch(dest_flat, h2p, nrows):
    t, half = h2p.shape
    td = _tile(t, DISPATCH_TOKENS)
    return pl.pallas_call(
        _dispatch_kernel,
        out_shape=jax.ShapeDtypeStruct((nrows, half), U32),
        grid=(t // td,),
        in_specs=[pl.BlockSpec(memory_space=pl.ANY),
                  pl.BlockSpec((td // SUBLANES, SUBLANES, half), lambda i: (i, 0, 0))],
        out_specs=pl.BlockSpec(memory_space=pl.ANY),
        scratch_shapes=[pltpu.SMEM((td * TOP_K,), I32), pltpu.SemaphoreType.DMA, pltpu.SemaphoreType.DMA],
        compiler_params=_cparams(("arbitrary",)),
        name="dispatch",
    )(dest_flat, h2p.reshape(t // SUBLANES, SUBLANES, half))


def _experts_kernel(be_ref, valid_ref, src_ref, xs_ref, wg_ref, wu_ref, wd_ref, ys_ref, wgb_ref, wub_ref,
                    wdb_ref):
    del src_ref
    blk = pl.program_id(0)
    valid = valid_ref[blk]
    used = valid > 0

    @pl.when(used & ((blk == 0) | (be_ref[blk] != be_ref[jnp.maximum(blk - 1, 0)])))
    def _():
        wgb_ref[...] = wg_ref[0].astype(BF16)
        wub_ref[...] = wu_ref[0].astype(BF16)
        wdb_ref[...] = wd_ref[0].astype(BF16)

    @pl.when(used)
    def _():
        rows = lax.broadcasted_iota(I32, (xs_ref.shape[0], 1), 0)
        x = _unpack_rows(jnp.where(rows < valid, xs_ref[...], jnp.uint32(0)))
        g = _dot(x, wgb_ref[...])
        u = _dot(x, wub_ref[...])
        a = (_silu(g) * u).astype(BF16)
        ys_ref[...] = _pack_rows(_dot(a, wdb_ref[...]))


def _experts(blk_expert, blk_valid, blk_src, xs, wg, wu, wd):
    p, half = xs.shape
    d = 2 * half
    de = wg.shape[2]
    nblk = p // MOE_BLOCK
    grid_spec = pltpu.PrefetchScalarGridSpec(
        num_scalar_prefetch=3,
        grid=(nblk,),
        in_specs=[
            pl.BlockSpec((MOE_BLOCK, half), lambda i, be, nv, src: (src[i], 0)),
            pl.BlockSpec((1, d, de), lambda i, be, nv, src: (be[i], 0, 0)),
            pl.BlockSpec((1, d, de), lambda i, be, nv, src: (be[i], 0, 0)),
            pl.BlockSpec((1, de, d), lambda i, be, nv, src: (be[i], 0, 0)),
        ],
        out_specs=pl.BlockSpec((MOE_BLOCK, half), lambda i, be, nv, src: (src[i], 0)),
        scratch_shapes=[pltpu.VMEM((d, de), BF16), pltpu.VMEM((d, de), BF16), pltpu.VMEM((de, d), BF16)],
    )
    return pl.pallas_call(
        _experts_kernel,
        out_shape=jax.ShapeDtypeStruct((p, half), U32),
        grid_spec=grid_spec,
        compiler_params=_cparams(("arbitrary",)),
        name="experts",
    )(blk_expert, blk_valid, blk_src, xs, wg, wu, wd)


def _combine_kernel(dest_hbm, w_ref, ys_hbm, h2_ref, x1_ref, mod_ref, wgs_ref, wus_ref, wds_ref, o_ref,
                    dest_smem, buf, sem_idx, sem_rows):
    i = pl.program_id(0)
    tc = h2_ref.shape[0]
    n = tc * TOP_K
    cp = pltpu.make_async_copy(dest_hbm.at[pl.ds(i * n, n)], dest_smem, sem_idx)
    cp.start()
    cp.wait()

    def row_copy(g, r, k, d):
        return pltpu.make_async_copy(ys_hbm.at[pl.ds(d, 1), :], buf.at[k, g, pl.ds(r, 1), :], sem_rows)

    def issue(g, carry):
        for r in range(SUBLANES):
            for k in range(TOP_K):
                row_copy(g, r, k, dest_smem[(g * SUBLANES + r) * TOP_K + k]).start(priority=k % 2)
        return carry

    lax.fori_loop(0, tc // SUBLANES, issue, 0)

    x = _unpack_rows(h2_ref[...])
    a = (_silu(_dot(x, wgs_ref[...])) * _dot(x, wus_ref[...])).astype(BF16)
    y = _dot(a, wds_ref[...])

    def drain(t, carry):
        for k in range(TOP_K):
            row_copy(0, 0, k, 0).wait()
        return carry

    lax.fori_loop(0, tc, drain, 0)

    half = buf.shape[3]
    y_lo, y_hi = y[:, :half], y[:, half:]
    for k in range(TOP_K):
        wk = w_ref[:, k:k + 1]
        words = buf[k].reshape(tc, half)
        y_lo = y_lo + pltpu.bitcast(words << 16, F32) * wk
        y_hi = y_hi + pltpu.bitcast(words & jnp.uint32(0xFFFF0000), F32) * wk
    o_ref[...] = x1_ref[...] + mod_ref[0, 5:6, :] * jnp.concatenate([y_lo, y_hi], axis=1)


def _combine(dest_flat, w, ys, h2p, x1, mod, wgs, wus, wds, seq):
    t, d = x1.shape
    half = d // 2
    tc = _tile(seq, COMBINE_TOKENS)
    de = wgs.shape[1]
    per_batch = seq // tc
    const = lambda i: (0, 0)
    return pl.pallas_call(
        _combine_kernel,
        out_shape=jax.ShapeDtypeStruct((t, d), F32),
        grid=(t // tc,),
        in_specs=[
            pl.BlockSpec(memory_space=pl.ANY),
            pl.BlockSpec((tc, TOP_K), lambda i: (i, 0)),
            pl.BlockSpec(memory_space=pl.ANY),
            pl.BlockSpec((tc, half), lambda i: (i, 0)),
            pl.BlockSpec((tc, d), lambda i: (i, 0)),
            pl.BlockSpec((1, 6, d), lambda i: (i // per_batch, 0, 0)),
            pl.BlockSpec((d, de), const), pl.BlockSpec((d, de), const), pl.BlockSpec((de, d), const),
        ],
        out_specs=pl.BlockSpec((tc, d), lambda i: (i, 0)),
        scratch_shapes=[pltpu.SMEM((tc * TOP_K,), I32), pltpu.VMEM((TOP_K, tc // SUBLANES, SUBLANES, half), U32),
                        pltpu.SemaphoreType.DMA, pltpu.SemaphoreType.DMA],
        compiler_params=_cparams(("arbitrary",)),
        name="combine",
    )(dest_flat, w, ys, h2p, x1, mod, wgs, wus, wds)


def _rope_tables(n):
    rows = n // GRID_W
    row = jnp.broadcast_to(jnp.arange(rows)[:, None], (rows, GRID_W)).reshape(-1)
    col = jnp.broadcast_to(jnp.arange(GRID_W)[None, :], (rows, GRID_W)).reshape(-1)
    freqs = ROPE_BASE ** (-jnp.arange(ROPE_FREQS, dtype=F32) / ROPE_FREQS)
    pos = jnp.stack([row, col], axis=-1).astype(F32)
    ang = pos[:, :, None] * freqs
    cos, sin = jnp.cos(ang), jnp.sin(ang)
    cos64 = jnp.concatenate([cos[:, 0], cos[:, 0], cos[:, 1], cos[:, 1]], axis=-1)
    sin64 = jnp.concatenate([-sin[:, 0], sin[:, 0], -sin[:, 1], sin[:, 1]], axis=-1)
    reps = MXU_N // HEAD_DIM
    return jnp.tile(cos64, (1, reps)), jnp.tile(sin64, (1, reps))


def _gain_rows(qn_a, kn_a, qn_b, kn_b):
    reps = MXU_N // HEAD_DIM
    ones = jnp.ones((LANES,), F32)
    kb = jnp.tile(kn_b, reps)
    qa = jnp.tile(qn_a, reps) * ATTN_SCALE
    qb = jnp.tile(qn_b, reps) * (ATTN_SCALE * LOG2E)
    mixed = jnp.concatenate([jnp.tile(kn_a, LANES // HEAD_DIM), ones])
    lat_g = jnp.stack([kb] * 4 + [qa] * 4 + [qb] * 4 + [mixed])[:, None, :]
    ctx_g = jnp.stack([kb] * 4 + [mixed])[:, None, :]
    return lat_g, ctx_g


def kernel(x, c, ctx, c_ctx, w_ada, b_ada, norm1_g, norm2_g, w_in, qnorm_a, knorm_a, sink_a, qnorm_b, knorm_b,
           lam_q1, lam_k1, lam_q2, lam_k2, subln_g, w_pa, w_pb, w_o, w_router, router_bias, w_gate_e, w_up_e,
           w_down_e, w_gate_s, w_up_s, w_down_s):
    b, s, d = x.shape
    nctx = ctx.shape[1]
    assert w_ada.shape[0] == 1 and d == 1024 and s % BLOCK == 0 and s % GRID_W == 0
    t = b * s

    cc = jnp.concatenate([c, c_ctx[None, :], jnp.zeros((-(b + 1) % SUBLANES, d), F32)], axis=0)
    mod_all = _ada(cc, w_ada[0], b_ada[0])
    mod = mod_all[:b].reshape(b, 6, d)
    mod_c = jnp.broadcast_to(mod_all[b].reshape(1, 6, d), (b, 6, d))

    w = w_in[0]
    ka_w, va_w = w[:, 0:128], w[:, 128:256]
    kb_w, vb_w = w[:, 256:1280], w[:, 1280:2304]
    qa_w, qb_w = w[:, 2304:3328], w[:, 3328:4352]
    gate_w = w[:, 4352:6400].astype(BF16)
    w_lat = jnp.concatenate([kb_w, qa_w, qb_w, ka_w, va_w, vb_w], axis=1).astype(BF16)
    w_ctx = jnp.concatenate([kb_w, ka_w, va_w, vb_w], axis=1).astype(BF16)

    lat_g, ctx_g = _gain_rows(qnorm_a[0], knorm_a[0], qnorm_b[0], knorm_b[0])
    hid = jnp.arange(MXU_N) // HEAD_DIM
    seg = jnp.where(hid[:, None] == hid[None, :], 1.0 / HEAD_DIM, 0.0).astype(BF16)
    cos_t, sin_t = _rope_tables(s)
    n1 = norm1_g[0].reshape(1, d)
    n2 = norm2_g[0].reshape(1, d)

    lat, vt = _proj(x, mod, n1, w_lat, lat_g, seg, cos_t, sin_t, nqk=LAT_NQK, use_rope=True)
    ctxp, vtx = _proj(ctx, mod_c, n1, w_ctx, ctx_g, seg, cos_t[:nctx], sin_t[:nctx], nqk=CTX_NQK, use_rope=False)

    oa = _attn_a(sink_a[0], lat, ctxp)
    lamv = jnp.stack([lam_q1[0], lam_k1[0], lam_q2[0], lam_k2[0]])
    bound = (HEAD_DIM * ATTN_SCALE * LOG2E * 1.01) * jnp.max(jnp.abs(qnorm_b[0])) * jnp.max(jnp.abs(knorm_b[0]))
    ob = _attn_b(bound.reshape(1), lamv, subln_g[0].reshape(LANES, 1), lat, vt, ctxp, vtx)

    x1, h2p, logits = _merge(x, mod, n1, n2, oa, ob, gate_w, w_pa[0].astype(BF16), w_pb[0].astype(BF16),
                             w_o[0].astype(BF16), w_router[0].astype(BF16))
    x1 = x1.reshape(t, d)
    h2p = h2p.reshape(t, d // 2)
    logits = logits.reshape(t, N_EXPERTS)

    idx_t, wts_t, rank_t, counts = _router(logits, router_bias[0])
    wts = wts_t.T

    cnt = counts[:, 0].astype(I32)
    padded = (cnt + MOE_BLOCK - 1) // MOE_BLOCK * MOE_BLOCK
    pend = jnp.cumsum(padded)
    pstart = pend - padded
    nblk = -(-(t * TOP_K) // MOE_BLOCK) + N_EXPERTS
    blk_row0 = jnp.arange(nblk, dtype=I32) * MOE_BLOCK
    blk_expert = jnp.minimum(jnp.sum((pend[None, :] <= blk_row0[:, None]).astype(I32), axis=1), N_EXPERTS - 1)
    blk_valid = jnp.clip(pstart[blk_expert] + cnt[blk_expert] - blk_row0, 0, MOE_BLOCK).astype(I32)

    dest = _dest(idx_t, rank_t, pstart.astype(F32)).T.reshape(t * TOP_K)
    xs = _dispatch(dest, h2p, nblk * MOE_BLOCK)
    blk_src = jnp.minimum(jnp.arange(nblk, dtype=I32), pend[-1] // MOE_BLOCK - 1)
    ys = _experts(blk_expert[blk_src], blk_valid, blk_src, xs, w_gate_e[0], w_up_e[0], w_down_e[0])
    out = _combine(dest, wts, ys, h2p, x1, mod, w_gate_s[0].astype(BF16), w_up_s[0].astype(BF16),
                   w_down_s[0].astype(BF16), s)
    return out.reshape(b, s, d)
```
